```python
import math
import jax, jax.numpy as jnp
from jax import lax
import numpy as np

D_MODEL = 2048
BATCH = 4
SEQ = 4096
DEPTH = 2

HEAD_DIM = 128
N_GROUPS = 4
GROUP_HEADS = D_MODEL // HEAD_DIM // N_GROUPS
GROUP_WIDTH = GROUP_HEADS * HEAD_DIM
Q_BLOCK = 128
ROPE_THETA = 500000.0
ROT_FRAC_DEN = 4
EPS = 1e-6

DIFF_SUB_DIM = HEAD_DIM // 2
MLA_Q_RANK = 448
MLA_KV_RANK = 128
MLA_NOPE_DIM = 128
MLA_ROPE_DIM = 64
MLA_V_DIM = HEAD_DIM
IDX_HEADS = 16
IDX_DIM = 64
TOPK_MAX = 256

IN_SIZES = ((GROUP_WIDTH,) * 3
            + (MLA_Q_RANK, MLA_KV_RANK, MLA_ROPE_DIM)
            + (GROUP_WIDTH,) * 3
            + (GROUP_WIDTH,) * 3
            + (IDX_HEADS * IDX_DIM, IDX_DIM, IDX_HEADS))
IN_WIDTH = sum(IN_SIZES)
IN_SPLITS = tuple(int(v) for v in np.cumsum(IN_SIZES)[:-1])

FFN_HIDDEN = ((8 * D_MODEL // 3 + 255) // 256) * 256

kernel_name = "hymba_style_diff_mla_stickbreak_dsa_block"


def _rms(x, g):
    xf = x.astype(jnp.float32)
    y = xf * lax.rsqrt(jnp.mean(xf * xf, axis=-1, keepdims=True) + EPS)
    return (y * g.astype(jnp.float32)).astype(x.dtype)


def _rotary(x, pos, rot_dim):
    half = rot_dim // 2
    inv = ROPE_THETA ** (-jnp.arange(half, dtype=jnp.float32) * 2.0 / rot_dim)
    ang = pos.astype(jnp.float32)[:, None] * inv[None, :]
    cos = jnp.cos(ang)[None, :, None, :]
    sin = jnp.sin(ang)[None, :, None, :]
    xf = x.astype(jnp.float32)
    x1 = xf[..., :half]
    x2 = xf[..., half:rot_dim]
    out = jnp.concatenate([x1 * cos - x2 * sin, x2 * cos + x1 * sin, xf[..., rot_dim:]], axis=-1)
    return out.astype(x.dtype)


def _sweep(fn, *q_side):
    B, S = q_side[0].shape[:2]
    nb = S // Q_BLOCK
    xs = tuple(jnp.moveaxis(a.reshape(B, nb, Q_BLOCK, *a.shape[2:]), 1, 0) for a in q_side)
    starts = jnp.arange(nb, dtype=jnp.int32) * Q_BLOCK
    out = lax.map(lambda args: fn(args[0], *args[1]), (starts, xs))
    return jnp.moveaxis(out, 0, 1).reshape(B, S, *out.shape[3:])


def _causal_softmax(q0, qb, k, scale):
    T, S = qb.shape[1], k.shape[1]
    s = jnp.einsum('bthd,bshd->bhts', qb, k).astype(jnp.float32) * scale
    mask = jnp.arange(S)[None, :] <= (q0 + jnp.arange(T))[:, None]
    return jax.nn.softmax(jnp.where(mask, s, -jnp.inf), axis=-1)


def _diff_attention(qp, kp, vp, pos, qk_norm, lam_vecs, subln, layer_idx):
    B, S, _ = qp.shape
    rot = DIFF_SUB_DIM // ROT_FRAC_DEN
    q = _rotary(_rms(qp.reshape(B, S, GROUP_HEADS * 2, DIFF_SUB_DIM), qk_norm[0]), pos, rot)
    k = _rotary(_rms(kp.reshape(B, S, GROUP_HEADS * 2, DIFF_SUB_DIM), qk_norm[1]), pos, rot)
    q = q.reshape(B, S, GROUP_HEADS, 2, DIFF_SUB_DIM)
    k = k.reshape(B, S, GROUP_HEADS, 2, DIFF_SUB_DIM)
    q1, q2 = q[..., 0, :], q[..., 1, :]
    k1, k2 = k[..., 0, :], k[..., 1, :]
    v = vp.reshape(B, S, GROUP_HEADS, HEAD_DIM)
    lam_init = 0.8 - 0.6 * math.exp(-0.3 * layer_idx)
    lv = lam_vecs.astype(jnp.float32)
    lam = jnp.exp(jnp.sum(lv[0] * lv[1])) - jnp.exp(jnp.sum(lv[2] * lv[3])) + lam_init
    scale = DIFF_SUB_DIM ** -0.5

    def block(q0, q1b, q2b):
        a = _causal_softmax(q0, q1b, k1, scale) - lam * _causal_softmax(q0, q2b, k2, scale)
        return jnp.einsum('bhts,bshd->bthd', a.astype(v.dtype), v)

    o = _sweep(block, q1, q2)
    o = _rms(o, subln) * (1.0 - lam_init)
    return o.reshape(B, S, GROUP_WIDTH)


def _mla(cq, ckv, kr, pos, q_a_norm, wq_b, kv_a_norm, wkv_b, qk_norm):
    B, S, _ = cq.shape
    qd = MLA_NOPE_DIM + MLA_ROPE_DIM
    q = (_rms(cq, q_a_norm) @ wq_b).reshape(B, S, GROUP_HEADS, qd)
    kv = (_rms(ckv, kv_a_norm) @ wkv_b).reshape(B, S, GROUP_HEADS, MLA_NOPE_DIM + MLA_V_DIM)
    k_nope, v = kv[..., :MLA_NOPE_DIM], kv[..., MLA_NOPE_DIM:]
    k_rope = jnp.broadcast_to(kr[:, :, None, :], (B, S, GROUP_HEADS, MLA_ROPE_DIM))
    k = jnp.concatenate([k_nope, k_rope], axis=-1)
    q = _rms(q, qk_norm[0])
    k = _rms(k, qk_norm[1])
    q = jnp.concatenate([q[..., :MLA_NOPE_DIM], _rotary(q[..., MLA_NOPE_DIM:], pos, MLA_ROPE_DIM)], axis=-1)
    k = jnp.concatenate([k[..., :MLA_NOPE_DIM], _rotary(k[..., MLA_NOPE_DIM:], pos, MLA_ROPE_DIM)], axis=-1)
    scale = qd ** -0.5

    def block(q0, qb):
        p = _causal_softmax(q0, qb, k, scale)
        return jnp.einsum('bhts,bshd->bthd', p.astype(v.dtype), v)

    return _sweep(block, q).reshape(B, S, GROUP_WIDTH)


def _stick_breaking(qp, kp, vp):
    B, S, _ = qp.shape
    q = qp.reshape(B, S, GROUP_HEADS, HEAD_DIM)
    k = kp.reshape(B, S, GROUP_HEADS, HEAD_DIM)
    v = vp.reshape(B, S, GROUP_HEADS, HEAD_DIM)
    scale = HEAD_DIM ** -0.5

    def block(q0, qb):
        T = qb.shape[1]
        z = jnp.einsum('bthd,bshd->bhts', qb, k).astype(jnp.float32) * scale
        mask = jnp.arange(S)[None, :] < (q0 + jnp.arange(T))[:, None]
        log_1mb = jnp.where(mask, jax.nn.log_sigmoid(-z), 0.0)
        suffix = lax.cumsum(log_1mb, axis=3, reverse=True) - log_1mb
        a = jnp.where(mask, jnp.exp(jax.nn.log_sigmoid(z) + suffix), 0.0)
        return jnp.einsum('bhts,bshd->bthd', a.astype(v.dtype), v)

    return _sweep(block, q).reshape(B, S, GROUP_WIDTH)


def _dsa(qp, kp, vp, iqp, ikp, iwp, pos, qk_norm, idx_k_norm):
    B, S, _ = qp.shape
    rot = HEAD_DIM // ROT_FRAC_DEN
    irot = IDX_DIM // ROT_FRAC_DEN
    q = _rotary(_rms(qp.reshape(B, S, GROUP_HEADS, HEAD_DIM), qk_norm[0]), pos, rot)
    k = _rotary(_rms(kp.reshape(B, S, GROUP_HEADS, HEAD_DIM), qk_norm[1]), pos, rot)
    v = vp.reshape(B, S, GROUP_HEADS, HEAD_DIM)
    iq = _rotary(iqp.reshape(B, S, IDX_HEADS, IDX_DIM), pos, irot)
    ik = _rotary(_rms(ikp, idx_k_norm)[:, :, None, :], pos, irot)[:, :, 0, :]
    topk = min(TOPK_MAX, S // 4)
    scale = HEAD_DIM ** -0.5
    gather = jax.vmap(lambda arr, ids: arr[ids])

    def block(q0, qb, iqb, iwb):
        T = qb.shape[1]
        tpos = q0 + jnp.arange(T)
        isc = jnp.einsum('bthd,bsd->bths', iqb, ik).astype(jnp.float32) * (IDX_DIM ** -0.5)
        isc = jnp.einsum('bths,bth->bts', jax.nn.relu(isc), iwb.astype(jnp.float32)) * (IDX_HEADS ** -0.5)
        admissible = jnp.arange(S)[None, :] <= tpos[:, None]
        isc = jnp.where(admissible[None], isc, -jnp.inf)
        _, idx = lax.top_k(isc, topk)
        valid = idx <= tpos[None, :, None]
        ksel = gather(k, idx)
        vsel = gather(v, idx)
        s = jnp.einsum('bthd,btkhd->bhtk', qb, ksel).astype(jnp.float32) * scale
        p = jax.nn.softmax(jnp.where(valid[:, None], s, -jnp.inf), axis=-1)
        return jnp.einsum('bhtk,btkhd->bthd', p.astype(vsel.dtype), vsel)

    return _sweep(block, q, iq, iwp).reshape(B, S, GROUP_WIDTH)


def setup_inputs(seed: int = 0) -> dict:
    key = jax.random.key(seed)
    ks = jax.random.split(key, 20)

    def nrm(k, shape, scale):
        return jax.random.normal(k, shape, jnp.float32) * scale

    def gain(k, shape):
        return 1.0 + 0.02 * jax.random.normal(k, shape, jnp.float32)

    L = DEPTH
    return {
        "x": nrm(ks[0], (BATCH, SEQ, D_MODEL), 1.0),
        "attn_norm": gain(ks[1], (L, D_MODEL)),
        "w_in": nrm(ks[2], (L, D_MODEL, IN_WIDTH), D_MODEL ** -0.5),
        "diff_qk_norm": gain(ks[3], (L, 2, DIFF_SUB_DIM)),
        "diff_lambda": nrm(ks[4], (L, 4, DIFF_SUB_DIM), 0.1),
        "diff_subln": gain(ks[5], (L, HEAD_DIM)),
        "mla_q_a_norm": gain(ks[6], (L, MLA_Q_RANK)),
        "mla_wq_b": nrm(ks[7], (L, MLA_Q_RANK, GROUP_HEADS * (MLA_NOPE_DIM + MLA_ROPE_DIM)), MLA_Q_RANK ** -0.5),
        "mla_kv_a_norm": gain(ks[8], (L, MLA_KV_RANK)),
        "mla_wkv_b": nrm(ks[9], (L, MLA_KV_RANK, GROUP_HEADS * (MLA_NOPE_DIM + MLA_V_DIM)), MLA_KV_RANK ** -0.5),
        "mla_qk_norm": gain(ks[10], (L, 2, MLA_NOPE_DIM + MLA_ROPE_DIM)),
        "dsa_qk_norm": gain(ks[11], (L, 2, HEAD_DIM)),
        "idx_k_norm": gain(ks[12], (L, IDX_DIM)),
        "w_o": nrm(ks[13], (L, N_GROUPS * GROUP_WIDTH, D_MODEL), (N_GROUPS * GROUP_WIDTH) ** -0.5),
        "ffn_norm": gain(ks[14], (L, D_MODEL)),
        "w_gate": nrm(ks[15], (L, D_MODEL, FFN_HIDDEN), D_MODEL ** -0.5),
        "w_up": nrm(ks[16], (L, D_MODEL, FFN_HIDDEN), D_MODEL ** -0.5),
        "w_down": nrm(ks[17], (L, FFN_HIDDEN, D_MODEL), FFN_HIDDEN ** -0.5),
    }


def reference(x, attn_norm, w_in, diff_qk_norm, diff_lambda, diff_subln, mla_q_a_norm, mla_wq_b,
              mla_kv_a_norm, mla_wkv_b, mla_qk_norm, dsa_qk_norm, idx_k_norm, w_o, ffn_norm,
              w_gate, w_up, w_down):
    S = x.shape[1]
    pos = jnp.arange(S, dtype=jnp.int32)
    for l in range(DEPTH):
        h = _rms(x, attn_norm[l])
        (a_q, a_k, a_v, b_cq, b_ckv, b_kr, c_q, c_k, c_v,
         d_q, d_k, d_v, d_iq, d_ik, d_iw) = jnp.split(h @ w_in[l], IN_SPLITS, axis=-1)
        o_a = _diff_attention(a_q, a_k, a_v, pos, diff_qk_norm[l], diff_lambda[l], diff_subln[l], l)
        o_b = _mla(b_cq, b_ckv, b_kr[:, :, None, :][:, :, 0, :], pos, mla_q_a_norm[l], mla_wq_b[l],
                   mla_kv_a_norm[l], mla_wkv_b[l], mla_qk_norm[l])
        o_c = _stick_breaking(c_q, c_k, c_v)
        o_d = _dsa(d_q, d_k, d_v, d_iq, d_ik, d_iw, pos, dsa_qk_norm[l], idx_k_norm[l])
        mixed = jnp.concatenate([o_a, o_b, o_c, o_d], axis=-1)
        x = x + mixed @ w_o[l]
        h = _rms(x, ffn_norm[l])
        x = x + (jax.nn.silu(h @ w_gate[l]) * (h @ w_up[l])) @ w_down[l]
    return x
```

```python
import functools
import math

import jax
import jax.numpy as jnp
from jax import lax
from jax.experimental import pallas as pl
from jax.experimental.pallas import tpu as pltpu

F32 = jnp.float32
BF16 = jnp.bfloat16

HEAD_DIM = 128
GROUP_HEADS = 4
GROUP_WIDTH = GROUP_HEADS * HEAD_DIM
ROPE_THETA = 500000.0
EPS = 1e-6
DIFF_SUB_DIM = 64
MLA_Q_RANK = 448
MLA_KV_RANK = 128
MLA_NOPE_DIM = 128
MLA_ROPE_DIM = 64
MLA_QK_DIM = MLA_NOPE_DIM + MLA_ROPE_DIM
MLA_PAD_DIM = 256
IDX_HEADS = 16
IDX_DIM = 64
TOPK_MAX = 256

LANES = 128
Y_WIDTH = 13 * GROUP_WIDTH
NEG = -1e30
INT_MIN = -2147483648
VMEM_LIMIT = 56 * 1024 * 1024

_OFF = dict(a_q=0, a_k=512, a_v=1024, b_cq=1536, b_ckv=2048, b_kr=2176, d_ik=2304, d_iw=2432,
            c_q=2560, c_k=3072, c_v=3584, d_q=4096, d_k=4608, d_v=5120, d_iq=5632)


def _cparams(n_axes):
    return pltpu.CompilerParams(dimension_semantics=("arbitrary",) * n_axes, vmem_limit_bytes=VMEM_LIMIT)


def _dot(a, b):
    return jnp.dot(a, b, preferred_element_type=F32)


def _dot_nt(a, b):
    return lax.dot_general(a, b, (((1,), (1,)), ((), ())), preferred_element_type=F32)


def _rms_rows(x, g):
    ms = jnp.mean(x * x, axis=-1, keepdims=True)
    return x * lax.rsqrt(ms + EPS) * g


def _rms_matmul_kernel(x_ref, g_ref, w_ref, o_ref, h_ref):
    @pl.when(pl.program_id(1) == 0)
    def _():
        h_ref[...] = _rms_rows(x_ref[...], g_ref[...]).astype(BF16)

    o_ref[...] = _dot(h_ref[...], w_ref[...]).astype(o_ref.dtype)


def _rms_matmul(x, g, w, *, tm, tn, out_dtype):
    n, d = x.shape
    m = w.shape[1]
    return pl.pallas_call(
        _rms_matmul_kernel,
        grid=(n // tm, m // tn),
        in_specs=[pl.BlockSpec((tm, d), lambda i, j: (i, 0)),
                  pl.BlockSpec((1, d), lambda i, j: (0, 0)),
                  pl.BlockSpec((d, tn), lambda i, j: (0, j))],
        out_specs=pl.BlockSpec((tm, tn), lambda i, j: (i, j)),
        out_shape=jax.ShapeDtypeStruct((n, m), out_dtype),
        scratch_shapes=[pltpu.VMEM((tm, d), BF16)],
        compiler_params=_cparams(2),
        name="rms_matmul",
    )(x, g, w)


def _ffn_up_kernel(x_ref, g_ref, wg_ref, wu_ref, o_ref, h_ref):
    @pl.when(pl.program_id(1) == 0)
    def _():
        h_ref[...] = _rms_rows(x_ref[...], g_ref[...]).astype(BF16)

    h = h_ref[...]
    a = _dot(h, wg_ref[...])
    b = _dot(h, wu_ref[...])
    o_ref[...] = (jax.nn.silu(a) * b).astype(o_ref.dtype)


def _ffn_up(x, g, wg, wu, *, tm, tn):
    n, d = x.shape
    m = wg.shape[1]
    return pl.pallas_call(
        _ffn_up_kernel,
        grid=(n // tm, m // tn),
        in_specs=[pl.BlockSpec((tm, d), lambda i, j: (i, 0)),
                  pl.BlockSpec((1, d), lambda i, j: (0, 0)),
                  pl.BlockSpec((d, tn), lambda i, j: (0, j)),
                  pl.BlockSpec((d, tn), lambda i, j: (0, j))],
        out_specs=pl.BlockSpec((tm, tn), lambda i, j: (i, j)),
        out_shape=jax.ShapeDtypeStruct((n, m), BF16),
        scratch_shapes=[pltpu.VMEM((tm, d), BF16)],
        compiler_params=_cparams(2),
        name="ffn_up",
    )(x, g, wg, wu)


def _matmul_residual_kernel(a_ref, w_ref, r_ref, o_ref):
    o_ref[...] = r_ref[...] + _dot(a_ref[...], w_ref[...])


def _matmul_residual(a, w, r, *, tm, tn):
    n, k = a.shape
    m = w.shape[1]
    return pl.pallas_call(
        _matmul_residual_kernel,
        grid=(n // tm, m // tn),
        in_specs=[pl.BlockSpec((tm, k), lambda i, j: (i, 0)),
                  pl.BlockSpec((k, tn), lambda i, j: (0, j)),
                  pl.BlockSpec((tm, tn), lambda i, j: (i, j))],
        out_specs=pl.BlockSpec((tm, tn), lambda i, j: (i, j)),
        out_shape=jax.ShapeDtypeStruct((n, m), F32),
        compiler_params=_cparams(2),
        name="matmul_residual",
    )(a, w, r)


def _rope(x, t_ref, half):
    n = x.shape[1]
    return (x * t_ref[0] + pltpu.roll(x, n - half, 1) * t_ref[1] + pltpu.roll(x, half, 1) * t_ref[2])


def _rms_lane_groups(x, width, count):
    outs = []
    for h in range(x.shape[1] // width):
        xh = x[:, h * width:(h + 1) * width]
        ms = jnp.sum(xh * xh, axis=-1, keepdims=True) * (1.0 / count)
        outs.append(xh * lax.rsqrt(ms + EPS))
    return outs[0] if len(outs) == 1 else jnp.concatenate(outs, axis=1)


def _rms_sub64(x, ones_ref):
    x2 = x * x
    hi = x2.astype(BF16)
    lo = (x2 - hi.astype(F32)).astype(BF16)
    ms = (_dot(hi, ones_ref[...]) + _dot(lo, ones_ref[...])) * (1.0 / DIFF_SUB_DIM)
    return x * lax.rsqrt(ms + EPS)


def _prep_kernel(y_ref, ta_ref, tb_ref, td_ref, ones_ref, gaq_ref, gak_ref, gcq_ref, gckv_ref, gbq_ref, gbk_ref,
                 gdq_ref, gdk_ref, gik_ref, wq_ref, wkv_ref,
                 qa_ref, ka_ref, va_ref, qb_ref, kb_ref, vb_ref, qc_ref, kc_ref, vc_ref,
                 qd_ref, kd_ref, vd_ref, iq_ref, ik_ref, iw_ref):
    def sec(name, width):
        return y_ref[:, _OFF[name]:_OFF[name] + width]

    tm = y_ref.shape[0]
    lane = lax.broadcasted_iota(jnp.int32, (tm, LANES), 1)
    first_half = lane < DIFF_SUB_DIM

    qa = _rope(_rms_sub64(sec("a_q", 512), ones_ref) * gaq_ref[...], ta_ref, 8) * (DIFF_SUB_DIM ** -0.5)
    for h in range(GROUP_HEADS):
        qh = qa[:, h * LANES:(h + 1) * LANES]
        qa_ref[:, (2 * h) * LANES:(2 * h + 1) * LANES] = jnp.where(first_half, qh, 0.0).astype(BF16)
        qa_ref[:, (2 * h + 1) * LANES:(2 * h + 2) * LANES] = jnp.where(first_half, 0.0, qh).astype(BF16)
    ka_ref[...] = _rope(_rms_sub64(sec("a_k", 512), ones_ref) * gak_ref[...], ta_ref, 8).astype(BF16)
    va_ref[...] = sec("a_v", 512).astype(BF16)

    cq = sec("b_cq", 512)
    cq = cq * lax.rsqrt(jnp.sum(cq * cq, axis=-1, keepdims=True) * (1.0 / MLA_Q_RANK) + EPS) * gcq_ref[...]
    qb = _dot(cq.astype(BF16), wq_ref[...])
    qb = _rms_lane_groups(qb, MLA_PAD_DIM, MLA_QK_DIM) * gbq_ref[...]
    ckv = _rms_lane_groups(sec("b_ckv", 128), 128, MLA_KV_RANK) * gckv_ref[...]
    kv = _dot(ckv.astype(BF16), wkv_ref[...])
    kr = sec("b_kr", 128)
    kr_ss = jnp.sum(kr * kr, axis=-1, keepdims=True)
    for h in range(GROUP_HEADS):
        lo, hi = h * MLA_PAD_DIM, (h + 1) * MLA_PAD_DIM
        qb_ref[:, lo:hi] = (_rope(qb[:, lo:hi], tb_ref, 32) * (MLA_QK_DIM ** -0.5)).astype(BF16)
        kn = kv[:, h * LANES:(h + 1) * LANES]
        ms = (jnp.sum(kn * kn, axis=-1, keepdims=True) + kr_ss) * (1.0 / MLA_QK_DIM)
        kh = jnp.concatenate([kn, kr], axis=1) * lax.rsqrt(ms + EPS) * gbk_ref[...]
        kb_ref[:, lo:hi] = _rope(kh, tb_ref, 32).astype(BF16)
    vb_ref[...] = kv[:, GROUP_WIDTH:].astype(BF16)

    qc_ref[...] = (sec("c_q", 512) * (HEAD_DIM ** -0.5)).astype(BF16)
    kc_ref[...] = sec("c_k", 512).astype(BF16)
    vc_ref[...] = sec("c_v", 512).astype(BF16)

    qd = _rope(_rms_lane_groups(sec("d_q", 512), HEAD_DIM, HEAD_DIM) * gdq_ref[...], td_ref, 16)
    qd_ref[...] = (qd * (HEAD_DIM ** -0.5)).astype(BF16)
    kd_ref[...] = _rope(_rms_lane_groups(sec("d_k", 512), HEAD_DIM, HEAD_DIM) * gdk_ref[...], td_ref, 16).astype(BF16)
    vd_ref[...] = sec("d_v", 512).astype(BF16)
    for half in range(2):
        iq = y_ref[:, _OFF["d_iq"] + half * 512:_OFF["d_iq"] + (half + 1) * 512]
        iq_ref[:, half * 512:(half + 1) * 512] = (_rope(iq, ta_ref, 8) * (IDX_DIM ** -0.5)).astype(BF16)
    ik = _rms_lane_groups(sec("d_ik", 128), 128, IDX_DIM) * gik_ref[...]
    ik = (ik * ta_ref[0, :, :LANES] + pltpu.roll(ik, LANES - 8, 1) * ta_ref[1, :, :LANES]
          + pltpu.roll(ik, 8, 1) * ta_ref[2, :, :LANES])
    ik_ref[:, :LANES] = ik.astype(BF16)
    ik_ref[:, LANES:] = pltpu.roll(ik, IDX_DIM, 1).astype(BF16)
    iw_ref[...] = sec("d_iw", 128) * (IDX_HEADS ** -0.5)


def _prep(y, seq, tabs, ones64, gains, wq, wkv, *, tm):
    n = y.shape[0]
    nblk_seq = seq // tm
    ta, tb, td = tabs

    def rows(width):
        return pl.BlockSpec((tm, width), lambda i: (i, 0))

    def table(width):
        return pl.BlockSpec((3, tm, width), lambda i: (0, i % nblk_seq, 0))

    def whole(a):
        return pl.BlockSpec(a.shape, lambda i: (0,) * a.ndim)

    widths = [1024, 512, 512, 1024, 1024, 512, 512, 512, 512, 512, 512, 512, 1024, 256]
    out_shape = [jax.ShapeDtypeStruct((n, w), BF16) for w in widths] + [jax.ShapeDtypeStruct((n, LANES), F32)]
    out_specs = [rows(w) for w in widths] + [rows(LANES)]
    return pl.pallas_call(
        _prep_kernel,
        grid=(n // tm,),
        in_specs=[rows(Y_WIDTH), table(512), table(256), table(512), whole(ones64)]
                 + [whole(g) for g in gains] + [whole(wq), whole(wkv)],
        out_specs=out_specs,
        out_shape=out_shape,
        compiler_params=_cparams(1),
        name="prep",
    )(y, ta, tb, td, ones64, *gains, wq, wkv)


def _softmax_step(s, v, m_ref, l_ref, acc_ref, idx):
    m_prev = m_ref[idx]
    m_new = jnp.maximum(m_prev, jnp.max(s, axis=-1, keepdims=True))
    p = jnp.exp(s - m_new)
    alpha = jnp.exp(m_prev - m_new)
    l_ref[idx] = alpha * l_ref[idx] + jnp.sum(p, axis=-1, keepdims=True)
    acc_ref[idx] = alpha * acc_ref[idx] + _dot(p.astype(BF16), v)
    m_ref[idx] = m_new


def _init_softmax(m_ref, l_ref, acc_ref):
    m_ref[...] = jnp.full(m_ref.shape, NEG, F32)
    l_ref[...] = jnp.zeros(l_ref.shape, F32)
    acc_ref[...] = jnp.zeros(acc_ref.shape, F32)


def _attn_diff_kernel(lam_ref, sub_ref, q_ref, k_ref, v_ref, o_ref, m_ref, l_ref, acc_ref, *, tq, tk, lam_init):
    qi, kj = pl.program_id(1), pl.program_id(2)
    last = (qi * tq + tq - 1) // tk

    @pl.when(kj == 0)
    def _():
        _init_softmax(m_ref, l_ref, acc_ref)

    @pl.when(kj <= last)
    def _():
        rows = qi * tq + lax.broadcasted_iota(jnp.int32, (tq, tk), 0)
        cols = kj * tk + lax.broadcasted_iota(jnp.int32, (tq, tk), 1)
        mask = cols <= rows
        for mp in range(2 * GROUP_HEADS):
            h = mp // 2
            q = q_ref[0, :, mp * LANES:(mp + 1) * LANES]
            k = k_ref[0, :, h * LANES:(h + 1) * LANES]
            v = v_ref[0, :, h * LANES:(h + 1) * LANES]
            s = jnp.where(mask, _dot_nt(q, k), NEG)
            _softmax_step(s, v, m_ref, l_ref, acc_ref, mp)

    @pl.when(kj == last)
    def _():
        lv = lam_ref[...]
        lam = (jnp.exp(jnp.sum(lv[0:1] * lv[1:2], axis=-1, keepdims=True))
               - jnp.exp(jnp.sum(lv[2:3] * lv[3:4], axis=-1, keepdims=True)) + lam_init)
        for h in range(GROUP_HEADS):
            o = acc_ref[2 * h] / l_ref[2 * h] - lam * (acc_ref[2 * h + 1] / l_ref[2 * h + 1])
            o = _rms_rows(o, sub_ref[...]) * (1.0 - lam_init)
            o_ref[0, :, h * LANES:(h + 1) * LANES] = o.astype(o_ref.dtype)


def _attn_mla_kernel(q_ref, k_ref, v_ref, o_ref, m_ref, l_ref, acc_ref, *, tq, tk):
    qi, kj = pl.program_id(1), pl.program_id(2)
    last = (qi * tq + tq - 1) // tk

    @pl.when(kj == 0)
    def _():
        _init_softmax(m_ref, l_ref, acc_ref)

    @pl.when(kj <= last)
    def _():
        rows = qi * tq + lax.broadcasted_iota(jnp.int32, (tq, tk), 0)
        cols = kj * tk + lax.broadcasted_iota(jnp.int32, (tq, tk), 1)
        mask = cols <= rows
        for h in range(GROUP_HEADS):
            q = q_ref[0, :, h * MLA_PAD_DIM:(h + 1) * MLA_PAD_DIM]
            k = k_ref[0, :, h * MLA_PAD_DIM:(h + 1) * MLA_PAD_DIM]
            v = v_ref[0, :, h * LANES:(h + 1) * LANES]
            s = jnp.where(mask, _dot_nt(q, k), NEG)
            _softmax_step(s, v, m_ref, l_ref, acc_ref, h)

    @pl.when(kj == last)
    def _():
        for h in range(GROUP_HEADS):
            o_ref[0, :, h * LANES:(h + 1) * LANES] = (acc_ref[h] / l_ref[h]).astype(o_ref.dtype)


def _attn_stick_kernel(tri_ref, q_ref, k_ref, v_ref, o_ref, carry_ref, acc_ref, *, tq, tk):
    qi, kj = pl.program_id(1), pl.program_id(2)
    last = (qi * tq + tq - 1) // tk
    kb = last - kj

    @pl.when(kj == 0)
    def _():
        carry_ref[...] = jnp.zeros(carry_ref.shape, F32)
        acc_ref[...] = jnp.zeros(acc_ref.shape, F32)

    @pl.when(kj <= last)
    def _():
        rows = qi * tq + lax.broadcasted_iota(jnp.int32, (tq, tk), 0)
        cols = kb * tk + lax.broadcasted_iota(jnp.int32, (tq, tk), 1)
        mask = cols < rows
        tri = tri_ref[...]
        for h in range(GROUP_HEADS):
            q = q_ref[0, :, h * LANES:(h + 1) * LANES]
            k = k_ref[0, :, h * LANES:(h + 1) * LANES]
            v = v_ref[0, :, h * LANES:(h + 1) * LANES]
            z = _dot_nt(q, k)
            log_b = jnp.minimum(z, 0.0) - jnp.log1p(jnp.exp(-jnp.abs(z)))
            log_1mb = jnp.where(mask, log_b - z, 0.0)
            hi = log_1mb.astype(BF16)
            lo = (log_1mb - hi.astype(F32)).astype(BF16)
            suffix = _dot(hi, tri) + _dot(lo, tri) + carry_ref[h]
            a = jnp.where(mask, jnp.exp(log_b + suffix), 0.0)
            acc_ref[h] += _dot(a.astype(BF16), v)
            carry_ref[h] += jnp.sum(log_1mb, axis=-1, keepdims=True)

    @pl.when(kj == last)
    def _():
        for h in range(GROUP_HEADS):
            o_ref[0, :, h * LANES:(h + 1) * LANES] = acc_ref[h].astype(o_ref.dtype)


def _causal_attention(kernel, q, k, v, extra, *, tq, tk, reverse, scratch, name):
    b, s, _ = q.shape
    nq, nk = s // tq, s // tk

    def last(qi):
        return (qi * tq + tq - 1) // tk

    if reverse:
        kv_map = lambda bi, qi, kj: (bi, jnp.maximum(last(qi) - kj, 0), 0)
    else:
        kv_map = lambda bi, qi, kj: (bi, jnp.minimum(kj, last(qi)), 0)
    q_map = lambda bi, qi, kj: (bi, qi, 0)
    extra_specs = [pl.BlockSpec(e.shape, lambda bi, qi, kj, nd=e.ndim: (0,) * nd) for e in extra]
    return pl.pallas_call(
        functools.partial(kernel, tq=tq, tk=tk),
        grid=(b, nq, nk),
        in_specs=extra_specs + [pl.BlockSpec((1, tq, q.shape[2]), q_map),
                                pl.BlockSpec((1, tk, k.shape[2]), kv_map),
                                pl.BlockSpec((1, tk, v.shape[2]), kv_map)],
        out_specs=pl.BlockSpec((1, tq, GROUP_WIDTH), q_map),
        out_shape=jax.ShapeDtypeStruct((b, s, GROUP_WIDTH), BF16),
        scratch_shapes=scratch,
        compiler_params=_cparams(3),
        name=name,
    )(*extra, q, k, v)


def _attn_dsa_kernel(q_ref, k_ref, v_ref, iq_ref, ik_ref, iw_ref, o_ref, key_ref, m_ref, l_ref, acc_ref,
                     *, tq, tkc, topk):
    qi = pl.program_id(1)
    nkc = (qi * tq + tq + tkc - 1) // tkc
    rows = qi * tq + lax.broadcasted_iota(jnp.int32, (tq, tkc), 0)
    col0 = lax.broadcasted_iota(jnp.int32, (tq, tkc), 1)

    def score_chunk(c, _):
        off = pl.multiple_of(c * tkc, tkc)
        ikc = ik_ref[0, pl.ds(off, tkc), :]
        iw = iw_ref[0]
        isc = jnp.zeros((tq, tkc), F32)
        for hp in range(IDX_HEADS // 2):
            iqp = iq_ref[0, :, hp * LANES:(hp + 1) * LANES]
            for e in range(2):
                hh = 2 * hp + e
                s = _dot_nt(iqp, ikc[:, e * LANES:(e + 1) * LANES])
                isc = isc + jnp.maximum(s, 0.0) * iw[:, hh:hh + 1]
        bits = lax.bitcast_convert_type(isc, jnp.int32)
        key = jnp.where(bits < 0, bits ^ jnp.int32(0x7FFFFFFF), bits)
        key_ref[c] = jnp.where(off + col0 <= rows, key, INT_MIN)
        return 0

    lax.fori_loop(0, nkc, score_chunk, 0)

    def search_bit(i, carry):
        thr_u, cnt_ge = carry
        cand_u = thr_u | lax.shift_left(jnp.int32(1), 31 - i)
        cand_s = cand_u ^ jnp.int32(INT_MIN)

        def count_chunk(c, part):
            ge = jnp.where(key_ref[c] >= cand_s, 1.0, 0.0)
            for g in range(tkc // LANES):
                part = part + ge[:, g * LANES:(g + 1) * LANES]
            return part

        part = lax.fori_loop(0, nkc, count_chunk, jnp.zeros((tq, LANES), F32))
        cnt = jnp.sum(part, axis=-1, keepdims=True)
        take = cnt >= topk
        return jnp.where(take, cand_u, thr_u), jnp.where(take, cnt, cnt_ge)

    zero = jnp.zeros((tq, 1), jnp.int32)
    thr_u, cnt_ge = lax.fori_loop(0, 32, search_bit, (zero, jnp.zeros((tq, 1), F32)))
    thr = jnp.maximum(thr_u ^ jnp.int32(INT_MIN), jnp.int32(INT_MIN + 1))
    del cnt_ge

    _init_softmax(m_ref, l_ref, acc_ref)

    def attend_chunk(c, _):
        off = pl.multiple_of(c * tkc, tkc)
        sel = key_ref[c] >= thr
        for h in range(GROUP_HEADS):
            q = q_ref[0, :, h * LANES:(h + 1) * LANES]
            k = k_ref[0, pl.ds(off, tkc), h * LANES:(h + 1) * LANES]
            v = v_ref[0, pl.ds(off, tkc), h * LANES:(h + 1) * LANES]
            s = jnp.where(sel, _dot_nt(q, k), NEG)
            _softmax_step(s, v, m_ref, l_ref, acc_ref, h)
        return 0

    lax.fori_loop(0, nkc, attend_chunk, 0)
    for h in range(GROUP_HEADS):
        o_ref[0, :, h * LANES:(h + 1) * LANES] = (acc_ref[h] / l_ref[h]).astype(o_ref.dtype)


def _attn_dsa(q, k, v, iq, ik, iw, *, tq, tkc):
    b, s, _ = q.shape
    topk = min(TOPK_MAX, s // 4)
    q_map = lambda bi, qi: (bi, qi, 0)
    all_map = lambda bi, qi: (bi, 0, 0)
    return pl.pallas_call(
        functools.partial(_attn_dsa_kernel, tq=tq, tkc=tkc, topk=topk),
        grid=(b, s // tq),
        in_specs=[pl.BlockSpec((1, tq, GROUP_WIDTH), q_map),
                  pl.BlockSpec((1, s, GROUP_WIDTH), all_map),
                  pl.BlockSpec((1, s, GROUP_WIDTH), all_map),
                  pl.BlockSpec((1, tq, IDX_HEADS * IDX_DIM), q_map),
                  pl.BlockSpec((1, s, 2 * LANES), all_map),
                  pl.BlockSpec((1, tq, LANES), q_map)],
        out_specs=pl.BlockSpec((1, tq, GROUP_WIDTH), q_map),
        out_shape=jax.ShapeDtypeStruct((b, s, GROUP_WIDTH), BF16),
        scratch_shapes=[pltpu.VMEM((s // tkc, tq, tkc), jnp.int32),
                        pltpu.VMEM((GROUP_HEADS, tq, 1), F32),
                        pltpu.VMEM((GROUP_HEADS, tq, 1), F32),
                        pltpu.VMEM((GROUP_HEADS, tq, LANES), F32)],
        compiler_params=_cparams(2),
        name="attn_dsa",
    )(q, k, v, iq, ik, iw)


def _rope_tables(seq, ncols, group, start, rot):
    half = rot // 2
    inv = ROPE_THETA ** (-jnp.arange(half, dtype=F32) * 2.0 / rot)
    j = jnp.arange(ncols) % group - start
    in_lo = (j >= 0) & (j < half)
    in_hi = (j >= half) & (j < rot)
    idx = jnp.clip(jnp.where(in_hi, j - half, j), 0, half - 1)
    ang = jnp.arange(seq, dtype=jnp.int32).astype(F32)[:, None] * inv[idx][None, :]
    cos, sin = jnp.cos(ang), jnp.sin(ang)
    return jnp.stack([jnp.where(in_lo | in_hi, cos, 1.0), jnp.where(in_lo, -sin, 0.0), jnp.where(in_hi, sin, 0.0)])


def _pad_cols(a, width):
    return jnp.pad(a, ((0, 0), (0, width - a.shape[1])))


def _relayout_w_in(w):
    sizes = (512,) * 3 + (MLA_Q_RANK, MLA_KV_RANK, MLA_ROPE_DIM) + (512,) * 6 + (IDX_HEADS * IDX_DIM, IDX_DIM, IDX_HEADS)
    names = ("a_q", "a_k", "a_v", "b_cq", "b_ckv", "b_kr", "c_q", "c_k", "c_v", "d_q", "d_k", "d_v", "d_iq", "d_ik", "d_iw")
    parts, start = {}, 0
    for nme, sz in zip(names, sizes):
        parts[nme] = w[:, start:start + sz]
        start += sz
    order = sorted(_OFF, key=_OFF.get)
    ends = [_OFF[nme] for nme in order[1:]] + [Y_WIDTH]
    return jnp.concatenate([_pad_cols(parts[nme], end - _OFF[nme]) for nme, end in zip(order, ends)], axis=1).astype(BF16)


def _relayout_mla_q(w):
    w = w.reshape(w.shape[0], GROUP_HEADS, MLA_QK_DIM)
    w = jnp.pad(w, ((0, 0), (0, 0), (0, MLA_PAD_DIM - MLA_QK_DIM)))
    return w.reshape(w.shape[0], GROUP_HEADS * MLA_PAD_DIM)


def _tile_row(g, reps):
    return jnp.tile(g.reshape(1, -1), (1, reps))


def kernel(x, attn_norm, w_in, diff_qk_norm, diff_lambda, diff_subln, mla_q_a_norm, mla_wq_b, mla_kv_a_norm,
           mla_wkv_b, mla_qk_norm, dsa_qk_norm, idx_k_norm, w_o, ffn_norm, w_gate, w_up, w_down):
    b, s, d = x.shape
    n = b * s
    depth = w_in.shape[0]
    tm = min(1024, n)
    tq = min(256, s)
    tabs = (_rope_tables(s, 512, DIFF_SUB_DIM, 0, DIFF_SUB_DIM // 4),
            _rope_tables(s, MLA_PAD_DIM, MLA_PAD_DIM, MLA_NOPE_DIM, MLA_ROPE_DIM),
            _rope_tables(s, 512, HEAD_DIM, 0, HEAD_DIM // 4))
    ones64 = jnp.kron(jnp.eye(512 // DIFF_SUB_DIM, dtype=F32), jnp.ones((DIFF_SUB_DIM, DIFF_SUB_DIM), F32)).astype(BF16)
    tk_stick = min(256, s)
    tri = (jnp.arange(tk_stick)[:, None] > jnp.arange(tk_stick)[None, :]).astype(BF16)

    xf = x.reshape(n, d)
    for l in range(depth):
        lam_init = 0.8 - 0.6 * math.exp(-0.3 * l)
        y = _rms_matmul(xf, attn_norm[l].reshape(1, d), _relayout_w_in(w_in[l]), tm=tm, tn=512, out_dtype=F32)

        wq = jnp.pad(_relayout_mla_q(mla_wq_b[l]), ((0, 512 - MLA_Q_RANK), (0, 0))).astype(BF16)
        wkv = mla_wkv_b[l].reshape(MLA_KV_RANK, GROUP_HEADS, 2, HEAD_DIM).transpose(0, 2, 1, 3)
        wkv = wkv.reshape(MLA_KV_RANK, 2 * GROUP_WIDTH).astype(BF16)
        gains = [
            _tile_row(diff_qk_norm[l, 0], 8), _tile_row(diff_qk_norm[l, 1], 8),
            _pad_cols(mla_q_a_norm[l].reshape(1, -1), 512), mla_kv_a_norm[l].reshape(1, -1),
            _relayout_mla_q(mla_qk_norm[l, 0].reshape(1, MLA_QK_DIM).repeat(GROUP_HEADS, 0).reshape(1, -1)),
            _pad_cols(mla_qk_norm[l, 1].reshape(1, -1), MLA_PAD_DIM),
            _tile_row(dsa_qk_norm[l, 0], 4), _tile_row(dsa_qk_norm[l, 1], 4),
            _pad_cols(idx_k_norm[l].reshape(1, -1), LANES),
        ]
        (qa, ka, va, qb, kb, vb, qc, kc, vc, qd, kd, vd, iq, ik, iw) = [
            a.reshape(b, s, a.shape[1]) for a in _prep(y, s, tabs, ones64, gains, wq, wkv, tm=min(256, s))]

        o_a = _causal_attention(
            functools.partial(_attn_diff_kernel, lam_init=lam_init), qa, ka, va,
            [diff_lambda[l], diff_subln[l].reshape(1, HEAD_DIM)], tq=tq, tk=min(512, s), reverse=False,
            scratch=[pltpu.VMEM((8, tq, 1), F32), pltpu.VMEM((8, tq, 1), F32), pltpu.VMEM((8, tq, LANES), F32)],
            name="attn_diff")
        o_b = _causal_attention(
            _attn_mla_kernel, qb, kb, vb, [], tq=tq, tk=min(512, s), reverse=False,
            scratch=[pltpu.VMEM((4, tq, 1), F32), pltpu.VMEM((4, tq, 1), F32), pltpu.VMEM((4, tq, LANES), F32)],
            name="attn_mla")
        o_c = _causal_attention(
            _attn_stick_kernel, qc, kc, vc, [tri], tq=tq, tk=tk_stick, reverse=True,
            scratch=[pltpu.VMEM((4, tq, 1), F32), pltpu.VMEM((4, tq, LANES), F32)],
            name="attn_stick")
        o_d = _attn_dsa(qd, kd, vd, iq, ik, iw, tq=tq, tkc=min(512, s))

        mixed = jnp.concatenate([o_a, o_b, o_c, o_d], axis=-1).reshape(n, 4 * GROUP_WIDTH)
        xf = _matmul_residual(mixed, w_o[l].astype(BF16), xf, tm=tm, tn=512)
        act = _ffn_up(xf, ffn_norm[l].reshape(1, d), w_gate[l].astype(BF16), w_up[l].astype(BF16), tm=tm, tn=512)
        xf = _matmul_residual(act, w_down[l].astype(BF16), xf, tm=min(512, n), tn=512)
    return xf.reshape(b, s, d)
```

```python
import functools
import math

import jax
import jax.numpy as jnp
from jax import lax
from jax.experimental import pallas as pl
from jax.experimental.pallas import tpu as pltpu

F32 = jnp.float32
BF16 = jnp.bfloat16

HEAD_DIM = 128
GROUP_HEADS = 4
GROUP_WIDTH = GROUP_HEADS * HEAD_DIM
ROPE_THETA = 500000.0
EPS = 1e-6
DIFF_SUB_DIM = 64
MLA_Q_RANK = 448
MLA_KV_RANK = 128
MLA_NOPE_DIM = 128
MLA_ROPE_DIM = 64
MLA_QK_DIM = MLA_NOPE_DIM + MLA_ROPE_DIM
MLA_PAD_DIM = 256
IDX_HEADS = 16
IDX_DIM = 64
TOPK_MAX = 256

LANES = 128
VEXT = 2 * LANES
Y_WIDTH = 13 * GROUP_WIDTH
NEG = -1e30
INT_MIN = -2147483648
LOG2E = 1.4426950408889634
VMEM_LIMIT = 56 * 1024 * 1024

_OFF = dict(a_q=0, a_k=512, a_v=1024, b_cq=1536, b_ckv=2048, b_kr=2176, d_ik=2304, d_iw=2432,
            c_q=2560, c_k=3072, c_v=3584, d_q=4096, d_k=4608, d_v=5120, d_iq=5632)


def _cparams(n_axes):
    return pltpu.CompilerParams(dimension_semantics=("arbitrary",) * n_axes, vmem_limit_bytes=VMEM_LIMIT)


def _dot(a, b):
    return jnp.dot(a, b, preferred_element_type=F32)


def _dot_nt(a, b):
    return lax.dot_general(a, b, (((1,), (1,)), ((), ())), preferred_element_type=F32)


def _rms_rows(x, g):
    ms = jnp.mean(x * x, axis=-1, keepdims=True)
    return x * lax.rsqrt(ms + EPS) * g


def _rms_matmul_kernel(x_ref, g_ref, w_ref, o_ref, h_ref):
    @pl.when(pl.program_id(1) == 0)
    def _():
        h_ref[...] = _rms_rows(x_ref[...], g_ref[...]).astype(BF16)

    o_ref[...] = _dot(h_ref[...], w_ref[...]).astype(o_ref.dtype)


def _rms_matmul(x, g, w, *, tm, tn, out_dtype):
    n, d = x.shape
    m = w.shape[1]
    return pl.pallas_call(
        _rms_matmul_kernel,
        grid=(n // tm, m // tn),
        in_specs=[pl.BlockSpec((tm, d), lambda i, j: (i, 0)),
                  pl.BlockSpec((1, d), lambda i, j: (0, 0)),
                  pl.BlockSpec((d, tn), lambda i, j: (0, j))],
        out_specs=pl.BlockSpec((tm, tn), lambda i, j: (i, j)),
        out_shape=jax.ShapeDtypeStruct((n, m), out_dtype),
        scratch_shapes=[pltpu.VMEM((tm, d), BF16)],
        compiler_params=_cparams(2),
        name="rms_matmul",
    )(x, g, w)


def _ffn_up_kernel(x_ref, g_ref, wg_ref, wu_ref, o_ref, h_ref):
    @pl.when(pl.program_id(1) == 0)
    def _():
        h_ref[...] = _rms_rows(x_ref[...], g_ref[...]).astype(BF16)

    h = h_ref[...]
    a = _dot(h, wg_ref[...])
    b = _dot(h, wu_ref[...])
    o_ref[...] = (jax.nn.silu(a) * b).astype(o_ref.dtype)


def _ffn_up(x, g, wg, wu, *, tm, tn):
    n, d = x.shape
    m = wg.shape[1]
    return pl.pallas_call(
        _ffn_up_kernel,
        grid=(n // tm, m // tn),
        in_specs=[pl.BlockSpec((tm, d), lambda i, j: (i, 0)),
                  pl.BlockSpec((1, d), lambda i, j: (0, 0)),
                  pl.BlockSpec((d, tn), lambda i, j: (0, j)),
                  pl.BlockSpec((d, tn), lambda i, j: (0, j))],
        out_specs=pl.BlockSpec((tm, tn), lambda i, j: (i, j)),
        out_shape=jax.ShapeDtypeStruct((n, m), BF16),
        scratch_shapes=[pltpu.VMEM((tm, d), BF16)],
        compiler_params=_cparams(2),
        name="ffn_up",
    )(x, g, wg, wu)


def _matmul_residual_kernel(a_ref, w_ref, r_ref, o_ref):
    o_ref[...] = r_ref[...] + _dot(a_ref[...], w_ref[...])


def _matmul_residual(a, w, r, *, tm, tn):
    n, k = a.shape
    m = w.shape[1]
    return pl.pallas_call(
        _matmul_residual_kernel,
        grid=(n // tm, m // tn),
        in_specs=[pl.BlockSpec((tm, k), lambda i, j: (i, 0)),
                  pl.BlockSpec((k, tn), lambda i, j: (0, j)),
                  pl.BlockSpec((tm, tn), lambda i, j: (i, j))],
        out_specs=pl.BlockSpec((tm, tn), lambda i, j: (i, j)),
        out_shape=jax.ShapeDtypeStruct((n, m), F32),
        compiler_params=_cparams(2),
        name="matmul_residual",
    )(a, w, r)


def _rope(x, t_ref, half):
    n = x.shape[1]
    return (x * t_ref[0] + pltpu.roll(x, n - half, 1) * t_ref[1] + pltpu.roll(x, half, 1) * t_ref[2])


def _rms_lane_groups(x, width, count):
    outs = []
    for h in range(x.shape[1] // width):
        xh = x[:, h * width:(h + 1) * width]
        ms = jnp.sum(xh * xh, axis=-1, keepdims=True) * (1.0 / count)
        outs.append(xh * lax.rsqrt(ms + EPS))
    return outs[0] if len(outs) == 1 else jnp.concatenate(outs, axis=1)


def _rms_sub64(x, ones_ref):
    x2 = x * x
    hi = x2.astype(BF16)
    lo = (x2 - hi.astype(F32)).astype(BF16)
    ms = (_dot(hi, ones_ref[...]) + _dot(lo, ones_ref[...])) * (1.0 / DIFF_SUB_DIM)
    return x * lax.rsqrt(ms + EPS)


def _store_vext(ref, v):
    ones = jnp.ones((v.shape[0], LANES), BF16)
    for h in range(GROUP_HEADS):
        ref[:, h * VEXT:h * VEXT + LANES] = v[:, h * LANES:(h + 1) * LANES].astype(BF16)
        ref[:, h * VEXT + LANES:(h + 1) * VEXT] = ones


def _prep_kernel(y_ref, ta_ref, tb_ref, td_ref, ones_ref, gaq_ref, gak_ref, gcq_ref, gckv_ref, gbq_ref, gbk_ref,
                 gdq_ref, gdk_ref, gik_ref, wq_ref, wkv_ref,
                 qa_ref, ka_ref, va_ref, qb_ref, kb_ref, vb_ref, qc_ref, kc_ref, vc_ref,
                 qd_ref, kd_ref, vd_ref, iq_ref, ik_ref, iw_ref):
    def sec(name, width):
        return y_ref[:, _OFF[name]:_OFF[name] + width]

    tm = y_ref.shape[0]
    lane = lax.broadcasted_iota(jnp.int32, (tm, LANES), 1)
    first_half = lane < DIFF_SUB_DIM

    qa = _rope(_rms_sub64(sec("a_q", 512), ones_ref) * gaq_ref[...], ta_ref, 8) * (DIFF_SUB_DIM ** -0.5 * LOG2E)
    for h in range(GROUP_HEADS):
        qh = qa[:, h * LANES:(h + 1) * LANES]
        qa_ref[:, (2 * h) * LANES:(2 * h + 1) * LANES] = jnp.where(first_half, qh, 0.0).astype(BF16)
        qa_ref[:, (2 * h + 1) * LANES:(2 * h + 2) * LANES] = jnp.where(first_half, 0.0, qh).astype(BF16)
    ka_ref[...] = _rope(_rms_sub64(sec("a_k", 512), ones_ref) * gak_ref[...], ta_ref, 8).astype(BF16)
    _store_vext(va_ref, sec("a_v", 512))

    cq = sec("b_cq", 512)
    cq = cq * lax.rsqrt(jnp.sum(cq * cq, axis=-1, keepdims=True) * (1.0 / MLA_Q_RANK) + EPS) * gcq_ref[...]
    qb = _dot(cq.astype(BF16), wq_ref[...])
    qb = _rms_lane_groups(qb, MLA_PAD_DIM, MLA_QK_DIM) * gbq_ref[...]
    ckv = _rms_lane_groups(sec("b_ckv", 128), 128, MLA_KV_RANK) * gckv_ref[...]
    kv = _dot(ckv.astype(BF16), wkv_ref[...])
    kr = sec("b_kr", 128)
    kr_ss = jnp.sum(kr * kr, axis=-1, keepdims=True)
    for h in range(GROUP_HEADS):
        lo, hi = h * MLA_PAD_DIM, (h + 1) * MLA_PAD_DIM
        qb_ref[:, lo:hi] = (_rope(qb[:, lo:hi], tb_ref, 32) * (MLA_QK_DIM ** -0.5 * LOG2E)).astype(BF16)
        kn = kv[:, h * LANES:(h + 1) * LANES]
        ms = (jnp.sum(kn * kn, axis=-1, keepdims=True) + kr_ss) * (1.0 / MLA_QK_DIM)
        kh = jnp.concatenate([kn, kr], axis=1) * lax.rsqrt(ms + EPS) * gbk_ref[...]
        kb_ref[:, lo:hi] = _rope(kh, tb_ref, 32).astype(BF16)
    _store_vext(vb_ref, kv[:, GROUP_WIDTH:])

    qc_ref[...] = (sec("c_q", 512) * (HEAD_DIM ** -0.5 * LOG2E)).astype(BF16)
    kc_ref[...] = sec("c_k", 512).astype(BF16)
    vc_ref[...] = sec("c_v", 512).astype(BF16)

    qd = _rope(_rms_lane_groups(sec("d_q", 512), HEAD_DIM, HEAD_DIM) * gdq_ref[...], td_ref, 16)
    qd_ref[...] = (qd * (HEAD_DIM ** -0.5 * LOG2E)).astype(BF16)
    kd_ref[...] = _rope(_rms_lane_groups(sec("d_k", 512), HEAD_DIM, HEAD_DIM) * gdk_ref[...], td_ref, 16).astype(BF16)
    _store_vext(vd_ref, sec("d_v", 512))
    for half in range(2):
        iq = y_ref[:, _OFF["d_iq"] + half * 512:_OFF["d_iq"] + (half + 1) * 512]
        iq_ref[:, half * 512:(half + 1) * 512] = (_rope(iq, ta_ref, 8) * (IDX_DIM ** -0.5)).astype(BF16)
    ik = _rms_lane_groups(sec("d_ik", 128), 128, IDX_DIM) * gik_ref[...]
    ik = (ik * ta_ref[0, :, :LANES] + pltpu.roll(ik, LANES - 8, 1) * ta_ref[1, :, :LANES]
          + pltpu.roll(ik, 8, 1) * ta_ref[2, :, :LANES])
    ik_ref[:, :LANES] = ik.astype(BF16)
    ik_ref[:, LANES:] = pltpu.roll(ik, IDX_DIM, 1).astype(BF16)
    iw_ref[...] = sec("d_iw", 128) * (IDX_HEADS ** -0.5)


def _prep(y, seq, tabs, ones64, gains, wq, wkv, *, tm):
    n = y.shape[0]
    nblk_seq = seq // tm
    ta, tb, td = tabs

    def rows(width):
        return pl.BlockSpec((tm, width), lambda i: (i, 0))

    def table(width):
        return pl.BlockSpec((3, tm, width), lambda i: (0, i % nblk_seq, 0))

    def whole(a):
        return pl.BlockSpec(a.shape, lambda i: (0,) * a.ndim)

    widths = [1024, 512, 1024, 1024, 1024, 1024, 512, 512, 512, 512, 512, 1024, 1024, 256]
    out_shape = [jax.ShapeDtypeStruct((n, w), BF16) for w in widths] + [jax.ShapeDtypeStruct((n, LANES), F32)]
    out_specs = [rows(w) for w in widths] + [rows(LANES)]
    return pl.pallas_call(
        _prep_kernel,
        grid=(n // tm,),
        in_specs=[rows(Y_WIDTH), table(512), table(256), table(512), whole(ones64)]
                 + [whole(g) for g in gains] + [whole(wq), whole(wkv)],
        out_specs=out_specs,
        out_shape=out_shape,
        compiler_params=_cparams(1),
        name="prep",
    )(y, ta, tb, td, ones64, *gains, wq, wkv)


def _softmax_chunk(s, vext, m_ref, acc_ref, idx):
    m_prev = m_ref[idx]
    m_new = jnp.maximum(m_prev, jnp.max(s, axis=1, keepdims=True))
    p = jnp.exp2(s - jnp.tile(m_new, (1, s.shape[1] // LANES)))
    alpha = jnp.exp2(m_prev - m_new)
    acc_ref[idx] = jnp.tile(alpha, (1, VEXT // LANES)) * acc_ref[idx] + _dot(p.astype(BF16), vext)
    m_ref[idx] = m_new


def _init_softmax(m_ref, acc_ref):
    m_ref[...] = jnp.full(m_ref.shape, NEG, F32)
    acc_ref[...] = jnp.zeros(acc_ref.shape, F32)


def _normalised(acc_ref, idx):
    acc = acc_ref[idx]
    return acc[:, :LANES] / acc[:, LANES:]


def _causal_mask(row0, col0, tq, tkc, strict=False):
    rows = row0 + lax.broadcasted_iota(jnp.int32, (tq, tkc), 0)
    cols = col0 + lax.broadcasted_iota(jnp.int32, (tq, tkc), 1)
    return cols < rows if strict else cols <= rows


def _attn_diff_kernel(lam_ref, sub_ref, q_ref, k_ref, v_ref, o_ref, m_ref, acc_ref, *, tq, lam_init):
    qi = pl.program_id(1)
    _init_softmax(m_ref, acc_ref)

    def chunk(off, mask):
        for mp in range(2 * GROUP_HEADS):
            h = mp // 2
            q = q_ref[0, :, mp * LANES:(mp + 1) * LANES]
            k = k_ref[0, pl.ds(off, tq), h * LANES:(h + 1) * LANES]
            v = v_ref[0, pl.ds(off, tq), h * VEXT:(h + 1) * VEXT]
            s = _dot_nt(q, k)
            if mask is not None:
                s = jnp.where(mask, s, NEG)
            _softmax_chunk(s, v, m_ref, acc_ref, mp)

    def full_chunk(c, carry):
        chunk(pl.multiple_of(c * tq, tq), None)
        return carry

    lax.fori_loop(0, qi, full_chunk, 0)
    diag = pl.multiple_of(qi * tq, tq)
    chunk(diag, _causal_mask(0, 0, tq, tq))

    lv = lam_ref[...]
    lam = (jnp.exp(jnp.sum(lv[0:1] * lv[1:2], axis=-1, keepdims=True))
           - jnp.exp(jnp.sum(lv[2:3] * lv[3:4], axis=-1, keepdims=True)) + lam_init)
    for h in range(GROUP_HEADS):
        o = _normalised(acc_ref, 2 * h) - lam * _normalised(acc_ref, 2 * h + 1)
        o = _rms_rows(o, sub_ref[...]) * (1.0 - lam_init)
        o_ref[0, :, h * LANES:(h + 1) * LANES] = o.astype(o_ref.dtype)


def _attn_mla_kernel(q_ref, k_ref, v_ref, o_ref, m_ref, acc_ref, *, tq):
    qi = pl.program_id(1)
    _init_softmax(m_ref, acc_ref)

    def chunk(off, mask):
        for h in range(GROUP_HEADS):
            q = q_ref[0, :, h * MLA_PAD_DIM:(h + 1) * MLA_PAD_DIM]
            k = k_ref[0, pl.ds(off, tq), h * MLA_PAD_DIM:(h + 1) * MLA_PAD_DIM]
            v = v_ref[0, pl.ds(off, tq), h * VEXT:(h + 1) * VEXT]
            s = _dot_nt(q, k)
            if mask is not None:
                s = jnp.where(mask, s, NEG)
            _softmax_chunk(s, v, m_ref, acc_ref, h)

    def full_chunk(c, carry):
        chunk(pl.multiple_of(c * tq, tq), None)
        return carry

    lax.fori_loop(0, qi, full_chunk, 0)
    chunk(pl.multiple_of(qi * tq, tq), _causal_mask(0, 0, tq, tq))
    for h in range(GROUP_HEADS):
        o_ref[0, :, h * LANES:(h + 1) * LANES] = _normalised(acc_ref, h).astype(o_ref.dtype)


def _attn_stick_kernel(tri_ref, q_ref, k_ref, v_ref, o_ref, carry_ref, acc_ref, *, tq, tkc):
    qi = pl.program_id(1)
    carry_ref[...] = jnp.zeros(carry_ref.shape, F32)
    acc_ref[...] = jnp.zeros(acc_ref.shape, F32)

    def chunk(off, mask):
        tri = tri_ref[...]
        for h in range(GROUP_HEADS):
            q = q_ref[0, :, h * LANES:(h + 1) * LANES]
            k = k_ref[0, pl.ds(off, tkc), h * LANES:(h + 1) * LANES]
            v = v_ref[0, pl.ds(off, tkc), h * LANES:(h + 1) * LANES]
            z = _dot_nt(q, k)
            log_b = jnp.minimum(z, 0.0) - jnp.log2(1.0 + jnp.exp2(-jnp.abs(z)))
            log_1mb = log_b - z
            if mask is not None:
                log_1mb = jnp.where(mask, log_1mb, 0.0)
            hi = log_1mb.astype(BF16)
            lo = (log_1mb - hi.astype(F32)).astype(BF16)
            inner = _dot(hi, tri) + _dot(lo, tri)
            carry = carry_ref[h]
            a = jnp.exp2(log_b + inner + jnp.tile(carry, (1, tkc // LANES)))
            if mask is not None:
                a = jnp.where(mask, a, 0.0)
            acc_ref[h] += _dot(a.astype(BF16), v)
            carry_ref[h] = carry + (inner[:, 0:1] + log_1mb[:, 0:1])

    n_diag = tq // tkc
    for j in range(n_diag):
        col0 = (n_diag - 1 - j) * tkc
        chunk(pl.multiple_of(qi * tq + col0, tkc), _causal_mask(0, col0, tq, tkc, strict=True))

    def full_chunk(c, carry):
        chunk(pl.multiple_of((qi * n_diag - 1 - c) * tkc, tkc), None)
        return carry

    lax.fori_loop(0, qi * n_diag, full_chunk, 0)
    for h in range(GROUP_HEADS):
        o_ref[0, :, h * LANES:(h + 1) * LANES] = acc_ref[h].astype(o_ref.dtype)


def _resident_attention(kernel, q, k, v, extra, *, tq, scratch, name):
    b, s, _ = q.shape
    q_map = lambda bi, qi: (bi, qi, 0)
    all_map = lambda bi, qi: (bi, 0, 0)
    extra_specs = [pl.BlockSpec(e.shape, lambda bi, qi, nd=e.ndim: (0,) * nd) for e in extra]
    return pl.pallas_call(
        kernel,
        grid=(b, s // tq),
        in_specs=extra_specs + [pl.BlockSpec((1, tq, q.shape[2]), q_map),
                                pl.BlockSpec((1, s, k.shape[2]), all_map),
                                pl.BlockSpec((1, s, v.shape[2]), all_map)],
        out_specs=pl.BlockSpec((1, tq, GROUP_WIDTH), q_map),
        out_shape=jax.ShapeDtypeStruct((b, s, GROUP_WIDTH), BF16),
        scratch_shapes=scratch,
        compiler_params=_cparams(2),
        name=name,
    )(*extra, q, k, v)


def _attn_dsa_kernel(q_ref, k_ref, v_ref, iq_ref, ik_ref, iw_ref, o_ref, key_ref, m_ref, acc_ref, *, tq, topk):
    qi = pl.program_id(1)
    nkc = qi + 1

    def score_chunk(c, mask):
        off = pl.multiple_of(c * tq, tq)
        ikc = ik_ref[0, pl.ds(off, tq), :]
        iw = iw_ref[0]
        isc = jnp.zeros((tq, tq), F32)
        for hp in range(IDX_HEADS // 2):
            iqp = iq_ref[0, :, hp * LANES:(hp + 1) * LANES]
            for e in range(2):
                hh = 2 * hp + e
                s = _dot_nt(iqp, ikc[:, e * LANES:(e + 1) * LANES])
                isc = isc + jnp.maximum(s, 0.0) * iw[:, hh:hh + 1]
        bits = lax.bitcast_convert_type(isc, jnp.int32)
        key = jnp.where(bits < 0, bits ^ jnp.int32(0x7FFFFFFF), bits)
        if mask is not None:
            key = jnp.where(mask, key, INT_MIN)
        key_ref[c] = key

    def full_score_chunk(c, carry):
        score_chunk(c, None)
        return carry

    lax.fori_loop(0, qi, full_score_chunk, 0)
    score_chunk(qi, _causal_mask(0, 0, tq, tq))

    def search_bit(i, thr_u):
        cand_u = thr_u | lax.shift_left(jnp.int32(1), 31 - i)
        cand_s = cand_u ^ jnp.int32(INT_MIN)

        def count_chunk(c, part):
            ge = jnp.where(key_ref[c] >= cand_s, 1.0, 0.0)
            for g in range(tq // LANES):
                part = part + ge[:, g * LANES:(g + 1) * LANES]
            return part

        part = lax.fori_loop(0, nkc, count_chunk, jnp.zeros((tq, LANES), F32))
        cnt = jnp.sum(part, axis=-1, keepdims=True)
        return jnp.where(cnt >= topk, cand_u, thr_u)

    thr_u = lax.fori_loop(0, 32, search_bit, jnp.zeros((tq, 1), jnp.int32))
    thr = jnp.maximum(thr_u ^ jnp.int32(INT_MIN), jnp.int32(INT_MIN + 1))

    _init_softmax(m_ref, acc_ref)

    def attend_chunk(c, carry):
        off = pl.multiple_of(c * tq, tq)
        sel = key_ref[c] >= thr
        for h in range(GROUP_HEADS):
            q = q_ref[0, :, h * LANES:(h + 1) * LANES]
            k = k_ref[0, pl.ds(off, tq), h * LANES:(h + 1) * LANES]
            v = v_ref[0, pl.ds(off, tq), h * VEXT:(h + 1) * VEXT]
            s = jnp.where(sel, _dot_nt(q, k), NEG)
            _softmax_chunk(s, v, m_ref, acc_ref, h)
        return carry

    lax.fori_loop(0, nkc, attend_chunk, 0)
    for h in range(GROUP_HEADS):
        o_ref[0, :, h * LANES:(h + 1) * LANES] = _normalised(acc_ref, h).astype(o_ref.dtype)


def _attn_dsa(q, k, v, iq, ik, iw, *, tq):
    b, s, _ = q.shape
    topk = min(TOPK_MAX, s // 4)
    q_map = lambda bi, qi: (bi, qi, 0)
    all_map = lambda bi, qi: (bi, 0, 0)
    return pl.pallas_call(
        functools.partial(_attn_dsa_kernel, tq=tq, topk=topk),
        grid=(b, s // tq),
        in_specs=[pl.BlockSpec((1, tq, GROUP_WIDTH), q_map),
                  pl.BlockSpec((1, s, GROUP_WIDTH), all_map),
                  pl.BlockSpec((1, s, GROUP_HEADS * VEXT), all_map),
                  pl.BlockSpec((1, tq, IDX_HEADS * IDX_DIM), q_map),
                  pl.BlockSpec((1, s, 2 * LANES), all_map),
                  pl.BlockSpec((1, tq, LANES), q_map)],
        out_specs=pl.BlockSpec((1, tq, GROUP_WIDTH), q_map),
        out_shape=jax.ShapeDtypeStruct((b, s, GROUP_WIDTH), BF16),
        scratch_shapes=[pltpu.VMEM((s // tq, tq, tq), jnp.int32),
                        pltpu.VMEM((GROUP_HEADS, tq, LANES), F32),
                        pltpu.VMEM((GROUP_HEADS, tq, VEXT), F32)],
        compiler_params=_cparams(2),
        name="attn_dsa",
    )(q, k, v, iq, ik, iw)


def _rope_tables(seq, ncols, group, start, rot):
    half = rot // 2
    inv = ROPE_THETA ** (-jnp.arange(half, dtype=F32) * 2.0 / rot)
    j = jnp.arange(ncols) % group - start
    in_lo = (j >= 0) & (j < half)
    in_hi = (j >= half) & (j < rot)
    idx = jnp.clip(jnp.where(in_hi, j - half, j), 0, half - 1)
    ang = jnp.arange(seq, dtype=jnp.int32).astype(F32)[:, None] * inv[idx][None, :]
    cos, sin = jnp.cos(ang), jnp.sin(ang)
    return jnp.stack([jnp.where(in_lo | in_hi, cos, 1.0), jnp.where(in_lo, -sin, 0.0), jnp.where(in_hi, sin, 0.0)])


def _pad_cols(a, width):
    return jnp.pad(a, ((0, 0), (0, width - a.shape[1])))


def _relayout_w_in(w):
    sizes = (512,) * 3 + (MLA_Q_RANK, MLA_KV_RANK, MLA_ROPE_DIM) + (512,) * 6 + (IDX_HEADS * IDX_DIM, IDX_DIM, IDX_HEADS)
    names = ("a_q", "a_k", "a_v", "b_cq", "b_ckv", "b_kr", "c_q", "c_k", "c_v", "d_q", "d_k", "d_v", "d_iq", "d_ik", "d_iw")
    parts, start = {}, 0
    for nme, sz in zip(names, sizes):
        parts[nme] = w[:, start:start + sz]
        start += sz
    order = sorted(_OFF, key=_OFF.get)
    ends = [_OFF[nme] for nme in order[1:]] + [Y_WIDTH]
    return jnp.concatenate([_pad_cols(parts[nme], end - _OFF[nme]) for nme, end in zip(order, ends)], axis=1).astype(BF16)


def _relayout_mla_q(w):
    w = w.reshape(w.shape[0], GROUP_HEADS, MLA_QK_DIM)
    w = jnp.pad(w, ((0, 0), (0, 0), (0, MLA_PAD_DIM - MLA_QK_DIM)))
    return w.reshape(w.shape[0], GROUP_HEADS * MLA_PAD_DIM)


def _tile_row(g, reps):
    return jnp.tile(g.reshape(1, -1), (1, reps))


def kernel(x, attn_norm, w_in, diff_qk_norm, diff_lambda, diff_subln, mla_q_a_norm, mla_wq_b, mla_kv_a_norm,
           mla_wkv_b, mla_qk_norm, dsa_qk_norm, idx_k_norm, w_o, ffn_norm, w_gate, w_up, w_down):
    b, s, d = x.shape
    n = b * s
    depth = w_in.shape[0]
    tm = min(1024, n)
    tq = min(512, s)
    tabs = (_rope_tables(s, 512, DIFF_SUB_DIM, 0, DIFF_SUB_DIM // 4),
            _rope_tables(s, MLA_PAD_DIM, MLA_PAD_DIM, MLA_NOPE_DIM, MLA_ROPE_DIM),
            _rope_tables(s, 512, HEAD_DIM, 0, HEAD_DIM // 4))
    ones64 = jnp.kron(jnp.eye(512 // DIFF_SUB_DIM, dtype=F32), jnp.ones((DIFF_SUB_DIM, DIFF_SUB_DIM), F32)).astype(BF16)
    tk_stick = min(256, s)
    tri = (jnp.arange(tk_stick)[:, None] > jnp.arange(tk_stick)[None, :]).astype(BF16)

    xf = x.reshape(n, d)
    for l in range(depth):
        lam_init = 0.8 - 0.6 * math.exp(-0.3 * l)
        y = _rms_matmul(xf, attn_norm[l].reshape(1, d), _relayout_w_in(w_in[l]), tm=tm, tn=512, out_dtype=F32)

        wq = jnp.pad(_relayout_mla_q(mla_wq_b[l]), ((0, 512 - MLA_Q_RANK), (0, 0))).astype(BF16)
        wkv = mla_wkv_b[l].reshape(MLA_KV_RANK, GROUP_HEADS, 2, HEAD_DIM).transpose(0, 2, 1, 3)
        wkv = wkv.reshape(MLA_KV_RANK, 2 * GROUP_WIDTH).astype(BF16)
        gains = [
            _tile_row(diff_qk_norm[l, 0], 8), _tile_row(diff_qk_norm[l, 1], 8),
            _pad_cols(mla_q_a_norm[l].reshape(1, -1), 512), mla_kv_a_norm[l].reshape(1, -1),
            _relayout_mla_q(mla_qk_norm[l, 0].reshape(1, MLA_QK_DIM).repeat(GROUP_HEADS, 0).reshape(1, -1)),
            _pad_cols(mla_qk_norm[l, 1].reshape(1, -1), MLA_PAD_DIM),
            _tile_row(dsa_qk_norm[l, 0], 4), _tile_row(dsa_qk_norm[l, 1], 4),
            _pad_cols(idx_k_norm[l].reshape(1, -1), LANES),
        ]
        (qa, ka, va, qb, kb, vb, qc, kc, vc, qd, kd, vd, iq, ik, iw) = [
            a.reshape(b, s, a.shape[1]) for a in _prep(y, s, tabs, ones64, gains, wq, wkv, tm=min(256, s))]

        o_a = _resident_attention(
            functools.partial(_attn_diff_kernel, tq=tq, lam_init=lam_init), qa, ka, va,
            [diff_lambda[l], diff_subln[l].reshape(1, HEAD_DIM)], tq=tq,
            scratch=[pltpu.VMEM((8, tq, LANES), F32), pltpu.VMEM((8, tq, VEXT), F32)], name="attn_diff")
        o_b = _resident_attention(
            functools.partial(_attn_mla_kernel, tq=tq), qb, kb, vb, [], tq=tq,
            scratch=[pltpu.VMEM((4, tq, LANES), F32), pltpu.VMEM((4, tq, VEXT), F32)], name="attn_mla")
        o_c = _resident_attention(
            functools.partial(_attn_stick_kernel, tq=tq, tkc=tk_stick), qc, kc, vc, [tri], tq=tq,
            scratch=[pltpu.VMEM((4, tq, LANES), F32), pltpu.VMEM((4, tq, LANES), F32)], name="attn_stick")
        o_d = _attn_dsa(qd, kd, vd, iq, ik, iw, tq=tq)

        mixed = jnp.concatenate([o_a, o_b, o_c, o_d], axis=-1).reshape(n, 4 * GROUP_WIDTH)
        xf = _matmul_residual(mixed, w_o[l].astype(BF16), xf, tm=tm, tn=512)
        act = _ffn_up(xf, ffn_norm[l].reshape(1, d), w_gate[l].astype(BF16), w_up[l].astype(BF16), tm=tm, tn=512)
        xf = _matmul_residual(act, w_down[l].astype(BF16), xf, tm=min(512, n), tn=512)
    return xf.reshape(b, s, d)
```

```python
import functools
import math

import jax
import jax.numpy as jnp
from jax import lax
from jax.experimental import pallas as pl
from jax.experimental.pallas import tpu as pltpu

F32 = jnp.float32
BF16 = jnp.bfloat16

HEAD_DIM = 128
GROUP_HEADS = 4
GROUP_WIDTH = GROUP_HEADS * HEAD_DIM
ROPE_THETA = 500000.0
EPS = 1e-6
DIFF_SUB_DIM = 64
MLA_Q_RANK = 448
MLA_KV_RANK = 128
MLA_NOPE_DIM = 128
MLA_ROPE_DIM = 64
MLA_QK_DIM = MLA_NOPE_DIM + MLA_ROPE_DIM
MLA_PAD_DIM = 256
IDX_HEADS = 16
IDX_DIM = 64
TOPK_MAX = 256

LANES = 128
VEXT = 2 * LANES
Y_WIDTH = 13 * GROUP_WIDTH
NEG = -1e30
INT_MIN = -2147483648
LOG2E = 1.4426950408889634
VMEM_LIMIT = 56 * 1024 * 1024

_OFF = dict(a_q=0, a_k=512, a_v=1024, b_cq=1536, b_ckv=2048, b_kr=2176, d_ik=2304, d_iw=2432,
            c_q=2560, c_k=3072, c_v=3584, d_q=4096, d_k=4608, d_v=5120, d_iq=5632)


def _cparams(n_axes):
    return pltpu.CompilerParams(dimension_semantics=("arbitrary",) * n_axes, vmem_limit_bytes=VMEM_LIMIT)


def _dot(a, b):
    return jnp.dot(a, b, preferred_element_type=F32)


def _dot_nt(a, b):
    return lax.dot_general(a, b, (((1,), (1,)), ((), ())), preferred_element_type=F32)


def _rms_rows(x, g):
    ms = jnp.mean(x * x, axis=-1, keepdims=True)
    return x * lax.rsqrt(ms + EPS) * g


def _rms_matmul_kernel(x_ref, g_ref, w_ref, o_ref, h_ref):
    @pl.when(pl.program_id(1) == 0)
    def _():
        h_ref[...] = _rms_rows(x_ref[...], g_ref[...]).astype(BF16)

    o_ref[...] = _dot(h_ref[...], w_ref[...]).astype(o_ref.dtype)


def _rms_matmul(x, g, w, *, tm, tn, out_dtype):
    n, d = x.shape
    m = w.shape[1]
    return pl.pallas_call(
        _rms_matmul_kernel,
        grid=(n // tm, m // tn),
        in_specs=[pl.BlockSpec((tm, d), lambda i, j: (i, 0)),
                  pl.BlockSpec((1, d), lambda i, j: (0, 0)),
                  pl.BlockSpec((d, tn), lambda i, j: (0, j))],
        out_specs=pl.BlockSpec((tm, tn), lambda i, j: (i, j)),
        out_shape=jax.ShapeDtypeStruct((n, m), out_dtype),
        scratch_shapes=[pltpu.VMEM((tm, d), BF16)],
        compiler_params=_cparams(2),
        name="rms_matmul",
    )(x, g, w)


def _ffn_up_kernel(x_ref, g_ref, wg_ref, wu_ref, o_ref, h_ref):
    @pl.when(pl.program_id(1) == 0)
    def _():
        h_ref[...] = _rms_rows(x_ref[...], g_ref[...]).astype(BF16)

    h = h_ref[...]
    a = _dot(h, wg_ref[...])
    b = _dot(h, wu_ref[...])
    o_ref[...] = (jax.nn.silu(a) * b).astype(o_ref.dtype)


def _ffn_up(x, g, wg, wu, *, tm, tn):
    n, d = x.shape
    m = wg.shape[1]
    return pl.pallas_call(
        _ffn_up_kernel,
        grid=(n // tm, m // tn),
        in_specs=[pl.BlockSpec((tm, d), lambda i, j: (i, 0)),
                  pl.BlockSpec((1, d), lambda i, j: (0, 0)),
                  pl.BlockSpec((d, tn), lambda i, j: (0, j)),
                  pl.BlockSpec((d, tn), lambda i, j: (0, j))],
        out_specs=pl.BlockSpec((tm, tn), lambda i, j: (i, j)),
        out_shape=jax.ShapeDtypeStruct((n, m), BF16),
        scratch_shapes=[pltpu.VMEM((tm, d), BF16)],
        compiler_params=_cparams(2),
        name="ffn_up",
    )(x, g, wg, wu)


def _matmul_residual_kernel(a_ref, w_ref, r_ref, o_ref):
    o_ref[...] = r_ref[...] + _dot(a_ref[...], w_ref[...])


def _matmul_residual(a, w, r, *, tm, tn):
    n, k = a.shape
    m = w.shape[1]
    return pl.pallas_call(
        _matmul_residual_kernel,
        grid=(n // tm, m // tn),
        in_specs=[pl.BlockSpec((tm, k), lambda i, j: (i, 0)),
                  pl.BlockSpec((k, tn), lambda i, j: (0, j)),
                  pl.BlockSpec((tm, tn), lambda i, j: (i, j))],
        out_specs=pl.BlockSpec((tm, tn), lambda i, j: (i, j)),
        out_shape=jax.ShapeDtypeStruct((n, m), F32),
        compiler_params=_cparams(2),
        name="matmul_residual",
    )(a, w, r)


def _rope(x, t_ref, half):
    n = x.shape[1]
    return (x * t_ref[0] + pltpu.roll(x, n - half, 1) * t_ref[1] + pltpu.roll(x, half, 1) * t_ref[2])


def _rms_lane_groups(x, width, count):
    outs = []
    for h in range(x.shape[1] // width):
        xh = x[:, h * width:(h + 1) * width]
        ms = jnp.sum(xh * xh, axis=-1, keepdims=True) * (1.0 / count)
        outs.append(xh * lax.rsqrt(ms + EPS))
    return outs[0] if len(outs) == 1 else jnp.concatenate(outs, axis=1)


def _rms_sub64(x, ones_ref):
    x2 = x * x
    hi = x2.astype(BF16)
    lo = (x2 - hi.astype(F32)).astype(BF16)
    ms = (_dot(hi, ones_ref[...]) + _dot(lo, ones_ref[...])) * (1.0 / DIFF_SUB_DIM)
    return x * lax.rsqrt(ms + EPS)


def _store_vext(ref, v):
    ones = jnp.ones((v.shape[0], LANES), BF16)
    for h in range(GROUP_HEADS):
        ref[:, h * VEXT:h * VEXT + LANES] = v[:, h * LANES:(h + 1) * LANES].astype(BF16)
        ref[:, h * VEXT + LANES:(h + 1) * VEXT] = ones


def _prep_kernel(y_ref, ta_ref, tb_ref, td_ref, ones_ref, gaq_ref, gak_ref, gcq_ref, gckv_ref, gbq_ref, gbk_ref,
                 gdq_ref, gdk_ref, gik_ref, wq_ref, wkv_ref,
                 qa_ref, ka_ref, va_ref, qb_ref, kb_ref, vb_ref, qc_ref, kc_ref, vc_ref,
                 qd_ref, kd_ref, vd_ref, iq_ref, ik_ref, iw_ref):
    def sec(name, width):
        return y_ref[:, _OFF[name]:_OFF[name] + width]

    tm = y_ref.shape[0]
    lane = lax.broadcasted_iota(jnp.int32, (tm, LANES), 1)
    first_half = lane < DIFF_SUB_DIM

    qa = _rope(_rms_sub64(sec("a_q", 512), ones_ref) * gaq_ref[...], ta_ref, 8) * (DIFF_SUB_DIM ** -0.5 * LOG2E)
    for h in range(GROUP_HEADS):
        qh = qa[:, h * LANES:(h + 1) * LANES]
        qa_ref[:, (2 * h) * LANES:(2 * h + 1) * LANES] = jnp.where(first_half, qh, 0.0).astype(BF16)
        qa_ref[:, (2 * h + 1) * LANES:(2 * h + 2) * LANES] = jnp.where(first_half, 0.0, qh).astype(BF16)
    ka_ref[...] = _rope(_rms_sub64(sec("a_k", 512), ones_ref) * gak_ref[...], ta_ref, 8).astype(BF16)
    _store_vext(va_ref, sec("a_v", 512))

    cq = sec("b_cq", 512)
    cq = cq * lax.rsqrt(jnp.sum(cq * cq, axis=-1, keepdims=True) * (1.0 / MLA_Q_RANK) + EPS) * gcq_ref[...]
    qb = _dot(cq.astype(BF16), wq_ref[...])
    qb = _rms_lane_groups(qb, MLA_PAD_DIM, MLA_QK_DIM) * gbq_ref[...]
    ckv = _rms_lane_groups(sec("b_ckv", 128), 128, MLA_KV_RANK) * gckv_ref[...]
    kv = _dot(ckv.astype(BF16), wkv_ref[...])
    kr = sec("b_kr", 128)
    kr_ss = jnp.sum(kr * kr, axis=-1, keepdims=True)
    for h in range(GROUP_HEADS):
        lo, hi = h * MLA_PAD_DIM, (h + 1) * MLA_PAD_DIM
        qb_ref[:, lo:hi] = (_rope(qb[:, lo:hi], tb_ref, 32) * (MLA_QK_DIM ** -0.5 * LOG2E)).astype(BF16)
        kn = kv[:, h * LANES:(h + 1) * LANES]
        ms = (jnp.sum(kn * kn, axis=-1, keepdims=True) + kr_ss) * (1.0 / MLA_QK_DIM)
        kh = jnp.concatenate([kn, kr], axis=1) * lax.rsqrt(ms + EPS) * gbk_ref[...]
        kb_ref[:, lo:hi] = _rope(kh, tb_ref, 32).astype(BF16)
    _store_vext(vb_ref, kv[:, GROUP_WIDTH:])

    qc_ref[...] = (sec("c_q", 512) * (HEAD_DIM ** -0.5 * LOG2E)).astype(BF16)
    kc_ref[...] = sec("c_k", 512).astype(BF16)
    vc_ref[...] = sec("c_v", 512).astype(BF16)

    qd = _rope(_rms_lane_groups(sec("d_q", 512), HEAD_DIM, HEAD_DIM) * gdq_ref[...], td_ref, 16)
    qd_ref[...] = (qd * (HEAD_DIM ** -0.5 * LOG2E)).astype(BF16)
    kd_ref[...] = _rope(_rms_lane_groups(sec("d_k", 512), HEAD_DIM, HEAD_DIM) * gdk_ref[...], td_ref, 16).astype(BF16)
    _store_vext(vd_ref, sec("d_v", 512))
    for half in range(2):
        iq = y_ref[:, _OFF["d_iq"] + half * 512:_OFF["d_iq"] + (half + 1) * 512]
        iq_ref[:, half * 512:(half + 1) * 512] = (_rope(iq, ta_ref, 8) * (IDX_DIM ** -0.5)).astype(BF16)
    ik = _rms_lane_groups(sec("d_ik", 128), 128, IDX_DIM) * gik_ref[...]
    ik = (ik * ta_ref[0, :, :LANES] + pltpu.roll(ik, LANES - 8, 1) * ta_ref[1, :, :LANES]
          + pltpu.roll(ik, 8, 1) * ta_ref[2, :, :LANES])
    ik_ref[:, :LANES] = ik.astype(BF16)
    ik_ref[:, LANES:] = pltpu.roll(ik, IDX_DIM, 1).astype(BF16)
    iw_ref[...] = sec("d_iw", 128) * (IDX_HEADS ** -0.5)


def _prep(y, seq, tabs, ones64, gains, wq, wkv, *, tm):
    n = y.shape[0]
    nblk_seq = seq // tm
    ta, tb, td = tabs

    def rows(width):
        return pl.BlockSpec((tm, width), lambda i: (i, 0))

    def table(width):
        return pl.BlockSpec((3, tm, width), lambda i: (0, i % nblk_seq, 0))

    def whole(a):
        return pl.BlockSpec(a.shape, lambda i: (0,) * a.ndim)

    widths = [1024, 512, 1024, 1024, 1024, 1024, 512, 512, 512, 512, 512, 1024, 1024, 256]
    out_shape = [jax.ShapeDtypeStruct((n, w), BF16) for w in widths] + [jax.ShapeDtypeStruct((n, LANES), F32)]
    out_specs = [rows(w) for w in widths] + [rows(LANES)]
    return pl.pallas_call(
        _prep_kernel,
        grid=(n // tm,),
        in_specs=[rows(Y_WIDTH), table(512), table(256), table(512), whole(ones64)]
                 + [whole(g) for g in gains] + [whole(wq), whole(wkv)],
        out_specs=out_specs,
        out_shape=out_shape,
        compiler_params=_cparams(1),
        name="prep",
    )(y, ta, tb, td, ones64, *gains, wq, wkv)


def _softmax_chunk(s, vext, m_ref, acc_ref, idx):
    m_prev = m_ref[idx]
    m_new = jnp.maximum(m_prev, jnp.max(s, axis=1, keepdims=True))
    p = jnp.exp2(s - jnp.tile(m_new, (1, s.shape[1] // LANES)))
    alpha = jnp.exp2(m_prev - m_new)
    acc_ref[idx] = jnp.tile(alpha, (1, VEXT // LANES)) * acc_ref[idx] + _dot(p.astype(BF16), vext)
    m_ref[idx] = m_new


def _init_softmax(m_ref, acc_ref):
    m_ref[...] = jnp.full(m_ref.shape, NEG, F32)
    acc_ref[...] = jnp.zeros(acc_ref.shape, F32)


def _normalised(acc_ref, idx):
    acc = acc_ref[idx]
    return acc[:, :LANES] / acc[:, LANES:]


def _causal_mask(row0, col0, tq, tkc, strict=False):
    rows = row0 + lax.broadcasted_iota(jnp.int32, (tq, tkc), 0)
    cols = col0 + lax.broadcasted_iota(jnp.int32, (tq, tkc), 1)
    return cols < rows if strict else cols <= rows


def _attn_diff_kernel(lam_ref, sub_ref, q_ref, k_ref, v_ref, o_ref, m_ref, acc_ref, *, tq, lam_init):
    qi = pl.program_id(1)
    _init_softmax(m_ref, acc_ref)

    def chunk(off, mask):
        for mp in range(2 * GROUP_HEADS):
            h = mp // 2
            q = q_ref[0, :, mp * LANES:(mp + 1) * LANES]
            k = k_ref[0, pl.ds(off, tq), h * LANES:(h + 1) * LANES]
            v = v_ref[0, pl.ds(off, tq), h * VEXT:(h + 1) * VEXT]
            s = _dot_nt(q, k)
            if mask is not None:
                s = jnp.where(mask, s, NEG)
            _softmax_chunk(s, v, m_ref, acc_ref, mp)

    def full_chunk(c, carry):
        chunk(pl.multiple_of(c * tq, tq), None)
        return carry

    lax.fori_loop(0, qi, full_chunk, 0)
    diag = pl.multiple_of(qi * tq, tq)
    chunk(diag, _causal_mask(0, 0, tq, tq))

    lv = lam_ref[...]
    lam = (jnp.exp(jnp.sum(lv[0:1] * lv[1:2], axis=-1, keepdims=True))
           - jnp.exp(jnp.sum(lv[2:3] * lv[3:4], axis=-1, keepdims=True)) + lam_init)
    for h in range(GROUP_HEADS):
        o = _normalised(acc_ref, 2 * h) - lam * _normalised(acc_ref, 2 * h + 1)
        o = _rms_rows(o, sub_ref[...]) * (1.0 - lam_init)
        o_ref[0, :, h * LANES:(h + 1) * LANES] = o.astype(o_ref.dtype)


def _attn_mla_kernel(q_ref, k_ref, v_ref, o_ref, m_ref, acc_ref, *, tq):
    qi = pl.program_id(1)
    _init_softmax(m_ref, acc_ref)

    def chunk(off, mask):
        for h in range(GROUP_HEADS):
            q = q_ref[0, :, h * MLA_PAD_DIM:(h + 1) * MLA_PAD_DIM]
            k = k_ref[0, pl.ds(off, tq), h * MLA_PAD_DIM:(h + 1) * MLA_PAD_DIM]
            v = v_ref[0, pl.ds(off, tq), h * VEXT:(h + 1) * VEXT]
            s = _dot_nt(q, k)
            if mask is not None:
                s = jnp.where(mask, s, NEG)
            _softmax_chunk(s, v, m_ref, acc_ref, h)

    def full_chunk(c, carry):
        chunk(pl.multiple_of(c * tq, tq), None)
        return carry

    lax.fori_loop(0, qi, full_chunk, 0)
    chunk(pl.multiple_of(qi * tq, tq), _causal_mask(0, 0, tq, tq))
    for h in range(GROUP_HEADS):
        o_ref[0, :, h * LANES:(h + 1) * LANES] = _normalised(acc_ref, h).astype(o_ref.dtype)


def _attn_stick_kernel(tri_ref, q_ref, k_ref, v_ref, o_ref, carry_ref, acc_ref, *, tq, tkc):
    qi = pl.program_id(1)
    carry_ref[...] = jnp.zeros(carry_ref.shape, F32)
    acc_ref[...] = jnp.zeros(acc_ref.shape, F32)

    def chunk(off, mask):
        tri = tri_ref[...]
        for h in range(GROUP_HEADS):
            q = q_ref[0, :, h * LANES:(h + 1) * LANES]
            k = k_ref[0, pl.ds(off, tkc), h * LANES:(h + 1) * LANES]
            v = v_ref[0, pl.ds(off, tkc), h * LANES:(h + 1) * LANES]
            z = _dot_nt(q, k)
            log_b = jnp.minimum(z, 0.0) - jnp.log2(1.0 + jnp.exp2(-jnp.abs(z)))
            log_1mb = log_b - z
            if mask is not None:
                log_1mb = jnp.where(mask, log_1mb, 0.0)
            hi = log_1mb.astype(BF16)
            lo = (log_1mb - hi.astype(F32)).astype(BF16)
            inner = _dot(hi, tri) + _dot(lo, tri)
            carry = carry_ref[h]
            a = jnp.exp2(log_b + inner + jnp.tile(carry, (1, tkc // LANES)))
            if mask is not None:
                a = jnp.where(mask, a, 0.0)
            acc_ref[h] += _dot(a.astype(BF16), v)
            carry_ref[h] = carry + (inner[:, 0:1] + log_1mb[:, 0:1])

    n_diag = tq // tkc
    for j in range(n_diag):
        col0 = (n_diag - 1 - j) * tkc
        chunk(pl.multiple_of(qi * tq + col0, tkc), _causal_mask(0, col0, tq, tkc, strict=True))

    def full_chunk(c, carry):
        chunk(pl.multiple_of((qi * n_diag - 1 - c) * tkc, tkc), None)
        return carry

    lax.fori_loop(0, qi * n_diag, full_chunk, 0)
    for h in range(GROUP_HEADS):
        o_ref[0, :, h * LANES:(h + 1) * LANES] = acc_ref[h].astype(o_ref.dtype)


def _resident_attention(kernel, q, k, v, extra, *, tq, scratch, name):
    b, s, _ = q.shape
    q_map = lambda bi, qi: (bi, qi, 0)
    all_map = lambda bi, qi: (bi, 0, 0)
    extra_specs = [pl.BlockSpec(e.shape, lambda bi, qi, nd=e.ndim: (0,) * nd) for e in extra]
    return pl.pallas_call(
        kernel,
        grid=(b, s // tq),
        in_specs=extra_specs + [pl.BlockSpec((1, tq, q.shape[2]), q_map),
                                pl.BlockSpec((1, s, k.shape[2]), all_map),
                                pl.BlockSpec((1, s, v.shape[2]), all_map)],
        out_specs=pl.BlockSpec((1, tq, GROUP_WIDTH), q_map),
        out_shape=jax.ShapeDtypeStruct((b, s, GROUP_WIDTH), BF16),
        scratch_shapes=scratch,
        compiler_params=_cparams(2),
        name=name,
    )(*extra, q, k, v)


def _attn_dsa_kernel(q_ref, k_ref, v_ref, iq_ref, ik_ref, iw_ref, before_ref, o_ref, key_ref, half_ref, m_ref, acc_ref,
                     *, tq, topk):
    qi = pl.program_id(1)
    nkc = qi + 1
    n_groups = tq // LANES

    def score_chunk(c, mask):
        off = pl.multiple_of(c * tq, tq)
        ikc = ik_ref[0, pl.ds(off, tq), :]
        iw = iw_ref[0]
        isc = jnp.zeros((tq, tq), F32)
        for hp in range(IDX_HEADS // 2):
            iqp = iq_ref[0, :, hp * LANES:(hp + 1) * LANES]
            for e in range(2):
                hh = 2 * hp + e
                s = _dot_nt(iqp, ikc[:, e * LANES:(e + 1) * LANES])
                isc = isc + jnp.maximum(s, 0.0) * iw[:, hh:hh + 1]
        bits = lax.bitcast_convert_type(isc, jnp.int32)
        key = jnp.where(bits < 0, bits ^ jnp.int32(0x7FFFFFFF), bits)
        if mask is not None:
            key = jnp.where(mask, key, INT_MIN)
        key_ref[c] = key
        half_ref[c] = (key >> 16).astype(jnp.int16)

    def full_score_chunk(c, carry):
        score_chunk(c, None)
        return carry

    lax.fori_loop(0, qi, full_score_chunk, 0)
    score_chunk(qi, _causal_mask(0, 0, tq, tq))

    def row_count(flags_of_chunk):
        def body(c, part):
            flags = flags_of_chunk(c)
            for g in range(n_groups):
                part = part + flags[:, g * LANES:(g + 1) * LANES]
            return part

        part = lax.fori_loop(0, nkc, body, jnp.zeros((tq, LANES), jnp.int16))
        return jnp.sum(part.astype(F32), axis=-1, keepdims=True)

    one16, zero16 = jnp.int16(1), jnp.int16(0)

    def search16(rank, cnt_init):
        def bit(i, carry):
            thr_u, cnt_thr = carry
            cand_u = thr_u | lax.shift_left(jnp.int32(1), 15 - i)
            cand = (cand_u - 32768).astype(jnp.int16)
            cnt = row_count(lambda c: jnp.where(half_ref[c] >= cand, one16, zero16))
            take = cnt >= rank
            return jnp.where(take, cand_u, thr_u), jnp.where(take, cnt, cnt_thr)

        return lax.fori_loop(0, 16, bit, (jnp.zeros((tq, 1), jnp.int32), cnt_init))

    k_f = jnp.full((tq, 1), float(topk), F32)
    hi_u, cnt_hi = search16(k_f, jnp.zeros((tq, 1), F32))
    hi_thr = (hi_u - 32768).astype(jnp.int16)
    cnt_gt = row_count(lambda c: jnp.where(half_ref[c] > hi_thr, one16, zero16))

    def low_halves(c, carry):
        lo = ((key_ref[c] & 0xFFFF) - 32768).astype(jnp.int16)
        half_ref[c] = jnp.where(half_ref[c] == hi_thr, lo, jnp.int16(-32768))
        return carry

    lax.fori_loop(0, nkc, low_halves, 0)
    lo_u, cnt_lo = search16(k_f - cnt_gt, cnt_hi - cnt_gt)
    thr_raw = ((hi_u << 16) | lo_u) ^ jnp.int32(INT_MIN)
    short_row = thr_raw == jnp.int32(INT_MIN)
    thr = jnp.maximum(thr_raw, jnp.int32(INT_MIN + 1))
    cnt_ge = cnt_gt + cnt_lo
    tied = jnp.logical_and(jnp.logical_not(short_row), cnt_ge > k_f)
    any_tied = jnp.max(jnp.where(tied, 1.0, 0.0)) > 0.0

    _init_softmax(m_ref, acc_ref)

    def attend(c, sel):
        off = pl.multiple_of(c * tq, tq)
        for h in range(GROUP_HEADS):
            q = q_ref[0, :, h * LANES:(h + 1) * LANES]
            k = k_ref[0, pl.ds(off, tq), h * LANES:(h + 1) * LANES]
            v = v_ref[0, pl.ds(off, tq), h * VEXT:(h + 1) * VEXT]
            s = jnp.where(sel, _dot_nt(q, k), NEG)
            _softmax_chunk(s, v, m_ref, acc_ref, h)

    @pl.when(jnp.logical_not(any_tied))
    def _():
        def attend_chunk(c, carry):
            attend(c, key_ref[c] >= thr)
            return carry

        lax.fori_loop(0, nkc, attend_chunk, 0)

    @pl.when(any_tied)
    def _():
        def eq_chunk(c, part):
            eq = jnp.where(key_ref[c] == thr, 1.0, 0.0)
            for g in range(n_groups):
                part = part + eq[:, g * LANES:(g + 1) * LANES]
            return part

        n_eq = jnp.sum(lax.fori_loop(0, nkc, eq_chunk, jnp.zeros((tq, LANES), F32)), axis=-1, keepdims=True)
        keep = jnp.where(short_row, 0.0, k_f - (cnt_ge - n_eq))

        def attend_chunk(c, seen):
            key = key_ref[c]
            eq = key == thr
            eq_f = jnp.where(eq, 1.0, 0.0)
            rank = seen + _dot(eq_f.astype(BF16), before_ref[...])
            sel = jnp.logical_or(key > thr, jnp.logical_and(eq, rank < keep))
            attend(c, sel)
            return seen + jnp.sum(eq_f, axis=-1, keepdims=True)

        lax.fori_loop(0, nkc, attend_chunk, jnp.zeros((tq, 1), F32))

    for h in range(GROUP_HEADS):
        o_ref[0, :, h * LANES:(h + 1) * LANES] = _normalised(acc_ref, h).astype(o_ref.dtype)


def _attn_dsa(q, k, v, iq, ik, iw, *, tq):
    b, s, _ = q.shape
    topk = min(TOPK_MAX, s // 4)
    before = (jnp.arange(tq)[:, None] < jnp.arange(tq)[None, :]).astype(BF16)
    q_map = lambda bi, qi: (bi, qi, 0)
    all_map = lambda bi, qi: (bi, 0, 0)
    return pl.pallas_call(
        functools.partial(_attn_dsa_kernel, tq=tq, topk=topk),
        grid=(b, s // tq),
        in_specs=[pl.BlockSpec((1, tq, GROUP_WIDTH), q_map),
                  pl.BlockSpec((1, s, GROUP_WIDTH), all_map),
                  pl.BlockSpec((1, s, GROUP_HEADS * VEXT), all_map),
                  pl.BlockSpec((1, tq, IDX_HEADS * IDX_DIM), q_map),
                  pl.BlockSpec((1, s, 2 * LANES), all_map),
                  pl.BlockSpec((1, tq, LANES), q_map),
                  pl.BlockSpec((tq, tq), lambda bi, qi: (0, 0))],
        out_specs=pl.BlockSpec((1, tq, GROUP_WIDTH), q_map),
        out_shape=jax.ShapeDtypeStruct((b, s, GROUP_WIDTH), BF16),
        scratch_shapes=[pltpu.VMEM((s // tq, tq, tq), jnp.int32),
                        pltpu.VMEM((s // tq, tq, tq), jnp.int16),
                        pltpu.VMEM((GROUP_HEADS, tq, LANES), F32),
                        pltpu.VMEM((GROUP_HEADS, tq, VEXT), F32)],
        compiler_params=_cparams(2),
        name="attn_dsa",
    )(q, k, v, iq, ik, iw, before)


def _rope_tables(seq, ncols, group, start, rot):
    half = rot // 2
    inv = ROPE_THETA ** (-jnp.arange(half, dtype=F32) * 2.0 / rot)
    j = jnp.arange(ncols) % group - start
    in_lo = (j >= 0) & (j < half)
    in_hi = (j >= half) & (j < rot)
    idx = jnp.clip(jnp.where(in_hi, j - half, j), 0, half - 1)
    ang = jnp.arange(seq, dtype=jnp.int32).astype(F32)[:, None] * inv[idx][None, :]
    cos, sin = jnp.cos(ang), jnp.sin(ang)
    return jnp.stack([jnp.where(in_lo | in_hi, cos, 1.0), jnp.where(in_lo, -sin, 0.0), jnp.where(in_hi, sin, 0.0)])


def _pad_cols(a, width):
    return jnp.pad(a, ((0, 0), (0, width - a.shape[1])))


def _relayout_w_in(w):
    sizes = (512,) * 3 + (MLA_Q_RANK, MLA_KV_RANK, MLA_ROPE_DIM) + (512,) * 6 + (IDX_HEADS * IDX_DIM, IDX_DIM, IDX_HEADS)
    names = ("a_q", "a_k", "a_v", "b_cq", "b_ckv", "b_kr", "c_q", "c_k", "c_v", "d_q", "d_k", "d_v", "d_iq", "d_ik", "d_iw")
    parts, start = {}, 0
    for nme, sz in zip(names, sizes):
        parts[nme] = w[:, start:start + sz]
        start += sz
    order = sorted(_OFF, key=_OFF.get)
    ends = [_OFF[nme] for nme in order[1:]] + [Y_WIDTH]
    return jnp.concatenate([_pad_cols(parts[nme], end - _OFF[nme]) for nme, end in zip(order, ends)], axis=1).astype(BF16)


def _relayout_mla_q(w):
    w = w.reshape(w.shape[0], GROUP_HEADS, MLA_QK_DIM)
    w = jnp.pad(w, ((0, 0), (0, 0), (0, MLA_PAD_DIM - MLA_QK_DIM)))
    return w.reshape(w.shape[0], GROUP_HEADS * MLA_PAD_DIM)


def _tile_row(g, reps):
    return jnp.tile(g.reshape(1, -1), (1, reps))


def kernel(x, attn_norm, w_in, diff_qk_norm, diff_lambda, diff_subln, mla_q_a_norm, mla_wq_b, mla_kv_a_norm,
           mla_wkv_b, mla_qk_norm, dsa_qk_norm, idx_k_norm, w_o, ffn_norm, w_gate, w_up, w_down):
    b, s, d = x.shape
    n = b * s
    depth = w_in.shape[0]
    tm = min(1024, n)
    tq = min(512, s)
    tabs = (_rope_tables(s, 512, DIFF_SUB_DIM, 0, DIFF_SUB_DIM // 4),
            _rope_tables(s, MLA_PAD_DIM, MLA_PAD_DIM, MLA_NOPE_DIM, MLA_ROPE_DIM),
            _rope_tables(s, 512, HEAD_DIM, 0, HEAD_DIM // 4))
    ones64 = jnp.kron(jnp.eye(512 // DIFF_SUB_DIM, dtype=F32), jnp.ones((DIFF_SUB_DIM, DIFF_SUB_DIM), F32)).astype(BF16)
    tk_stick = min(256, s)
    tri = (jnp.arange(tk_stick)[:, None] > jnp.arange(tk_stick)[None, :]).astype(BF16)

    xf = x.reshape(n, d)
    for l in range(depth):
        lam_init = 0.8 - 0.6 * math.exp(-0.3 * l)
        y = _rms_matmul(xf, attn_norm[l].reshape(1, d), _relayout_w_in(w_in[l]), tm=tm, tn=512, out_dtype=F32)

        wq = jnp.pad(_relayout_mla_q(mla_wq_b[l]), ((0, 512 - MLA_Q_RANK), (0, 0))).astype(BF16)
        wkv = mla_wkv_b[l].reshape(MLA_KV_RANK, GROUP_HEADS, 2, HEAD_DIM).transpose(0, 2, 1, 3)
        wkv = wkv.reshape(MLA_KV_RANK, 2 * GROUP_WIDTH).astype(BF16)
        gains = [
            _tile_row(diff_qk_norm[l, 0], 8), _tile_row(diff_qk_norm[l, 1], 8),
            _pad_cols(mla_q_a_norm[l].reshape(1, -1), 512), mla_kv_a_norm[l].reshape(1, -1),
            _relayout_mla_q(mla_qk_norm[l, 0].reshape(1, MLA_QK_DIM).repeat(GROUP_HEADS, 0).reshape(1, -1)),
            _pad_cols(mla_qk_norm[l, 1].reshape(1, -1), MLA_PAD_DIM),
            _tile_row(dsa_qk_norm[l, 0], 4), _tile_row(dsa_qk_norm[l, 1], 4),
            _pad_cols(idx_k_norm[l].reshape(1, -1), LANES),
        ]
        (qa, ka, va, qb, kb, vb, qc, kc, vc, qd, kd, vd, iq, ik, iw) = [
            a.reshape(b, s, a.shape[1]) for a in _prep(y, s, tabs, ones64, gains, wq, wkv, tm=min(256, s))]

        o_a = _resident_attention(
            functools.partial(_attn_diff_kernel, tq=tq, lam_init=lam_init), qa, ka, va,
            [diff_lambda[l], diff_subln[l].reshape(1, HEAD_DIM)], tq=tq,
            scratch=[pltpu.VMEM((8, tq, LANES), F32), pltpu.VMEM((8, tq, VEXT), F32)], name="attn_diff")
        o_b = _resident_attention(
            functools.partial(_attn_mla_kernel, tq=tq), qb, kb, vb, [], tq=tq,
            scratch=[pltpu.VMEM((4, tq, LANES), F32), pltpu.VMEM((4, tq, VEXT), F32)], name="attn_mla")
        o_c = _resident_attention(
            functools.partial(_attn_stick_kernel, tq=tq, tkc=tk_stick), qc, kc, vc, [tri], tq=tq,
            scratch=[pltpu.VMEM((4, tq, LANES), F32), pltpu.VMEM((4, tq, LANES), F32)], name="attn_stick")
        o_d = _attn_dsa(qd, kd, vd, iq, ik, iw, tq=tq)

        mixed = jnp.concatenate([o_a, o_b, o_c, o_d], axis=-1).reshape(n, 4 * GROUP_WIDTH)
        xf = _matmul_residual(mixed, w_o[l].astype(BF16), xf, tm=tm, tn=512)
        act = _ffn_up(xf, ffn_norm[l].reshape(1, d), w_gate[l].astype(BF16), w_up[l].astype(BF16), tm=tm, tn=512)
        xf = _matmul_residual(act, w_down[l].astype(BF16), xf, tm=min(512, n), tn=512)
    return xf.reshape(b, s, d)
```

```python
import functools
import math

import jax
import jax.numpy as jnp
from jax import lax
from jax.experimental import pallas as pl
from jax.experimental.pallas import tpu as pltpu

F32 = jnp.float32
BF16 = jnp.bfloat16

HEAD_DIM = 128
GROUP_HEADS = 4
GROUP_WIDTH = GROUP_HEADS * HEAD_DIM
ROPE_THETA = 500000.0
EPS = 1e-6
DIFF_SUB_DIM = 64
MLA_Q_RANK = 448
MLA_KV_RANK = 128
MLA_NOPE_DIM = 128
MLA_ROPE_DIM = 64
MLA_QK_DIM = MLA_NOPE_DIM + MLA_ROPE_DIM
MLA_PAD_DIM = 256
IDX_HEADS = 16
IDX_DIM = 64
TOPK_MAX = 256

LANES = 128
VEXT = 2 * LANES
Y_WIDTH = 13 * GROUP_WIDTH
NEG = -1e30
INT_MIN = -2147483648
LOG2E = 1.4426950408889634
STICK_DEAD_LOG2 = -160.0
VMEM_LIMIT = 56 * 1024 * 1024

_OFF = dict(a_q=0, a_k=512, a_v=1024, b_cq=1536, b_ckv=2048, b_kr=2176, d_ik=2304, d_iw=2432,
            c_q=2560, c_k=3072, c_v=3584, d_q=4096, d_k=4608, d_v=5120, d_iq=5632)


def _cparams(n_axes):
    return pltpu.CompilerParams(dimension_semantics=("arbitrary",) * n_axes, vmem_limit_bytes=VMEM_LIMIT)


def _dot(a, b):
    return jnp.dot(a, b, preferred_element_type=F32)


def _dot_nt(a, b):
    return lax.dot_general(a, b, (((1,), (1,)), ((), ())), preferred_element_type=F32)


def _rms_rows(x, g):
    ms = jnp.mean(x * x, axis=-1, keepdims=True)
    return x * lax.rsqrt(ms + EPS) * g


def _rms_matmul_kernel(x_ref, g_ref, w_ref, o_ref, h_ref):
    @pl.when(pl.program_id(1) == 0)
    def _():
        h_ref[...] = _rms_rows(x_ref[...], g_ref[...]).astype(BF16)

    o_ref[...] = _dot(h_ref[...], w_ref[...]).astype(o_ref.dtype)


def _rms_matmul(x, g, w, *, tm, tn, out_dtype):
    n, d = x.shape
    m = w.shape[1]
    return pl.pallas_call(
        _rms_matmul_kernel,
        grid=(n // tm, m // tn),
        in_specs=[pl.BlockSpec((tm, d), lambda i, j: (i, 0)),
                  pl.BlockSpec((1, d), lambda i, j: (0, 0)),
                  pl.BlockSpec((d, tn), lambda i, j: (0, j))],
        out_specs=pl.BlockSpec((tm, tn), lambda i, j: (i, j)),
        out_shape=jax.ShapeDtypeStruct((n, m), out_dtype),
        scratch_shapes=[pltpu.VMEM((tm, d), BF16)],
        compiler_params=_cparams(2),
        name="rms_matmul",
    )(x, g, w)


def _ffn_up_kernel(x_ref, g_ref, wg_ref, wu_ref, o_ref, h_ref):
    @pl.when(pl.program_id(1) == 0)
    def _():
        h_ref[...] = _rms_rows(x_ref[...], g_ref[...]).astype(BF16)

    h = h_ref[...]
    a = _dot(h, wg_ref[...])
    b = _dot(h, wu_ref[...])
    o_ref[...] = (jax.nn.silu(a) * b).astype(o_ref.dtype)


def _ffn_up(x, g, wg, wu, *, tm, tn):
    n, d = x.shape
    m = wg.shape[1]
    return pl.pallas_call(
        _ffn_up_kernel,
        grid=(n // tm, m // tn),
        in_specs=[pl.BlockSpec((tm, d), lambda i, j: (i, 0)),
                  pl.BlockSpec((1, d), lambda i, j: (0, 0)),
                  pl.BlockSpec((d, tn), lambda i, j: (0, j)),
                  pl.BlockSpec((d, tn), lambda i, j: (0, j))],
        out_specs=pl.BlockSpec((tm, tn), lambda i, j: (i, j)),
        out_shape=jax.ShapeDtypeStruct((n, m), BF16),
        scratch_shapes=[pltpu.VMEM((tm, d), BF16)],
        compiler_params=_cparams(2),
        name="ffn_up",
    )(x, g, wg, wu)


def _matmul_residual_kernel(*refs):
    a_refs, (w_ref, r_ref, o_ref) = refs[:-3], refs[-3:]
    acc = r_ref[...]
    row = 0
    for a_ref in a_refs:
        k = a_ref.shape[1]
        acc = acc + _dot(a_ref[...], w_ref[row:row + k, :])
        row += k
    o_ref[...] = acc


def _matmul_residual(parts, w, r, *, tm, tn):
    n = parts[0].shape[0]
    k, m = w.shape
    return pl.pallas_call(
        _matmul_residual_kernel,
        grid=(n // tm, m // tn),
        in_specs=[pl.BlockSpec((tm, a.shape[1]), lambda i, j: (i, 0)) for a in parts]
                 + [pl.BlockSpec((k, tn), lambda i, j: (0, j)),
                    pl.BlockSpec((tm, tn), lambda i, j: (i, j))],
        out_specs=pl.BlockSpec((tm, tn), lambda i, j: (i, j)),
        out_shape=jax.ShapeDtypeStruct((n, m), F32),
        compiler_params=_cparams(2),
        name="matmul_residual",
    )(*parts, w, r)


def _rope(x, t_ref, half):
    n = x.shape[1]
    return (x * t_ref[0] + pltpu.roll(x, n - half, 1) * t_ref[1] + pltpu.roll(x, half, 1) * t_ref[2])


def _rms_lane_groups(x, width, count):
    outs = []
    for h in range(x.shape[1] // width):
        xh = x[:, h * width:(h + 1) * width]
        ms = jnp.sum(xh * xh, axis=-1, keepdims=True) * (1.0 / count)
        outs.append(xh * lax.rsqrt(ms + EPS))
    return outs[0] if len(outs) == 1 else jnp.concatenate(outs, axis=1)


def _rms_sub64(x, ones_ref):
    x2 = x * x
    hi = x2.astype(BF16)
    lo = (x2 - hi.astype(F32)).astype(BF16)
    ms = (_dot(hi, ones_ref[...]) + _dot(lo, ones_ref[...])) * (1.0 / DIFF_SUB_DIM)
    return x * lax.rsqrt(ms + EPS)


def _store_vext(ref, v):
    ones = jnp.ones((v.shape[0], LANES), BF16)
    for h in range(GROUP_HEADS):
        ref[:, h * VEXT:h * VEXT + LANES] = v[:, h * LANES:(h + 1) * LANES].astype(BF16)
        ref[:, h * VEXT + LANES:(h + 1) * VEXT] = ones


def _prep_kernel(y_ref, ta_ref, tb_ref, td_ref, ones_ref, gaq_ref, gak_ref, gcq_ref, gckv_ref, gbq_ref, gbk_ref,
                 gdq_ref, gdk_ref, gik_ref, wq_ref, wkv_ref,
                 qa_ref, ka_ref, va_ref, qb_ref, kb_ref, vb_ref, qc_ref, kc_ref, vc_ref,
                 qd_ref, kd_ref, vd_ref, iq_ref, ik_ref, iw_ref):
    def sec(name, width):
        return y_ref[:, _OFF[name]:_OFF[name] + width]

    tm = y_ref.shape[0]
    lane = lax.broadcasted_iota(jnp.int32, (tm, LANES), 1)
    first_half = lane < DIFF_SUB_DIM

    qa = _rope(_rms_sub64(sec("a_q", 512), ones_ref) * gaq_ref[...], ta_ref, 8) * (DIFF_SUB_DIM ** -0.5 * LOG2E)
    for h in range(GROUP_HEADS):
        qh = qa[:, h * LANES:(h + 1) * LANES]
        qa_ref[:, (2 * h) * LANES:(2 * h + 1) * LANES] = jnp.where(first_half, qh, 0.0).astype(BF16)
        qa_ref[:, (2 * h + 1) * LANES:(2 * h + 2) * LANES] = jnp.where(first_half, 0.0, qh).astype(BF16)
    ka_ref[...] = _rope(_rms_sub64(sec("a_k", 512), ones_ref) * gak_ref[...], ta_ref, 8).astype(BF16)
    _store_vext(va_ref, sec("a_v", 512))

    cq = sec("b_cq", 512)
    cq = cq * lax.rsqrt(jnp.sum(cq * cq, axis=-1, keepdims=True) * (1.0 / MLA_Q_RANK) + EPS) * gcq_ref[...]
    qb = _dot(cq.astype(BF16), wq_ref[...])
    qb = _rms_lane_groups(qb, MLA_PAD_DIM, MLA_QK_DIM) * gbq_ref[...]
    ckv = _rms_lane_groups(sec("b_ckv", 128), 128, MLA_KV_RANK) * gckv_ref[...]
    kv = _dot(ckv.astype(BF16), wkv_ref[...])
    kr = sec("b_kr", 128)
    kr_ss = jnp.sum(kr * kr, axis=-1, keepdims=True)
    for h in range(GROUP_HEADS):
        lo, hi = h * MLA_PAD_DIM, (h + 1) * MLA_PAD_DIM
        qb_ref[:, lo:hi] = (_rope(qb[:, lo:hi], tb_ref, 32) * (MLA_QK_DIM ** -0.5 * LOG2E)).astype(BF16)
        kn = kv[:, h * LANES:(h + 1) * LANES]
        ms = (jnp.sum(kn * kn, axis=-1, keepdims=True) + kr_ss) * (1.0 / MLA_QK_DIM)
        kh = jnp.concatenate([kn, kr], axis=1) * lax.rsqrt(ms + EPS) * gbk_ref[...]
        kb_ref[:, lo:hi] = _rope(kh, tb_ref, 32).astype(BF16)
    _store_vext(vb_ref, kv[:, GROUP_WIDTH:])

    qc_ref[...] = (sec("c_q", 512) * (HEAD_DIM ** -0.5 * LOG2E)).astype(BF16)
    kc_ref[...] = sec("c_k", 512).astype(BF16)
    vc_ref[...] = sec("c_v", 512).astype(BF16)

    qd = _rope(_rms_lane_groups(sec("d_q", 512), HEAD_DIM, HEAD_DIM) * gdq_ref[...], td_ref, 16)
    qd_ref[...] = (qd * (HEAD_DIM ** -0.5 * LOG2E)).astype(BF16)
    kd_ref[...] = _rope(_rms_lane_groups(sec("d_k", 512), HEAD_DIM, HEAD_DIM) * gdk_ref[...], td_ref, 16).astype(BF16)
    _store_vext(vd_ref, sec("d_v", 512))
    for half in range(2):
        iq = y_ref[:, _OFF["d_iq"] + half * 512:_OFF["d_iq"] + (half + 1) * 512]
        iq_ref[:, half * 512:(half + 1) * 512] = (_rope(iq, ta_ref, 8) * (IDX_DIM ** -0.5)).astype(BF16)
    ik = _rms_lane_groups(sec("d_ik", 128), 128, IDX_DIM) * gik_ref[...]
    ik = (ik * ta_ref[0, :, :LANES] + pltpu.roll(ik, LANES - 8, 1) * ta_ref[1, :, :LANES]
          + pltpu.roll(ik, 8, 1) * ta_ref[2, :, :LANES])
    ik_ref[:, :LANES] = ik.astype(BF16)
    ik_ref[:, LANES:] = pltpu.roll(ik, IDX_DIM, 1).astype(BF16)
    iw_ref[...] = sec("d_iw", 128) * (IDX_HEADS ** -0.5)


def _prep(y, seq, tabs, ones64, gains, wq, wkv, *, tm):
    n = y.shape[0]
    nblk_seq = seq // tm
    ta, tb, td = tabs

    def rows(width):
        return pl.BlockSpec((tm, width), lambda i: (i, 0))

    def table(width):
        return pl.BlockSpec((3, tm, width), lambda i: (0, i % nblk_seq, 0))

    def whole(a):
        return pl.BlockSpec(a.shape, lambda i: (0,) * a.ndim)

    widths = [1024, 512, 1024, 1024, 1024, 1024, 512, 512, 512, 512, 512, 1024, 1024, 256]
    out_shape = [jax.ShapeDtypeStruct((n, w), BF16) for w in widths] + [jax.ShapeDtypeStruct((n, LANES), F32)]
    out_specs = [rows(w) for w in widths] + [rows(LANES)]
    return pl.pallas_call(
        _prep_kernel,
        grid=(n // tm,),
        in_specs=[rows(Y_WIDTH), table(512), table(256), table(512), whole(ones64)]
                 + [whole(g) for g in gains] + [whole(wq), whole(wkv)],
        out_specs=out_specs,
        out_shape=out_shape,
        compiler_params=_cparams(1),
        name="prep",
    )(y, ta, tb, td, ones64, *gains, wq, wkv)


def _softmax_chunk(s, vext, m_ref, acc_ref, idx):
    m_prev = m_ref[idx]
    m_new = jnp.maximum(m_prev, jnp.max(s, axis=1, keepdims=True))
    p = jnp.exp2(s - jnp.tile(m_new, (1, s.shape[1] // LANES)))
    alpha = jnp.exp2(m_prev - m_new)
    acc_ref[idx] = jnp.tile(alpha, (1, VEXT // LANES)) * acc_ref[idx] + _dot(p.astype(BF16), vext)
    m_ref[idx] = m_new


def _init_softmax(m_ref, acc_ref):
    m_ref[...] = jnp.full(m_ref.shape, NEG, F32)
    acc_ref[...] = jnp.zeros(acc_ref.shape, F32)


def _normalised(acc_ref, idx):
    acc = acc_ref[idx]
    return acc[:, :LANES] / acc[:, LANES:]


def _causal_mask(row0, col0, tq, tkc, strict=False):
    rows = row0 + lax.broadcasted_iota(jnp.int32, (tq, tkc), 0)
    cols = col0 + lax.broadcasted_iota(jnp.int32, (tq, tkc), 1)
    return cols < rows if strict else cols <= rows


def _attn_diff_kernel(lam_ref, sub_ref, q_ref, k_ref, v_ref, o_ref, m_ref, acc_ref, *, tq, lam_init):
    qi = pl.program_id(1)
    _init_softmax(m_ref, acc_ref)

    def chunk(off, mask):
        for mp in range(2 * GROUP_HEADS):
            h = mp // 2
            q = q_ref[0, :, mp * LANES:(mp + 1) * LANES]
            k = k_ref[0, pl.ds(off, tq), h * LANES:(h + 1) * LANES]
            v = v_ref[0, pl.ds(off, tq), h * VEXT:(h + 1) * VEXT]
            s = _dot_nt(q, k)
            if mask is not None:
                s = jnp.where(mask, s, NEG)
            _softmax_chunk(s, v, m_ref, acc_ref, mp)

    def full_chunk(c, carry):
        chunk(pl.multiple_of(c * tq, tq), None)
        return carry

    lax.fori_loop(0, qi, full_chunk, 0)
    diag = pl.multiple_of(qi * tq, tq)
    chunk(diag, _causal_mask(0, 0, tq, tq))

    lv = lam_ref[...]
    lam = (jnp.exp(jnp.sum(lv[0:1] * lv[1:2], axis=-1, keepdims=True))
           - jnp.exp(jnp.sum(lv[2:3] * lv[3:4], axis=-1, keepdims=True)) + lam_init)
    for h in range(GROUP_HEADS):
        o = _normalised(acc_ref, 2 * h) - lam * _normalised(acc_ref, 2 * h + 1)
        o = _rms_rows(o, sub_ref[...]) * (1.0 - lam_init)
        o_ref[0, :, h * LANES:(h + 1) * LANES] = o.astype(o_ref.dtype)


def _attn_mla_kernel(q_ref, k_ref, v_ref, o_ref, m_ref, acc_ref, *, tq):
    qi = pl.program_id(1)
    _init_softmax(m_ref, acc_ref)

    def chunk(off, mask):
        for h in range(GROUP_HEADS):
            q = q_ref[0, :, h * MLA_PAD_DIM:(h + 1) * MLA_PAD_DIM]
            k = k_ref[0, pl.ds(off, tq), h * MLA_PAD_DIM:(h + 1) * MLA_PAD_DIM]
            v = v_ref[0, pl.ds(off, tq), h * VEXT:(h + 1) * VEXT]
            s = _dot_nt(q, k)
            if mask is not None:
                s = jnp.where(mask, s, NEG)
            _softmax_chunk(s, v, m_ref, acc_ref, h)

    def full_chunk(c, carry):
        chunk(pl.multiple_of(c * tq, tq), None)
        return carry

    lax.fori_loop(0, qi, full_chunk, 0)
    chunk(pl.multiple_of(qi * tq, tq), _causal_mask(0, 0, tq, tq))
    for h in range(GROUP_HEADS):
        o_ref[0, :, h * LANES:(h + 1) * LANES] = _normalised(acc_ref, h).astype(o_ref.dtype)


def _attn_stick_kernel(tri_ref, q_ref, k_ref, v_ref, o_ref, carry_ref, acc_ref, *, tq, tkc):
    qi = pl.program_id(1)
    carry_ref[...] = jnp.zeros(carry_ref.shape, F32)
    acc_ref[...] = jnp.zeros(acc_ref.shape, F32)

    def chunk(off, mask):
        tri = tri_ref[...]
        for h in range(GROUP_HEADS):
            q = q_ref[0, :, h * LANES:(h + 1) * LANES]
            k = k_ref[0, pl.ds(off, tkc), h * LANES:(h + 1) * LANES]
            v = v_ref[0, pl.ds(off, tkc), h * LANES:(h + 1) * LANES]
            z = _dot_nt(q, k)
            log_b = jnp.minimum(z, 0.0) - jnp.log2(1.0 + jnp.exp2(-jnp.abs(z)))
            log_1mb = log_b - z
            if mask is not None:
                log_1mb = jnp.where(mask, log_1mb, 0.0)
            hi = log_1mb.astype(BF16)
            lo = (log_1mb - hi.astype(F32)).astype(BF16)
            inner = _dot(hi, tri) + _dot(lo, tri)
            carry = carry_ref[h]
            a = jnp.exp2(log_b + inner + jnp.tile(carry, (1, tkc // LANES)))
            if mask is not None:
                a = jnp.where(mask, a, 0.0)
            acc_ref[h] += _dot(a.astype(BF16), v)
            carry_ref[h] = carry + (inner[:, 0:1] + log_1mb[:, 0:1])

    n_diag = tq // tkc
    for j in range(n_diag):
        col0 = (n_diag - 1 - j) * tkc
        chunk(pl.multiple_of(qi * tq + col0, tkc), _causal_mask(0, col0, tq, tkc, strict=True))

    n_full = qi * n_diag

    def full_chunk(state):
        c, _ = state
        chunk(pl.multiple_of((n_full - 1 - c) * tkc, tkc), None)
        return c + 1, (jnp.max(carry_ref[...]) > STICK_DEAD_LOG2).astype(jnp.int32)

    lax.while_loop(lambda state: jnp.logical_and(state[0] < n_full, state[1] > 0), full_chunk,
                   (jnp.int32(0), jnp.int32(1)))
    for h in range(GROUP_HEADS):
        o_ref[0, :, h * LANES:(h + 1) * LANES] = acc_ref[h].astype(o_ref.dtype)


def _resident_attention(kernel, q, k, v, extra, *, tq, scratch, name):
    b, s, _ = q.shape
    q_map = lambda bi, qi: (bi, qi, 0)
    all_map = lambda bi, qi: (bi, 0, 0)
    extra_specs = [pl.BlockSpec(e.shape, lambda bi, qi, nd=e.ndim: (0,) * nd) for e in extra]
    return pl.pallas_call(
        kernel,
        grid=(b, s // tq),
        in_specs=extra_specs + [pl.BlockSpec((1, tq, q.shape[2]), q_map),
                                pl.BlockSpec((1, s, k.shape[2]), all_map),
                                pl.BlockSpec((1, s, v.shape[2]), all_map)],
        out_specs=pl.BlockSpec((1, tq, GROUP_WIDTH), q_map),
        out_shape=jax.ShapeDtypeStruct((b, s, GROUP_WIDTH), BF16),
        scratch_shapes=scratch,
        compiler_params=_cparams(2),
        name=name,
    )(*extra, q, k, v)


def _attn_dsa_kernel(q_ref, k_ref, v_ref, iq_ref, ik_ref, iw_ref, before_ref, o_ref, key_ref, half_ref, m_ref, acc_ref,
                     *, tq, topk):
    qi = pl.program_id(1)
    nkc = qi + 1
    n_groups = tq // LANES

    def score_chunk(c, mask):
        off = pl.multiple_of(c * tq, tq)
        ikc = ik_ref[0, pl.ds(off, tq), :]
        iw = iw_ref[0]
        isc = jnp.zeros((tq, tq), F32)
        for hp in range(IDX_HEADS // 2):
            iqp = iq_ref[0, :, hp * LANES:(hp + 1) * LANES]
            for e in range(2):
                hh = 2 * hp + e
                s = _dot_nt(iqp, ikc[:, e * LANES:(e + 1) * LANES])
                isc = isc + jnp.maximum(s, 0.0) * iw[:, hh:hh + 1]
        bits = lax.bitcast_convert_type(isc, jnp.int32)
        key = jnp.where(bits < 0, bits ^ jnp.int32(0x7FFFFFFF), bits)
        if mask is not None:
            key = jnp.where(mask, key, INT_MIN)
        key_ref[c] = key
        half_ref[c] = (key >> 16).astype(jnp.int16)

    def full_score_chunk(c, carry):
        score_chunk(c, None)
        return carry

    lax.fori_loop(0, qi, full_score_chunk, 0)
    score_chunk(qi, _causal_mask(0, 0, tq, tq))

    def row_count(flags_of_chunk):
        def body(c, part):
            flags = flags_of_chunk(c)
            for g in range(n_groups):
                part = part + flags[:, g * LANES:(g + 1) * LANES]
            return part

        part = lax.fori_loop(0, nkc, body, jnp.zeros((tq, LANES), jnp.int16))
        return jnp.sum(part.astype(F32), axis=-1, keepdims=True)

    one16, zero16 = jnp.int16(1), jnp.int16(0)

    def search16(rank, cnt_init):
        def bit(i, carry):
            thr_u, cnt_thr = carry
            cand_u = thr_u | lax.shift_left(jnp.int32(1), 15 - i)
            cand = (cand_u - 32768).astype(jnp.int16)
            cnt = row_count(lambda c: jnp.where(half_ref[c] >= cand, one16, zero16))
            take = cnt >= rank
            return jnp.where(take, cand_u, thr_u), jnp.where(take, cnt, cnt_thr)

        return lax.fori_loop(0, 16, bit, (jnp.zeros((tq, 1), jnp.int32), cnt_init))

    k_f = jnp.full((tq, 1), float(topk), F32)
    hi_u, cnt_hi = search16(k_f, jnp.zeros((tq, 1), F32))
    hi_thr = (hi_u - 32768).astype(jnp.int16)
    cnt_gt = row_count(lambda c: jnp.where(half_ref[c] > hi_thr, one16, zero16))

    def low_halves(c, carry):
        lo = ((key_ref[c] & 0xFFFF) - 32768).astype(jnp.int16)
        half_ref[c] = jnp.where(half_ref[c] == hi_thr, lo, jnp.int16(-32768))
        return carry

    lax.fori_loop(0, nkc, low_halves, 0)
    lo_u, cnt_lo = search16(k_f - cnt_gt, cnt_hi - cnt_gt)
    thr_raw = ((hi_u << 16) | lo_u) ^ jnp.int32(INT_MIN)
    short_row = thr_raw == jnp.int32(INT_MIN)
    thr = jnp.maximum(thr_raw, jnp.int32(INT_MIN + 1))
    cnt_ge = cnt_gt + cnt_lo
    tied = jnp.logical_and(jnp.logical_not(short_row), cnt_ge > k_f)
    any_tied = jnp.max(jnp.where(tied, 1.0, 0.0)) > 0.0

    _init_softmax(m_ref, acc_ref)

    def attend(c, sel):
        off = pl.multiple_of(c * tq, tq)
        for h in range(GROUP_HEADS):
            q = q_ref[0, :, h * LANES:(h + 1) * LANES]
            k = k_ref[0, pl.ds(off, tq), h * LANES:(h + 1) * LANES]
            v = v_ref[0, pl.ds(off, tq), h * VEXT:(h + 1) * VEXT]
            s = jnp.where(sel, _dot_nt(q, k), NEG)
            _softmax_chunk(s, v, m_ref, acc_ref, h)

    @pl.when(jnp.logical_not(any_tied))
    def _():
        def attend_chunk(c, carry):
            attend(c, key_ref[c] >= thr)
            return carry

        lax.fori_loop(0, nkc, attend_chunk, 0)

    @pl.when(any_tied)
    def _():
        def eq_chunk(c, part):
            eq = jnp.where(key_ref[c] == thr, 1.0, 0.0)
            for g in range(n_groups):
                part = part + eq[:, g * LANES:(g + 1) * LANES]
            return part

        n_eq = jnp.sum(lax.fori_loop(0, nkc, eq_chunk, jnp.zeros((tq, LANES), F32)), axis=-1, keepdims=True)
        keep = jnp.where(short_row, 0.0, k_f - (cnt_ge - n_eq))

        def attend_chunk(c, seen):
            key = key_ref[c]
            eq = key == thr
            eq_f = jnp.where(eq, 1.0, 0.0)
            rank = seen + _dot(eq_f.astype(BF16), before_ref[...])
            sel = jnp.logical_or(key > thr, jnp.logical_and(eq, rank < keep))
            attend(c, sel)
            return seen + jnp.sum(eq_f, axis=-1, keepdims=True)

        lax.fori_loop(0, nkc, attend_chunk, jnp.zeros((tq, 1), F32))

    for h in range(GROUP_HEADS):
        o_ref[0, :, h * LANES:(h + 1) * LANES] = _normalised(acc_ref, h).astype(o_ref.dtype)


def _attn_dsa(q, k, v, iq, ik, iw, *, tq):
    b, s, _ = q.shape
    topk = min(TOPK_MAX, s // 4)
    before = (jnp.arange(tq)[:, None] < jnp.arange(tq)[None, :]).astype(BF16)
    q_map = lambda bi, qi: (bi, qi, 0)
    all_map = lambda bi, qi: (bi, 0, 0)
    return pl.pallas_call(
        functools.partial(_attn_dsa_kernel, tq=tq, topk=topk),
        grid=(b, s // tq),
        in_specs=[pl.BlockSpec((1, tq, GROUP_WIDTH), q_map),
                  pl.BlockSpec((1, s, GROUP_WIDTH), all_map),
                  pl.BlockSpec((1, s, GROUP_HEADS * VEXT), all_map),
                  pl.BlockSpec((1, tq, IDX_HEADS * IDX_DIM), q_map),
                  pl.BlockSpec((1, s, 2 * LANES), all_map),
                  pl.BlockSpec((1, tq, LANES), q_map),
                  pl.BlockSpec((tq, tq), lambda bi, qi: (0, 0))],
        out_specs=pl.BlockSpec((1, tq, GROUP_WIDTH), q_map),
        out_shape=jax.ShapeDtypeStruct((b, s, GROUP_WIDTH), BF16),
        scratch_shapes=[pltpu.VMEM((s // tq, tq, tq), jnp.int32),
                        pltpu.VMEM((s // tq, tq, tq), jnp.int16),
                        pltpu.VMEM((GROUP_HEADS, tq, LANES), F32),
                        pltpu.VMEM((GROUP_HEADS, tq, VEXT), F32)],
        compiler_params=_cparams(2),
        name="attn_dsa",
    )(q, k, v, iq, ik, iw, before)


def _rope_tables(seq, ncols, group, start, rot):
    half = rot // 2
    inv = ROPE_THETA ** (-jnp.arange(half, dtype=F32) * 2.0 / rot)
    j = jnp.arange(ncols) % group - start
    in_lo = (j >= 0) & (j < half)
    in_hi = (j >= half) & (j < rot)
    idx = jnp.clip(jnp.where(in_hi, j - half, j), 0, half - 1)
    ang = jnp.arange(seq, dtype=jnp.int32).astype(F32)[:, None] * inv[idx][None, :]
    cos, sin = jnp.cos(ang), jnp.sin(ang)
    return jnp.stack([jnp.where(in_lo | in_hi, cos, 1.0), jnp.where(in_lo, -sin, 0.0), jnp.where(in_hi, sin, 0.0)])


def _pad_cols(a, width):
    return jnp.pad(a, ((0, 0), (0, width - a.shape[1])))


def _relayout_w_in(w):
    sizes = (512,) * 3 + (MLA_Q_RANK, MLA_KV_RANK, MLA_ROPE_DIM) + (512,) * 6 + (IDX_HEADS * IDX_DIM, IDX_DIM, IDX_HEADS)
    names = ("a_q", "a_k", "a_v", "b_cq", "b_ckv", "b_kr", "c_q", "c_k", "c_v", "d_q", "d_k", "d_v", "d_iq", "d_ik", "d_iw")
    parts, start = {}, 0
    for nme, sz in zip(names, sizes):
        parts[nme] = w[:, start:start + sz]
        start += sz
    order = sorted(_OFF, key=_OFF.get)
    ends = [_OFF[nme] for nme in order[1:]] + [Y_WIDTH]
    return jnp.concatenate([_pad_cols(parts[nme], end - _OFF[nme]) for nme, end in zip(order, ends)], axis=1).astype(BF16)


def _relayout_mla_q(w):
    w = w.reshape(w.shape[0], GROUP_HEADS, MLA_QK_DIM)
    w = jnp.pad(w, ((0, 0), (0, 0), (0, MLA_PAD_DIM - MLA_QK_DIM)))
    return w.reshape(w.shape[0], GROUP_HEADS * MLA_PAD_DIM)


def _tile_row(g, reps):
    return jnp.tile(g.reshape(1, -1), (1, reps))


def kernel(x, attn_norm, w_in, diff_qk_norm, diff_lambda, diff_subln, mla_q_a_norm, mla_wq_b, mla_kv_a_norm,
           mla_wkv_b, mla_qk_norm, dsa_qk_norm, idx_k_norm, w_o, ffn_norm, w_gate, w_up, w_down):
    b, s, d = x.shape
    n = b * s
    depth = w_in.shape[0]
    tm = min(1024, n)
    tq = min(512, s)
    tabs = (_rope_tables(s, 512, DIFF_SUB_DIM, 0, DIFF_SUB_DIM // 4),
            _rope_tables(s, MLA_PAD_DIM, MLA_PAD_DIM, MLA_NOPE_DIM, MLA_ROPE_DIM),
            _rope_tables(s, 512, HEAD_DIM, 0, HEAD_DIM // 4))
    ones64 = jnp.kron(jnp.eye(512 // DIFF_SUB_DIM, dtype=F32), jnp.ones((DIFF_SUB_DIM, DIFF_SUB_DIM), F32)).astype(BF16)
    tk_stick = min(256, s)
    tri = (jnp.arange(tk_stick)[:, None] > jnp.arange(tk_stick)[None, :]).astype(BF16)

    xf = x.reshape(n, d)
    for l in range(depth):
        lam_init = 0.8 - 0.6 * math.exp(-0.3 * l)
        y = _rms_matmul(xf, attn_norm[l].reshape(1, d), _relayout_w_in(w_in[l]), tm=tm, tn=512, out_dtype=F32)

        wq = jnp.pad(_relayout_mla_q(mla_wq_b[l]), ((0, 512 - MLA_Q_RANK), (0, 0))).astype(BF16)
        wkv = mla_wkv_b[l].reshape(MLA_KV_RANK, GROUP_HEADS, 2, HEAD_DIM).transpose(0, 2, 1, 3)
        wkv = wkv.reshape(MLA_KV_RANK, 2 * GROUP_WIDTH).astype(BF16)
        gains = [
            _tile_row(diff_qk_norm[l, 0], 8), _tile_row(diff_qk_norm[l, 1], 8),
            _pad_cols(mla_q_a_norm[l].reshape(1, -1), 512), mla_kv_a_norm[l].reshape(1, -1),
            _relayout_mla_q(mla_qk_norm[l, 0].reshape(1, MLA_QK_DIM).repeat(GROUP_HEADS, 0).reshape(1, -1)),
            _pad_cols(mla_qk_norm[l, 1].reshape(1, -1), MLA_PAD_DIM),
            _tile_row(dsa_qk_norm[l, 0], 4), _tile_row(dsa_qk_norm[l, 1], 4),
            _pad_cols(idx_k_norm[l].reshape(1, -1), LANES),
        ]
        (qa, ka, va, qb, kb, vb, qc, kc, vc, qd, kd, vd, iq, ik, iw) = [
            a.reshape(b, s, a.shape[1]) for a in _prep(y, s, tabs, ones64, gains, wq, wkv, tm=min(256, s))]

        o_a = _resident_attention(
            functools.partial(_attn_diff_kernel, tq=tq, lam_init=lam_init), qa, ka, va,
            [diff_lambda[l], diff_subln[l].reshape(1, HEAD_DIM)], tq=tq,
            scratch=[pltpu.VMEM((8, tq, LANES), F32), pltpu.VMEM((8, tq, VEXT), F32)], name="attn_diff")
        o_b = _resident_attention(
            functools.partial(_attn_mla_kernel, tq=tq), qb, kb, vb, [], tq=tq,
            scratch=[pltpu.VMEM((4, tq, LANES), F32), pltpu.VMEM((4, tq, VEXT), F32)], name="attn_mla")
        o_c = _resident_attention(
            functools.partial(_attn_stick_kernel, tq=tq, tkc=tk_stick), qc, kc, vc, [tri], tq=tq,
            scratch=[pltpu.VMEM((4, tq, LANES), F32), pltpu.VMEM((4, tq, LANES), F32)], name="attn_stick")
        o_d = _attn_dsa(qd, kd, vd, iq, ik, iw, tq=tq)

        mixed = [o.reshape(n, GROUP_WIDTH) for o in (o_a, o_b, o_c, o_d)]
        xf = _matmul_residual(mixed, w_o[l].astype(BF16), xf, tm=tm, tn=512)
        act = _ffn_up(xf, ffn_norm[l].reshape(1, d), w_gate[l].astype(BF16), w_up[l].astype(BF16), tm=tm, tn=512)
        xf = _matmul_residual([act], w_down[l].astype(BF16), xf, tm=tm, tn=512)
    return xf.reshape(b, s, d)
```

```python
import functools
import math

import jax
import jax.numpy as jnp
from jax import lax
from jax.experimental import pallas as pl
from jax.experimental.pallas import tpu as pltpu

F32 = jnp.float32
BF16 = jnp.bfloat16

HEAD_DIM = 128
GROUP_HEADS = 4
GROUP_WIDTH = GROUP_HEADS * HEAD_DIM
ROPE_THETA = 500000.0
EPS = 1e-6
DIFF_SUB_DIM = 64
MLA_Q_RANK = 448
MLA_KV_RANK = 128
MLA_NOPE_DIM = 128
MLA_ROPE_DIM = 64
MLA_QK_DIM = MLA_NOPE_DIM + MLA_ROPE_DIM
MLA_PAD_DIM = 256
IDX_HEADS = 16
IDX_DIM = 64
TOPK_MAX = 256

LANES = 128
VEXT = 2 * LANES
Y_WIDTH = 13 * GROUP_WIDTH
NEG = -1e30
INT_MIN = -2147483648
LOG2E = 1.4426950408889634
COUNT_UNROLL = 2
COUNT_ACC_ROWS = 64
STICK_DEAD_LOG2 = -160.0
VMEM_LIMIT = 56 * 1024 * 1024

_OFF = dict(a_q=0, a_k=512, a_v=1024, b_cq=1536, b_ckv=2048, b_kr=2176, d_ik=2304, d_iw=2432,
            c_q=2560, c_k=3072, c_v=3584, d_q=4096, d_k=4608, d_v=5120, d_iq=5632)


def _cparams(n_axes):
    return pltpu.CompilerParams(dimension_semantics=("arbitrary",) * n_axes, vmem_limit_bytes=VMEM_LIMIT)


def _dot(a, b):
    return jnp.dot(a, b, preferred_element_type=F32)


def _dot_nt(a, b):
    return lax.dot_general(a, b, (((1,), (1,)), ((), ())), preferred_element_type=F32)


def _rms_rows(x, g):
    ms = jnp.mean(x * x, axis=-1, keepdims=True)
    return x * lax.rsqrt(ms + EPS) * g


def _rms_matmul_kernel(x_ref, g_ref, w_ref, o_ref, h_ref):
    @pl.when(pl.program_id(1) == 0)
    def _():
        h_ref[...] = _rms_rows(x_ref[...], g_ref[...]).astype(BF16)

    o_ref[...] = _dot(h_ref[...], w_ref[...]).astype(o_ref.dtype)


def _rms_matmul(x, g, w, *, tm, tn, out_dtype):
    n, d = x.shape
    m = w.shape[1]
    return pl.pallas_call(
        _rms_matmul_kernel,
        grid=(n // tm, m // tn),
        in_specs=[pl.BlockSpec((tm, d), lambda i, j: (i, 0)),
                  pl.BlockSpec((1, d), lambda i, j: (0, 0)),
                  pl.BlockSpec((d, tn), lambda i, j: (0, j))],
        out_specs=pl.BlockSpec((tm, tn), lambda i, j: (i, j)),
        out_shape=jax.ShapeDtypeStruct((n, m), out_dtype),
        scratch_shapes=[pltpu.VMEM((tm, d), BF16)],
        compiler_params=_cparams(2),
        name="rms_matmul",
    )(x, g, w)


def _ffn_up_kernel(x_ref, g_ref, wg_ref, wu_ref, o_ref, h_ref):
    @pl.when(pl.program_id(1) == 0)
    def _():
        h_ref[...] = _rms_rows(x_ref[...], g_ref[...]).astype(BF16)

    h = h_ref[...]
    a = _dot(h, wg_ref[...])
    b = _dot(h, wu_ref[...])
    o_ref[...] = (jax.nn.silu(a) * b).astype(o_ref.dtype)


def _ffn_up(x, g, wg, wu, *, tm, tn):
    n, d = x.shape
    m = wg.shape[1]
    return pl.pallas_call(
        _ffn_up_kernel,
        grid=(n // tm, m // tn),
        in_specs=[pl.BlockSpec((tm, d), lambda i, j: (i, 0)),
                  pl.BlockSpec((1, d), lambda i, j: (0, 0)),
                  pl.BlockSpec((d, tn), lambda i, j: (0, j)),
                  pl.BlockSpec((d, tn), lambda i, j: (0, j))],
        out_specs=pl.BlockSpec((tm, tn), lambda i, j: (i, j)),
        out_shape=jax.ShapeDtypeStruct((n, m), BF16),
        scratch_shapes=[pltpu.VMEM((tm, d), BF16)],
        compiler_params=_cparams(2),
        name="ffn_up",
    )(x, g, wg, wu)


def _matmul_residual_kernel(*refs):
    a_refs, (w_ref, r_ref, o_ref) = refs[:-3], refs[-3:]
    acc = r_ref[...]
    row = 0
    for a_ref in a_refs:
        k = a_ref.shape[1]
        acc = acc + _dot(a_ref[...], w_ref[row:row + k, :])
        row += k
    o_ref[...] = acc


def _matmul_residual(parts, w, r, *, tm, tn):
    n = parts[0].shape[0]
    k, m = w.shape
    return pl.pallas_call(
        _matmul_residual_kernel,
        grid=(n // tm, m // tn),
        in_specs=[pl.BlockSpec((tm, a.shape[1]), lambda i, j: (i, 0)) for a in parts]
                 + [pl.BlockSpec((k, tn), lambda i, j: (0, j)),
                    pl.BlockSpec((tm, tn), lambda i, j: (i, j))],
        out_specs=pl.BlockSpec((tm, tn), lambda i, j: (i, j)),
        out_shape=jax.ShapeDtypeStruct((n, m), F32),
        compiler_params=_cparams(2),
        name="matmul_residual",
    )(*parts, w, r)


def _rope(x, t_ref, half):
    n = x.shape[1]
    return (x * t_ref[0] + pltpu.roll(x, n - half, 1) * t_ref[1] + pltpu.roll(x, half, 1) * t_ref[2])


def _rms_lane_groups(x, width, count):
    outs = []
    for h in range(x.shape[1] // width):
        xh = x[:, h * width:(h + 1) * width]
        ms = jnp.sum(xh * xh, axis=-1, keepdims=True) * (1.0 / count)
        outs.append(xh * lax.rsqrt(ms + EPS))
    return outs[0] if len(outs) == 1 else jnp.concatenate(outs, axis=1)


def _rms_sub64(x, ones_ref):
    x2 = x * x
    hi = x2.astype(BF16)
    lo = (x2 - hi.astype(F32)).astype(BF16)
    ms = (_dot(hi, ones_ref[...]) + _dot(lo, ones_ref[...])) * (1.0 / DIFF_SUB_DIM)
    return x * lax.rsqrt(ms + EPS)


def _store_vext(ref, v):
    ones = jnp.ones((v.shape[0], LANES), BF16)
    for h in range(GROUP_HEADS):
        ref[:, h * VEXT:h * VEXT + LANES] = v[:, h * LANES:(h + 1) * LANES].astype(BF16)
        ref[:, h * VEXT + LANES:(h + 1) * VEXT] = ones


def _prep_kernel(y_ref, ta_ref, tb_ref, td_ref, ones_ref, gaq_ref, gak_ref, gcq_ref, gckv_ref, gbq_ref, gbk_ref,
                 gdq_ref, gdk_ref, gik_ref, wq_ref, wkv_ref,
                 qa_ref, ka_ref, va_ref, qb_ref, kb_ref, vb_ref, qc_ref, kc_ref, vc_ref,
                 qd_ref, kd_ref, vd_ref, iq_ref, ik_ref, iw_ref):
    def sec(name, width):
        return y_ref[:, _OFF[name]:_OFF[name] + width]

    tm = y_ref.shape[0]
    lane = lax.broadcasted_iota(jnp.int32, (tm, LANES), 1)
    first_half = lane < DIFF_SUB_DIM

    qa = _rope(_rms_sub64(sec("a_q", 512), ones_ref) * gaq_ref[...], ta_ref, 8) * (DIFF_SUB_DIM ** -0.5 * LOG2E)
    for h in range(GROUP_HEADS):
        qh = qa[:, h * LANES:(h + 1) * LANES]
        qa_ref[:, (2 * h) * LANES:(2 * h + 1) * LANES] = jnp.where(first_half, qh, 0.0).astype(BF16)
        qa_ref[:, (2 * h + 1) * LANES:(2 * h + 2) * LANES] = jnp.where(first_half, 0.0, qh).astype(BF16)
    ka_ref[...] = _rope(_rms_sub64(sec("a_k", 512), ones_ref) * gak_ref[...], ta_ref, 8).astype(BF16)
    _store_vext(va_ref, sec("a_v", 512))

    cq = sec("b_cq", 512)
    cq = cq * lax.rsqrt(jnp.sum(cq * cq, axis=-1, keepdims=True) * (1.0 / MLA_Q_RANK) + EPS) * gcq_ref[...]
    qb = _dot(cq.astype(BF16), wq_ref[...])
    qb = _rms_lane_groups(qb, MLA_PAD_DIM, MLA_QK_DIM) * gbq_ref[...]
    ckv = _rms_lane_groups(sec("b_ckv", 128), 128, MLA_KV_RANK) * gckv_ref[...]
    kv = _dot(ckv.astype(BF16), wkv_ref[...])
    kr = sec("b_kr", 128)
    kr_ss = jnp.sum(kr * kr, axis=-1, keepdims=True)
    for h in range(GROUP_HEADS):
        lo, hi = h * MLA_PAD_DIM, (h + 1) * MLA_PAD_DIM
        qb_ref[:, lo:hi] = (_rope(qb[:, lo:hi], tb_ref, 32) * (MLA_QK_DIM ** -0.5 * LOG2E)).astype(BF16)
        kn = kv[:, h * LANES:(h + 1) * LANES]
        ms = (jnp.sum(kn * kn, axis=-1, keepdims=True) + kr_ss) * (1.0 / MLA_QK_DIM)
        kh = jnp.concatenate([kn, kr], axis=1) * lax.rsqrt(ms + EPS) * gbk_ref[...]
        kb_ref[:, lo:hi] = _rope(kh, tb_ref, 32).astype(BF16)
    _store_vext(vb_ref, kv[:, GROUP_WIDTH:])

    qc_ref[...] = (sec("c_q", 512) * (HEAD_DIM ** -0.5 * LOG2E)).astype(BF16)
    kc_ref[...] = sec("c_k", 512).astype(BF16)
    vc_ref[...] = sec("c_v", 512).astype(BF16)

    qd = _rope(_rms_lane_groups(sec("d_q", 512), HEAD_DIM, HEAD_DIM) * gdq_ref[...], td_ref, 16)
    qd_ref[...] = (qd * (HEAD_DIM ** -0.5 * LOG2E)).astype(BF16)
    kd_ref[...] = _rope(_rms_lane_groups(sec("d_k", 512), HEAD_DIM, HEAD_DIM) * gdk_ref[...], td_ref, 16).astype(BF16)
    _store_vext(vd_ref, sec("d_v", 512))
    for half in range(2):
        iq = y_ref[:, _OFF["d_iq"] + half * 512:_OFF["d_iq"] + (half + 1) * 512]
        iq_ref[:, half * 512:(half + 1) * 512] = (_rope(iq, ta_ref, 8) * (IDX_DIM ** -0.5)).astype(BF16)
    ik = _rms_lane_groups(sec("d_ik", 128), 128, IDX_DIM) * gik_ref[...]
    ik = (ik * ta_ref[0, :, :LANES] + pltpu.roll(ik, LANES - 8, 1) * ta_ref[1, :, :LANES]
          + pltpu.roll(ik, 8, 1) * ta_ref[2, :, :LANES])
    ik_ref[:, :LANES] = ik.astype(BF16)
    ik_ref[:, LANES:] = pltpu.roll(ik, IDX_DIM, 1).astype(BF16)
    iw_ref[...] = sec("d_iw", 128) * (IDX_HEADS ** -0.5)


def _prep(y, seq, tabs, ones64, gains, wq, wkv, *, tm):
    n = y.shape[0]
    nblk_seq = seq // tm
    ta, tb, td = tabs

    def rows(width):
        return pl.BlockSpec((tm, width), lambda i: (i, 0))

    def table(width):
        return pl.BlockSpec((3, tm, width), lambda i: (0, i % nblk_seq, 0))

    def whole(a):
        return pl.BlockSpec(a.shape, lambda i: (0,) * a.ndim)

    widths = [1024, 512, 1024, 1024, 1024, 1024, 512, 512, 512, 512, 512, 1024, 1024, 256]
    out_shape = [jax.ShapeDtypeStruct((n, w), BF16) for w in widths] + [jax.ShapeDtypeStruct((n, LANES), F32)]
    out_specs = [rows(w) for w in widths] + [rows(LANES)]
    return pl.pallas_call(
        _prep_kernel,
        grid=(n // tm,),
        in_specs=[rows(Y_WIDTH), table(512), table(256), table(512), whole(ones64)]
                 + [whole(g) for g in gains] + [whole(wq), whole(wkv)],
        out_specs=out_specs,
        out_shape=out_shape,
        compiler_params=_cparams(1),
        name="prep",
    )(y, ta, tb, td, ones64, *gains, wq, wkv)


def _softmax_chunk(s, vext, m_ref, acc_ref, idx):
    m_prev = m_ref[idx]
    m_new = jnp.maximum(m_prev, jnp.max(s, axis=1, keepdims=True))
    p = jnp.exp2(s - jnp.tile(m_new, (1, s.shape[1] // LANES)))
    alpha = jnp.exp2(m_prev - m_new)
    acc_ref[idx] = jnp.tile(alpha, (1, VEXT // LANES)) * acc_ref[idx] + _dot(p.astype(BF16), vext)
    m_ref[idx] = m_new


def _init_softmax(m_ref, acc_ref):
    m_ref[...] = jnp.full(m_ref.shape, NEG, F32)
    acc_ref[...] = jnp.zeros(acc_ref.shape, F32)


def _normalised(acc_ref, idx):
    acc = acc_ref[idx]
    return acc[:, :LANES] / acc[:, LANES:]


def _causal_mask(row0, col0, tq, tkc, strict=False):
    rows = row0 + lax.broadcasted_iota(jnp.int32, (tq, tkc), 0)
    cols = col0 + lax.broadcasted_iota(jnp.int32, (tq, tkc), 1)
    return cols < rows if strict else cols <= rows


def _attn_diff_kernel(lam_ref, sub_ref, q_ref, k_ref, v_ref, o_ref, m_ref, acc_ref, *, tq, lam_init):
    qi = pl.program_id(1)
    _init_softmax(m_ref, acc_ref)

    def chunk(off, mask):
        for mp in range(2 * GROUP_HEADS):
            h = mp // 2
            q = q_ref[0, :, mp * LANES:(mp + 1) * LANES]
            k = k_ref[0, pl.ds(off, tq), h * LANES:(h + 1) * LANES]
            v = v_ref[0, pl.ds(off, tq), h * VEXT:(h + 1) * VEXT]
            s = _dot_nt(q, k)
            if mask is not None:
                s = jnp.where(mask, s, NEG)
            _softmax_chunk(s, v, m_ref, acc_ref, mp)

    def full_chunk(c, carry):
        chunk(pl.multiple_of(c * tq, tq), None)
        return carry

    lax.fori_loop(0, qi, full_chunk, 0)
    diag = pl.multiple_of(qi * tq, tq)
    chunk(diag, _causal_mask(0, 0, tq, tq))

    lv = lam_ref[...]
    lam = (jnp.exp(jnp.sum(lv[0:1] * lv[1:2], axis=-1, keepdims=True))
           - jnp.exp(jnp.sum(lv[2:3] * lv[3:4], axis=-1, keepdims=True)) + lam_init)
    for h in range(GROUP_HEADS):
        o = _normalised(acc_ref, 2 * h) - lam * _normalised(acc_ref, 2 * h + 1)
        o = _rms_rows(o, sub_ref[...]) * (1.0 - lam_init)
        o_ref[0, :, h * LANES:(h + 1) * LANES] = o.astype(o_ref.dtype)


def _attn_mla_kernel(q_ref, k_ref, v_ref, o_ref, m_ref, acc_ref, *, tq):
    qi = pl.program_id(1)
    _init_softmax(m_ref, acc_ref)

    def chunk(off, mask):
        for h in range(GROUP_HEADS):
            q = q_ref[0, :, h * MLA_PAD_DIM:(h + 1) * MLA_PAD_DIM]
            k = k_ref[0, pl.ds(off, tq), h * MLA_PAD_DIM:(h + 1) * MLA_PAD_DIM]
            v = v_ref[0, pl.ds(off, tq), h * VEXT:(h + 1) * VEXT]
            s = _dot_nt(q, k)
            if mask is not None:
                s = jnp.where(mask, s, NEG)
            _softmax_chunk(s, v, m_ref, acc_ref, h)

    def full_chunk(c, carry):
        chunk(pl.multiple_of(c * tq, tq), None)
        return carry

    lax.fori_loop(0, qi, full_chunk, 0)
    chunk(pl.multiple_of(qi * tq, tq), _causal_mask(0, 0, tq, tq))
    for h in range(GROUP_HEADS):
        o_ref[0, :, h * LANES:(h + 1) * LANES] = _normalised(acc_ref, h).astype(o_ref.dtype)


def _attn_stick_kernel(tri_ref, q_ref, k_ref, v_ref, o_ref, carry_ref, acc_ref, *, tq, tkc):
    qi = pl.program_id(1)
    carry_ref[...] = jnp.zeros(carry_ref.shape, F32)
    acc_ref[...] = jnp.zeros(acc_ref.shape, F32)

    def chunk(off, mask):
        tri = tri_ref[...]
        for h in range(GROUP_HEADS):
            q = q_ref[0, :, h * LANES:(h + 1) * LANES]
            k = k_ref[0, pl.ds(off, tkc), h * LANES:(h + 1) * LANES]
            v = v_ref[0, pl.ds(off, tkc), h * LANES:(h + 1) * LANES]
            z = _dot_nt(q, k)
            log_b = jnp.minimum(z, 0.0) - jnp.log2(1.0 + jnp.exp2(-jnp.abs(z)))
            log_1mb = log_b - z
            if mask is not None:
                log_1mb = jnp.where(mask, log_1mb, 0.0)
            hi = log_1mb.astype(BF16)
            lo = (log_1mb - hi.astype(F32)).astype(BF16)
            inner = _dot(hi, tri) + _dot(lo, tri)
            carry = carry_ref[h]
            a = jnp.exp2(log_b + inner + jnp.tile(carry, (1, tkc // LANES)))
            if mask is not None:
                a = jnp.where(mask, a, 0.0)
            acc_ref[h] += _dot(a.astype(BF16), v)
            carry_ref[h] = carry + (inner[:, 0:1] + log_1mb[:, 0:1])

    n_diag = tq // tkc
    for j in range(n_diag):
        col0 = (n_diag - 1 - j) * tkc
        chunk(pl.multiple_of(qi * tq + col0, tkc), _causal_mask(0, col0, tq, tkc, strict=True))

    n_full = qi * n_diag

    def full_chunk(state):
        c, _ = state
        chunk(pl.multiple_of((n_full - 1 - c) * tkc, tkc), None)
        return c + 1, (jnp.max(carry_ref[...]) > STICK_DEAD_LOG2).astype(jnp.int32)

    lax.while_loop(lambda state: jnp.logical_and(state[0] < n_full, state[1] > 0), full_chunk,
                   (jnp.int32(0), jnp.int32(1)))
    for h in range(GROUP_HEADS):
        o_ref[0, :, h * LANES:(h + 1) * LANES] = acc_ref[h].astype(o_ref.dtype)


def _resident_attention(kernel, q, k, v, extra, *, tq, scratch, name):
    b, s, _ = q.shape
    q_map = lambda bi, qi: (bi, qi, 0)
    all_map = lambda bi, qi: (bi, 0, 0)
    extra_specs = [pl.BlockSpec(e.shape, lambda bi, qi, nd=e.ndim: (0,) * nd) for e in extra]
    return pl.pallas_call(
        kernel,
        grid=(b, s // tq),
        in_specs=extra_specs + [pl.BlockSpec((1, tq, q.shape[2]), q_map),
                                pl.BlockSpec((1, s, k.shape[2]), all_map),
                                pl.BlockSpec((1, s, v.shape[2]), all_map)],
        out_specs=pl.BlockSpec((1, tq, GROUP_WIDTH), q_map),
        out_shape=jax.ShapeDtypeStruct((b, s, GROUP_WIDTH), BF16),
        scratch_shapes=scratch,
        compiler_params=_cparams(2),
        name=name,
    )(*extra, q, k, v)


def _attn_dsa_kernel(q_ref, k_ref, v_ref, iq_ref, ik_ref, iw_ref, earlier_ref, o_ref, key_ref, half_ref, m_ref,
                     acc_ref, *, tq, topk):
    qi = pl.program_id(1)
    nkc = qi + 1
    iw_t = iw_ref[0].T

    def score_chunk(c, mask):
        off = pl.multiple_of(c * tq, tq)
        ikc = ik_ref[0, pl.ds(off, tq), :]
        isc = jnp.zeros((tq, tq), F32)
        for hp in range(IDX_HEADS // 2):
            iqp = iq_ref[0, :, hp * LANES:(hp + 1) * LANES]
            for e in range(2):
                hh = 2 * hp + e
                s = _dot_nt(ikc[:, e * LANES:(e + 1) * LANES], iqp)
                isc = isc + jnp.maximum(s, 0.0) * iw_t[hh:hh + 1, :]
        bits = lax.bitcast_convert_type(isc, jnp.int32)
        key = jnp.where(bits < 0, bits ^ jnp.int32(0x7FFFFFFF), bits)
        if mask is not None:
            key = jnp.where(mask, key, INT_MIN)
        key_ref[c] = key
        half_ref[c] = (key >> 16).astype(jnp.int16)

    def full_score_chunk(c, carry):
        score_chunk(c, None)
        return carry

    lax.fori_loop(0, qi, full_score_chunk, 0)
    key_pos = lax.broadcasted_iota(jnp.int32, (tq, tq), 0)
    query_pos = lax.broadcasted_iota(jnp.int32, (tq, tq), 1)
    score_chunk(qi, key_pos <= query_pos)

    one16, zero16, lowest16 = jnp.int16(1), jnp.int16(0), jnp.int16(-32768)

    @pl.when(nkc % COUNT_UNROLL == 1)
    def _():
        half_ref[nkc] = jnp.full((tq, tq), lowest16, jnp.int16)

    def row_count(bound, strict):
        def body(t, part):
            for u in range(COUNT_UNROLL):
                for r in range(tq // COUNT_ACC_ROWS):
                    half = half_ref[COUNT_UNROLL * t + u, r * COUNT_ACC_ROWS:(r + 1) * COUNT_ACC_ROWS, :]
                    part = part + jnp.where(half > bound if strict else half >= bound, one16, zero16)
            return part

        part = lax.fori_loop(0, (nkc + COUNT_UNROLL - 1) // COUNT_UNROLL, body,
                             jnp.zeros((COUNT_ACC_ROWS, tq), jnp.int16))
        return jnp.sum(part.astype(F32), axis=0, keepdims=True)

    def search16(rank, cnt_init):
        def bit(i, carry):
            thr_u, cnt_thr = carry
            cand_u = thr_u | lax.shift_left(jnp.int32(1), 15 - i)
            cnt = row_count((cand_u - 32768).astype(jnp.int16), strict=False)
            take = cnt >= rank
            return jnp.where(take, cand_u, thr_u), jnp.where(take, cnt, cnt_thr)

        return lax.fori_loop(0, 16, bit, (jnp.zeros((1, tq), jnp.int32), cnt_init))

    k_f = jnp.full((1, tq), float(topk), F32)
    hi_u, cnt_hi = search16(k_f, jnp.zeros((1, tq), F32))
    hi_thr = (hi_u - 32768).astype(jnp.int16)
    cnt_gt = row_count(hi_thr, strict=True)

    def low_halves(c, carry):
        lo = ((key_ref[c] & 0xFFFF) - 32768).astype(jnp.int16)
        half_ref[c] = jnp.where(half_ref[c] == hi_thr, lo, lowest16)
        return carry

    lax.fori_loop(0, nkc, low_halves, 0)
    lo_u, cnt_lo = search16(k_f - cnt_gt, cnt_hi - cnt_gt)
    thr_raw = ((hi_u << 16) | lo_u) ^ jnp.int32(INT_MIN)
    short_row = thr_raw == jnp.int32(INT_MIN)
    thr = jnp.maximum(thr_raw, jnp.int32(INT_MIN + 1))
    cnt_ge = cnt_gt + cnt_lo
    tied = jnp.logical_and(jnp.logical_not(short_row), cnt_ge > k_f)
    any_tied = jnp.max(jnp.where(tied, 1.0, 0.0)) > 0.0

    _init_softmax(m_ref, acc_ref)

    def attend(c, selected):
        bias = jnp.where(selected, 0.0, NEG).T
        off = pl.multiple_of(c * tq, tq)
        for h in range(GROUP_HEADS):
            q = q_ref[0, :, h * LANES:(h + 1) * LANES]
            k = k_ref[0, pl.ds(off, tq), h * LANES:(h + 1) * LANES]
            v = v_ref[0, pl.ds(off, tq), h * VEXT:(h + 1) * VEXT]
            _softmax_chunk(_dot_nt(q, k) + bias, v, m_ref, acc_ref, h)

    @pl.when(jnp.logical_not(any_tied))
    def _():
        def attend_chunk(c, carry):
            attend(c, key_ref[c] >= thr)
            return carry

        lax.fori_loop(0, nkc, attend_chunk, 0)

    @pl.when(any_tied)
    def _():
        def eq_chunk(c, n):
            return n + jnp.sum(jnp.where(key_ref[c] == thr, 1.0, 0.0), axis=0, keepdims=True)

        n_eq = lax.fori_loop(0, nkc, eq_chunk, jnp.zeros((1, tq), F32))
        keep = jnp.where(short_row, 0.0, k_f - (cnt_ge - n_eq))

        def attend_chunk(c, seen):
            key = key_ref[c]
            eq = key == thr
            eq_f = jnp.where(eq, 1.0, 0.0)
            rank = seen + _dot(earlier_ref[...], eq_f.astype(BF16))
            attend(c, jnp.logical_or(key > thr, jnp.logical_and(eq, rank < keep)))
            return seen + jnp.sum(eq_f, axis=0, keepdims=True)

        lax.fori_loop(0, nkc, attend_chunk, jnp.zeros((1, tq), F32))

    for h in range(GROUP_HEADS):
        o_ref[0, :, h * LANES:(h + 1) * LANES] = _normalised(acc_ref, h).astype(o_ref.dtype)


def _attn_dsa(q, k, v, iq, ik, iw, *, tq):
    b, s, _ = q.shape
    topk = min(TOPK_MAX, s // 4)
    earlier = (jnp.arange(tq)[:, None] > jnp.arange(tq)[None, :]).astype(BF16)
    q_map = lambda bi, qi: (bi, qi, 0)
    all_map = lambda bi, qi: (bi, 0, 0)
    return pl.pallas_call(
        functools.partial(_attn_dsa_kernel, tq=tq, topk=topk),
        grid=(b, s // tq),
        in_specs=[pl.BlockSpec((1, tq, GROUP_WIDTH), q_map),
                  pl.BlockSpec((1, s, GROUP_WIDTH), all_map),
                  pl.BlockSpec((1, s, GROUP_HEADS * VEXT), all_map),
                  pl.BlockSpec((1, tq, IDX_HEADS * IDX_DIM), q_map),
                  pl.BlockSpec((1, s, 2 * LANES), all_map),
                  pl.BlockSpec((1, tq, LANES), q_map),
                  pl.BlockSpec((tq, tq), lambda bi, qi: (0, 0))],
        out_specs=pl.BlockSpec((1, tq, GROUP_WIDTH), q_map),
        out_shape=jax.ShapeDtypeStruct((b, s, GROUP_WIDTH), BF16),
        scratch_shapes=[pltpu.VMEM((s // tq, tq, tq), jnp.int32),
                        pltpu.VMEM((-(-(s // tq) // COUNT_UNROLL) * COUNT_UNROLL, tq, tq), jnp.int16),
                        pltpu.VMEM((GROUP_HEADS, tq, LANES), F32),
                        pltpu.VMEM((GROUP_HEADS, tq, VEXT), F32)],
        compiler_params=_cparams(2),
        name="attn_dsa",
    )(q, k, v, iq, ik, iw, earlier)


def _rope_tables(seq, ncols, group, start, rot):
    half = rot // 2
    inv = ROPE_THETA ** (-jnp.arange(half, dtype=F32) * 2.0 / rot)
    j = jnp.arange(ncols) % group - start
    in_lo = (j >= 0) & (j < half)
    in_hi = (j >= half) & (j < rot)
    idx = jnp.clip(jnp.where(in_hi, j - half, j), 0, half - 1)
    ang = jnp.arange(seq, dtype=jnp.int32).astype(F32)[:, None] * inv[idx][None, :]
    cos, sin = jnp.cos(ang), jnp.sin(ang)
    return jnp.stack([jnp.where(in_lo | in_hi, cos, 1.0), jnp.where(in_lo, -sin, 0.0), jnp.where(in_hi, sin, 0.0)])


def _pad_cols(a, width):
    return jnp.pad(a, ((0, 0), (0, width - a.shape[1])))


def _relayout_w_in(w):
    sizes = (512,) * 3 + (MLA_Q_RANK, MLA_KV_RANK, MLA_ROPE_DIM) + (512,) * 6 + (IDX_HEADS * IDX_DIM, IDX_DIM, IDX_HEADS)
    names = ("a_q", "a_k", "a_v", "b_cq", "b_ckv", "b_kr", "c_q", "c_k", "c_v", "d_q", "d_k", "d_v", "d_iq", "d_ik", "d_iw")
    parts, start = {}, 0
    for nme, sz in zip(names, sizes):
        parts[nme] = w[:, start:start + sz]
        start += sz
    order = sorted(_OFF, key=_OFF.get)
    ends = [_OFF[nme] for nme in order[1:]] + [Y_WIDTH]
    return jnp.concatenate([_pad_cols(parts[nme], end - _OFF[nme]) for nme, end in zip(order, ends)], axis=1).astype(BF16)


def _relayout_mla_q(w):
    w = w.reshape(w.shape[0], GROUP_HEADS, MLA_QK_DIM)
    w = jnp.pad(w, ((0, 0), (0, 0), (0, MLA_PAD_DIM - MLA_QK_DIM)))
    return w.reshape(w.shape[0], GROUP_HEADS * MLA_PAD_DIM)


def _tile_row(g, reps):
    return jnp.tile(g.reshape(1, -1), (1, reps))


def kernel(x, attn_norm, w_in, diff_qk_norm, diff_lambda, diff_subln, mla_q_a_norm, mla_wq_b, mla_kv_a_norm,
           mla_wkv_b, mla_qk_norm, dsa_qk_norm, idx_k_norm, w_o, ffn_norm, w_gate, w_up, w_down):
    b, s, d = x.shape
    n = b * s
    depth = w_in.shape[0]
    tm = min(1024, n)
    tq = min(512, s)
    tabs = (_rope_tables(s, 512, DIFF_SUB_DIM, 0, DIFF_SUB_DIM // 4),
            _rope_tables(s, MLA_PAD_DIM, MLA_PAD_DIM, MLA_NOPE_DIM, MLA_ROPE_DIM),
            _rope_tables(s, 512, HEAD_DIM, 0, HEAD_DIM // 4))
    ones64 = jnp.kron(jnp.eye(512 // DIFF_SUB_DIM, dtype=F32), jnp.ones((DIFF_SUB_DIM, DIFF_SUB_DIM), F32)).astype(BF16)
    tk_stick = min(256, s)
    tri = (jnp.arange(tk_stick)[:, None] > jnp.arange(tk_stick)[None, :]).astype(BF16)

    xf = x.reshape(n, d)
    for l in range(depth):
        lam_init = 0.8 - 0.6 * math.exp(-0.3 * l)
        y = _rms_matmul(xf, attn_norm[l].reshape(1, d), _relayout_w_in(w_in[l]), tm=tm, tn=512, out_dtype=F32)

        wq = jnp.pad(_relayout_mla_q(mla_wq_b[l]), ((0, 512 - MLA_Q_RANK), (0, 0))).astype(BF16)
        wkv = mla_wkv_b[l].reshape(MLA_KV_RANK, GROUP_HEADS, 2, HEAD_DIM).transpose(0, 2, 1, 3)
        wkv = wkv.reshape(MLA_KV_RANK, 2 * GROUP_WIDTH).astype(BF16)
        gains = [
            _tile_row(diff_qk_norm[l, 0], 8), _tile_row(diff_qk_norm[l, 1], 8),
            _pad_cols(mla_q_a_norm[l].reshape(1, -1), 512), mla_kv_a_norm[l].reshape(1, -1),
            _relayout_mla_q(mla_qk_norm[l, 0].reshape(1, MLA_QK_DIM).repeat(GROUP_HEADS, 0).reshape(1, -1)),
            _pad_cols(mla_qk_norm[l, 1].reshape(1, -1), MLA_PAD_DIM),
            _tile_row(dsa_qk_norm[l, 0], 4), _tile_row(dsa_qk_norm[l, 1], 4),
            _pad_cols(idx_k_norm[l].reshape(1, -1), LANES),
        ]
        (qa, ka, va, qb, kb, vb, qc, kc, vc, qd, kd, vd, iq, ik, iw) = [
            a.reshape(b, s, a.shape[1]) for a in _prep(y, s, tabs, ones64, gains, wq, wkv, tm=min(256, s))]

        o_a = _resident_attention(
            functools.partial(_attn_diff_kernel, tq=tq, lam_init=lam_init), qa, ka, va,
            [diff_lambda[l], diff_subln[l].reshape(1, HEAD_DIM)], tq=tq,
            scratch=[pltpu.VMEM((8, tq, LANES), F32), pltpu.VMEM((8, tq, VEXT), F32)], name="attn_diff")
        o_b = _resident_attention(
            functools.partial(_attn_mla_kernel, tq=tq), qb, kb, vb, [], tq=tq,
            scratch=[pltpu.VMEM((4, tq, LANES), F32), pltpu.VMEM((4, tq, VEXT), F32)], name="attn_mla")
        o_c = _resident_attention(
            functools.partial(_attn_stick_kernel, tq=tq, tkc=tk_stick), qc, kc, vc, [tri], tq=tq,
            scratch=[pltpu.VMEM((4, tq, LANES), F32), pltpu.VMEM((4, tq, LANES), F32)], name="attn_stick")
        o_d = _attn_dsa(qd, kd, vd, iq, ik, iw, tq=tq)

        mixed = [o.reshape(n, GROUP_WIDTH) for o in (o_a, o_b, o_c, o_d)]
        xf = _matmul_residual(mixed, w_o[l].astype(BF16), xf, tm=tm, tn=512)
        act = _ffn_up(xf, ffn_norm[l].reshape(1, d), w_gate[l].astype(BF16), w_up[l].astype(BF16), tm=tm, tn=512)
        xf = _matmul_residual([act], w_down[l].astype(BF16), xf, tm=tm, tn=512)
    return xf.reshape(b, s, d)
```

```python
import functools
import math

import jax
import jax.numpy as jnp
from jax import lax
from jax.experimental import pallas as pl
from jax.experimental.pallas import tpu as pltpu

F32 = jnp.float32
BF16 = jnp.bfloat16

HEAD_DIM = 128
GROUP_HEADS = 4
GROUP_WIDTH = GROUP_HEADS * HEAD_DIM
ROPE_THETA = 500000.0
EPS = 1e-6
DIFF_SUB_DIM = 64
MLA_Q_RANK = 448
MLA_KV_RANK = 128
MLA_NOPE_DIM = 128
MLA_ROPE_DIM = 64
MLA_QK_DIM = MLA_NOPE_DIM + MLA_ROPE_DIM
MLA_PAD_DIM = 256
IDX_HEADS = 16
IDX_DIM = 64
TOPK_MAX = 256

LANES = 128
VEXT = 2 * LANES
Y_WIDTH = 13 * GROUP_WIDTH
NEG = -1e30
INT_MIN = -2147483648
LOG2E = 1.4426950408889634
COUNT_UNROLL = 2
COUNT_ACC_ROWS = 64
MAX_FIXED_SHIFT = 56.0
SCORE_BOUND_MARGIN = 1.02
STICK_DEAD_LOG2 = -160.0
VMEM_LIMIT = 56 * 1024 * 1024

_OFF = dict(a_q=0, a_k=512, a_v=1024, b_cq=1536, b_ckv=2048, b_kr=2176, d_ik=2304, d_iw=2432,
            c_q=2560, c_k=3072, c_v=3584, d_q=4096, d_k=4608, d_v=5120, d_iq=5632)


def _cparams(n_axes):
    return pltpu.CompilerParams(dimension_semantics=("arbitrary",) * n_axes, vmem_limit_bytes=VMEM_LIMIT)


def _dot(a, b):
    return jnp.dot(a, b, preferred_element_type=F32)


def _dot_nt(a, b):
    return lax.dot_general(a, b, (((1,), (1,)), ((), ())), preferred_element_type=F32)


def _rms_rows(x, g):
    ms = jnp.mean(x * x, axis=-1, keepdims=True)
    return x * lax.rsqrt(ms + EPS) * g


def _rms_matmul_kernel(x_ref, g_ref, w_ref, o_ref, h_ref):
    @pl.when(pl.program_id(1) == 0)
    def _():
        h_ref[...] = _rms_rows(x_ref[...], g_ref[...]).astype(BF16)

    o_ref[...] = _dot(h_ref[...], w_ref[...]).astype(o_ref.dtype)


def _rms_matmul(x, g, w, *, tm, tn, out_dtype):
    n, d = x.shape
    m = w.shape[1]
    return pl.pallas_call(
        _rms_matmul_kernel,
        grid=(n // tm, m // tn),
        in_specs=[pl.BlockSpec((tm, d), lambda i, j: (i, 0)),
                  pl.BlockSpec((1, d), lambda i, j: (0, 0)),
                  pl.BlockSpec((d, tn), lambda i, j: (0, j))],
        out_specs=pl.BlockSpec((tm, tn), lambda i, j: (i, j)),
        out_shape=jax.ShapeDtypeStruct((n, m), out_dtype),
        scratch_shapes=[pltpu.VMEM((tm, d), BF16)],
        compiler_params=_cparams(2),
        name="rms_matmul",
    )(x, g, w)


def _ffn_up_kernel(x_ref, g_ref, wg_ref, wu_ref, o_ref, h_ref):
    @pl.when(pl.program_id(1) == 0)
    def _():
        h_ref[...] = _rms_rows(x_ref[...], g_ref[...]).astype(BF16)

    h = h_ref[...]
    a = _dot(h, wg_ref[...])
    b = _dot(h, wu_ref[...])
    o_ref[...] = (jax.nn.silu(a) * b).astype(o_ref.dtype)


def _ffn_up(x, g, wg, wu, *, tm, tn):
    n, d = x.shape
    m = wg.shape[1]
    return pl.pallas_call(
        _ffn_up_kernel,
        grid=(n // tm, m // tn),
        in_specs=[pl.BlockSpec((tm, d), lambda i, j: (i, 0)),
                  pl.BlockSpec((1, d), lambda i, j: (0, 0)),
                  pl.BlockSpec((d, tn), lambda i, j: (0, j)),
                  pl.BlockSpec((d, tn), lambda i, j: (0, j))],
        out_specs=pl.BlockSpec((tm, tn), lambda i, j: (i, j)),
        out_shape=jax.ShapeDtypeStruct((n, m), BF16),
        scratch_shapes=[pltpu.VMEM((tm, d), BF16)],
        compiler_params=_cparams(2),
        name="ffn_up",
    )(x, g, wg, wu)


def _matmul_residual_kernel(*refs):
    a_refs, (w_ref, r_ref, o_ref) = refs[:-3], refs[-3:]
    acc = r_ref[...]
    row = 0
    for a_ref in a_refs:
        k = a_ref.shape[1]
        acc = acc + _dot(a_ref[...], w_ref[row:row + k, :])
        row += k
    o_ref[...] = acc


def _matmul_residual(parts, w, r, *, tm, tn):
    n = parts[0].shape[0]
    k, m = w.shape
    return pl.pallas_call(
        _matmul_residual_kernel,
        grid=(n // tm, m // tn),
        in_specs=[pl.BlockSpec((tm, a.shape[1]), lambda i, j: (i, 0)) for a in parts]
                 + [pl.BlockSpec((k, tn), lambda i, j: (0, j)),
                    pl.BlockSpec((tm, tn), lambda i, j: (i, j))],
        out_specs=pl.BlockSpec((tm, tn), lambda i, j: (i, j)),
        out_shape=jax.ShapeDtypeStruct((n, m), F32),
        compiler_params=_cparams(2),
        name="matmul_residual",
    )(*parts, w, r)


def _rope(x, t_ref, half):
    n = x.shape[1]
    return (x * t_ref[0] + pltpu.roll(x, n - half, 1) * t_ref[1] + pltpu.roll(x, half, 1) * t_ref[2])


def _rms_lane_groups(x, width, count):
    outs = []
    for h in range(x.shape[1] // width):
        xh = x[:, h * width:(h + 1) * width]
        ms = jnp.sum(xh * xh, axis=-1, keepdims=True) * (1.0 / count)
        outs.append(xh * lax.rsqrt(ms + EPS))
    return outs[0] if len(outs) == 1 else jnp.concatenate(outs, axis=1)


def _rms_sub64(x, ones_ref):
    x2 = x * x
    hi = x2.astype(BF16)
    lo = (x2 - hi.astype(F32)).astype(BF16)
    ms = (_dot(hi, ones_ref[...]) + _dot(lo, ones_ref[...])) * (1.0 / DIFF_SUB_DIM)
    return x * lax.rsqrt(ms + EPS)


def _store_vext(ref, v):
    ones = jnp.ones((v.shape[0], LANES), BF16)
    for h in range(GROUP_HEADS):
        ref[:, h * VEXT:h * VEXT + LANES] = v[:, h * LANES:(h + 1) * LANES].astype(BF16)
        ref[:, h * VEXT + LANES:(h + 1) * VEXT] = ones


def _prep_kernel(y_ref, ta_ref, tb_ref, td_ref, ones_ref, gaq_ref, gak_ref, gcq_ref, gckv_ref, gbq_ref, gbk_ref,
                 gdq_ref, gdk_ref, gik_ref, wq_ref, wkv_ref,
                 qa_ref, ka_ref, va_ref, qb_ref, kb_ref, vb_ref, qc_ref, kc_ref, vc_ref,
                 qd_ref, kd_ref, vd_ref, iq_ref, ik_ref, iw_ref):
    def sec(name, width):
        return y_ref[:, _OFF[name]:_OFF[name] + width]

    tm = y_ref.shape[0]
    lane = lax.broadcasted_iota(jnp.int32, (tm, LANES), 1)
    first_half = lane < DIFF_SUB_DIM

    qa = _rope(_rms_sub64(sec("a_q", 512), ones_ref) * gaq_ref[...], ta_ref, 8) * (DIFF_SUB_DIM ** -0.5 * LOG2E)
    for h in range(GROUP_HEADS):
        qh = qa[:, h * LANES:(h + 1) * LANES]
        qa_ref[:, (2 * h) * LANES:(2 * h + 1) * LANES] = jnp.where(first_half, qh, 0.0).astype(BF16)
        qa_ref[:, (2 * h + 1) * LANES:(2 * h + 2) * LANES] = jnp.where(first_half, 0.0, qh).astype(BF16)
    ka_ref[...] = _rope(_rms_sub64(sec("a_k", 512), ones_ref) * gak_ref[...], ta_ref, 8).astype(BF16)
    _store_vext(va_ref, sec("a_v", 512))

    cq = sec("b_cq", 512)
    cq = cq * lax.rsqrt(jnp.sum(cq * cq, axis=-1, keepdims=True) * (1.0 / MLA_Q_RANK) + EPS) * gcq_ref[...]
    qb = _dot(cq.astype(BF16), wq_ref[...])
    qb = _rms_lane_groups(qb, MLA_PAD_DIM, MLA_QK_DIM) * gbq_ref[...]
    ckv = _rms_lane_groups(sec("b_ckv", 128), 128, MLA_KV_RANK) * gckv_ref[...]
    kv = _dot(ckv.astype(BF16), wkv_ref[...])
    kr = sec("b_kr", 128)
    kr_ss = jnp.sum(kr * kr, axis=-1, keepdims=True)
    for h in range(GROUP_HEADS):
        lo, hi = h * MLA_PAD_DIM, (h + 1) * MLA_PAD_DIM
        qb_ref[:, lo:hi] = (_rope(qb[:, lo:hi], tb_ref, 32) * (MLA_QK_DIM ** -0.5 * LOG2E)).astype(BF16)
        kn = kv[:, h * LANES:(h + 1) * LANES]
        ms = (jnp.sum(kn * kn, axis=-1, keepdims=True) + kr_ss) * (1.0 / MLA_QK_DIM)
        kh = jnp.concatenate([kn, kr], axis=1) * lax.rsqrt(ms + EPS) * gbk_ref[...]
        kb_ref[:, lo:hi] = _rope(kh, tb_ref, 32).astype(BF16)
    _store_vext(vb_ref, kv[:, GROUP_WIDTH:])

    qc_ref[...] = (sec("c_q", 512) * (HEAD_DIM ** -0.5 * LOG2E)).astype(BF16)
    kc_ref[...] = sec("c_k", 512).astype(BF16)
    vc_ref[...] = sec("c_v", 512).astype(BF16)

    qd = _rope(_rms_lane_groups(sec("d_q", 512), HEAD_DIM, HEAD_DIM) * gdq_ref[...], td_ref, 16)
    qd_ref[...] = (qd * (HEAD_DIM ** -0.5 * LOG2E)).astype(BF16)
    kd_ref[...] = _rope(_rms_lane_groups(sec("d_k", 512), HEAD_DIM, HEAD_DIM) * gdk_ref[...], td_ref, 16).astype(BF16)
    _store_vext(vd_ref, sec("d_v", 512))
    for half in range(2):
        iq = y_ref[:, _OFF["d_iq"] + half * 512:_OFF["d_iq"] + (half + 1) * 512]
        iq_ref[:, half * 512:(half + 1) * 512] = (_rope(iq, ta_ref, 8) * (IDX_DIM ** -0.5)).astype(BF16)
    ik = _rms_lane_groups(sec("d_ik", 128), 128, IDX_DIM) * gik_ref[...]
    ik = (ik * ta_ref[0, :, :LANES] + pltpu.roll(ik, LANES - 8, 1) * ta_ref[1, :, :LANES]
          + pltpu.roll(ik, 8, 1) * ta_ref[2, :, :LANES])
    ik_ref[:, :LANES] = ik.astype(BF16)
    ik_ref[:, LANES:] = pltpu.roll(ik, IDX_DIM, 1).astype(BF16)
    iw_ref[...] = sec("d_iw", 128) * (IDX_HEADS ** -0.5)


def _prep(y, seq, tabs, ones64, gains, wq, wkv, *, tm):
    n = y.shape[0]
    nblk_seq = seq // tm
    ta, tb, td = tabs

    def rows(width):
        return pl.BlockSpec((tm, width), lambda i: (i, 0))

    def table(width):
        return pl.BlockSpec((3, tm, width), lambda i: (0, i % nblk_seq, 0))

    def whole(a):
        return pl.BlockSpec(a.shape, lambda i: (0,) * a.ndim)

    widths = [1024, 512, 1024, 1024, 1024, 1024, 512, 512, 512, 512, 512, 1024, 1024, 256]
    out_shape = [jax.ShapeDtypeStruct((n, w), BF16) for w in widths] + [jax.ShapeDtypeStruct((n, LANES), F32)]
    out_specs = [rows(w) for w in widths] + [rows(LANES)]
    return pl.pallas_call(
        _prep_kernel,
        grid=(n // tm,),
        in_specs=[rows(Y_WIDTH), table(512), table(256), table(512), whole(ones64)]
                 + [whole(g) for g in gains] + [whole(wq), whole(wkv)],
        out_specs=out_specs,
        out_shape=out_shape,
        compiler_params=_cparams(1),
        name="prep",
    )(y, ta, tb, td, ones64, *gains, wq, wkv)


def _softmax_chunk(s, vext, m_ref, acc_ref, idx, running_max):
    if not running_max:
        acc_ref[idx] += _dot(jnp.exp2(s).astype(BF16), vext)
        return
    m_prev = m_ref[idx]
    m_new = jnp.maximum(m_prev, jnp.max(s, axis=1, keepdims=True))
    p = jnp.exp2(s - jnp.tile(m_new, (1, s.shape[1] // LANES)))
    alpha = jnp.exp2(m_prev - m_new)
    acc_ref[idx] = jnp.tile(alpha, (1, VEXT // LANES)) * acc_ref[idx] + _dot(p.astype(BF16), vext)
    m_ref[idx] = m_new


def _init_softmax(m_ref, acc_ref):
    m_ref[...] = jnp.full(m_ref.shape, NEG, F32)
    acc_ref[...] = jnp.zeros(acc_ref.shape, F32)


def _with_score_bound(bound, attend):
    @pl.when(bound <= MAX_FIXED_SHIFT)
    def _():
        attend(bound)

    @pl.when(jnp.logical_not(bound <= MAX_FIXED_SHIFT))
    def _():
        attend(None)


def _normalised(acc_ref, idx):
    acc = acc_ref[idx]
    return acc[:, :LANES] / acc[:, LANES:]


def _causal_mask(row0, col0, tq, tkc, strict=False):
    rows = row0 + lax.broadcasted_iota(jnp.int32, (tq, tkc), 0)
    cols = col0 + lax.broadcasted_iota(jnp.int32, (tq, tkc), 1)
    return cols < rows if strict else cols <= rows


def _attn_diff_kernel(bound_ref, lam_ref, sub_ref, q_ref, k_ref, v_ref, o_ref, m_ref, acc_ref, *, tq, lam_init):
    qi = pl.program_id(1)
    _init_softmax(m_ref, acc_ref)

    def attend(shift):
        def chunk(off, mask):
            for mp in range(2 * GROUP_HEADS):
                h = mp // 2
                q = q_ref[0, :, mp * LANES:(mp + 1) * LANES]
                k = k_ref[0, pl.ds(off, tq), h * LANES:(h + 1) * LANES]
                v = v_ref[0, pl.ds(off, tq), h * VEXT:(h + 1) * VEXT]
                s = _dot_nt(q, k)
                if shift is not None:
                    s = s - shift
                if mask is not None:
                    s = jnp.where(mask, s, NEG)
                _softmax_chunk(s, v, m_ref, acc_ref, mp, running_max=shift is None)

        def full_chunk(c, carry):
            chunk(pl.multiple_of(c * tq, tq), None)
            return carry

        lax.fori_loop(0, qi, full_chunk, 0)
        chunk(pl.multiple_of(qi * tq, tq), _causal_mask(0, 0, tq, tq))

    _with_score_bound(bound_ref[0, 0], attend)

    lv = lam_ref[...]
    lam = (jnp.exp(jnp.sum(lv[0:1] * lv[1:2], axis=-1, keepdims=True))
           - jnp.exp(jnp.sum(lv[2:3] * lv[3:4], axis=-1, keepdims=True)) + lam_init)
    for h in range(GROUP_HEADS):
        o = _normalised(acc_ref, 2 * h) - lam * _normalised(acc_ref, 2 * h + 1)
        o = _rms_rows(o, sub_ref[...]) * (1.0 - lam_init)
        o_ref[0, :, h * LANES:(h + 1) * LANES] = o.astype(o_ref.dtype)


def _attn_mla_kernel(bound_ref, q_ref, k_ref, v_ref, o_ref, m_ref, acc_ref, *, tq):
    qi = pl.program_id(1)
    _init_softmax(m_ref, acc_ref)

    def attend(shift):
        def chunk(off, mask):
            for h in range(GROUP_HEADS):
                q = q_ref[0, :, h * MLA_PAD_DIM:(h + 1) * MLA_PAD_DIM]
                k = k_ref[0, pl.ds(off, tq), h * MLA_PAD_DIM:(h + 1) * MLA_PAD_DIM]
                v = v_ref[0, pl.ds(off, tq), h * VEXT:(h + 1) * VEXT]
                s = _dot_nt(q, k)
                if shift is not None:
                    s = s - shift
                if mask is not None:
                    s = jnp.where(mask, s, NEG)
                _softmax_chunk(s, v, m_ref, acc_ref, h, running_max=shift is None)

        def full_chunk(c, carry):
            chunk(pl.multiple_of(c * tq, tq), None)
            return carry

        lax.fori_loop(0, qi, full_chunk, 0)
        chunk(pl.multiple_of(qi * tq, tq), _causal_mask(0, 0, tq, tq))

    _with_score_bound(bound_ref[0, 0], attend)
    for h in range(GROUP_HEADS):
        o_ref[0, :, h * LANES:(h + 1) * LANES] = _normalised(acc_ref, h).astype(o_ref.dtype)


def _attn_stick_kernel(tri_ref, q_ref, k_ref, v_ref, o_ref, carry_ref, acc_ref, *, tq, tkc):
    qi = pl.program_id(1)
    carry_ref[...] = jnp.zeros(carry_ref.shape, F32)
    acc_ref[...] = jnp.zeros(acc_ref.shape, F32)

    def chunk(off, mask):
        tri = tri_ref[...]
        for h in range(GROUP_HEADS):
            q = q_ref[0, :, h * LANES:(h + 1) * LANES]
            k = k_ref[0, pl.ds(off, tkc), h * LANES:(h + 1) * LANES]
            v = v_ref[0, pl.ds(off, tkc), h * LANES:(h + 1) * LANES]
            z = _dot_nt(q, k)
            log_b = jnp.minimum(z, 0.0) - jnp.log2(1.0 + jnp.exp2(-jnp.abs(z)))
            log_1mb = log_b - z
            if mask is not None:
                log_1mb = jnp.where(mask, log_1mb, 0.0)
            hi = log_1mb.astype(BF16)
            lo = (log_1mb - hi.astype(F32)).astype(BF16)
            inner = _dot(hi, tri) + _dot(lo, tri)
            carry = carry_ref[h]
            a = jnp.exp2(log_b + inner + jnp.tile(carry, (1, tkc // LANES)))
            if mask is not None:
                a = jnp.where(mask, a, 0.0)
            acc_ref[h] += _dot(a.astype(BF16), v)
            carry_ref[h] = carry + (inner[:, 0:1] + log_1mb[:, 0:1])

    n_diag = tq // tkc
    for j in range(n_diag):
        col0 = (n_diag - 1 - j) * tkc
        chunk(pl.multiple_of(qi * tq + col0, tkc), _causal_mask(0, col0, tq, tkc, strict=True))

    n_full = qi * n_diag

    def full_chunk(state):
        c, _ = state
        chunk(pl.multiple_of((n_full - 1 - c) * tkc, tkc), None)
        return c + 1, (jnp.max(carry_ref[...]) > STICK_DEAD_LOG2).astype(jnp.int32)

    lax.while_loop(lambda state: jnp.logical_and(state[0] < n_full, state[1] > 0), full_chunk,
                   (jnp.int32(0), jnp.int32(1)))
    for h in range(GROUP_HEADS):
        o_ref[0, :, h * LANES:(h + 1) * LANES] = acc_ref[h].astype(o_ref.dtype)


def _resident_attention(kernel, q, k, v, extra, *, tq, scratch, name, score_bound=None):
    b, s, _ = q.shape
    q_map = lambda bi, qi: (bi, qi, 0)
    all_map = lambda bi, qi: (bi, 0, 0)
    extra_specs = [pl.BlockSpec(e.shape, lambda bi, qi, nd=e.ndim: (0,) * nd) for e in extra]
    if score_bound is not None:
        extra = [score_bound] + list(extra)
        extra_specs = [pl.BlockSpec(memory_space=pltpu.SMEM)] + extra_specs
    return pl.pallas_call(
        kernel,
        grid=(b, s // tq),
        in_specs=extra_specs + [pl.BlockSpec((1, tq, q.shape[2]), q_map),
                                pl.BlockSpec((1, s, k.shape[2]), all_map),
                                pl.BlockSpec((1, s, v.shape[2]), all_map)],
        out_specs=pl.BlockSpec((1, tq, GROUP_WIDTH), q_map),
        out_shape=jax.ShapeDtypeStruct((b, s, GROUP_WIDTH), BF16),
        scratch_shapes=scratch,
        compiler_params=_cparams(2),
        name=name,
    )(*extra, q, k, v)


def _attn_dsa_kernel(bound_ref, q_ref, k_ref, v_ref, iq_ref, ik_ref, iw_ref, earlier_ref, o_ref, key_ref, half_ref,
                     m_ref, acc_ref, *, tq, topk):
    qi = pl.program_id(1)
    nkc = qi + 1
    iw_t = iw_ref[0].T

    def score_chunk(c, mask):
        off = pl.multiple_of(c * tq, tq)
        ikc = ik_ref[0, pl.ds(off, tq), :]
        isc = jnp.zeros((tq, tq), F32)
        for hp in range(IDX_HEADS // 2):
            iqp = iq_ref[0, :, hp * LANES:(hp + 1) * LANES]
            for e in range(2):
                hh = 2 * hp + e
                s = _dot_nt(ikc[:, e * LANES:(e + 1) * LANES], iqp)
                isc = isc + jnp.maximum(s, 0.0) * iw_t[hh:hh + 1, :]
        bits = lax.bitcast_convert_type(isc, jnp.int32)
        key = jnp.where(bits < 0, bits ^ jnp.int32(0x7FFFFFFF), bits)
        if mask is not None:
            key = jnp.where(mask, key, INT_MIN)
        key_ref[c] = key
        half_ref[c] = (key >> 16).astype(jnp.int16)

    def full_score_chunk(c, carry):
        score_chunk(c, None)
        return carry

    lax.fori_loop(0, qi, full_score_chunk, 0)
    key_pos = lax.broadcasted_iota(jnp.int32, (tq, tq), 0)
    query_pos = lax.broadcasted_iota(jnp.int32, (tq, tq), 1)
    score_chunk(qi, key_pos <= query_pos)

    one16, zero16, lowest16 = jnp.int16(1), jnp.int16(0), jnp.int16(-32768)

    @pl.when(nkc % COUNT_UNROLL == 1)
    def _():
        half_ref[nkc] = jnp.full((tq, tq), lowest16, jnp.int16)

    def row_count(bound, strict):
        def body(t, part):
            for u in range(COUNT_UNROLL):
                for r in range(tq // COUNT_ACC_ROWS):
                    half = half_ref[COUNT_UNROLL * t + u, r * COUNT_ACC_ROWS:(r + 1) * COUNT_ACC_ROWS, :]
                    part = part + jnp.where(half > bound if strict else half >= bound, one16, zero16)
            return part

        part = lax.fori_loop(0, (nkc + COUNT_UNROLL - 1) // COUNT_UNROLL, body,
                             jnp.zeros((COUNT_ACC_ROWS, tq), jnp.int16))
        return jnp.sum(part.astype(F32), axis=0, keepdims=True)

    def search16(rank, cnt_init):
        def bit(i, carry):
            thr_u, cnt_thr = carry
            cand_u = thr_u | lax.shift_left(jnp.int32(1), 15 - i)
            cnt = row_count((cand_u - 32768).astype(jnp.int16), strict=False)
            take = cnt >= rank
            return jnp.where(take, cand_u, thr_u), jnp.where(take, cnt, cnt_thr)

        return lax.fori_loop(0, 16, bit, (jnp.zeros((1, tq), jnp.int32), cnt_init))

    k_f = jnp.full((1, tq), float(topk), F32)
    hi_u, cnt_hi = search16(k_f, jnp.zeros((1, tq), F32))
    hi_thr = (hi_u - 32768).astype(jnp.int16)
    cnt_gt = row_count(hi_thr, strict=True)

    def low_halves(c, carry):
        lo = ((key_ref[c] & 0xFFFF) - 32768).astype(jnp.int16)
        half_ref[c] = jnp.where(half_ref[c] == hi_thr, lo, lowest16)
        return carry

    lax.fori_loop(0, nkc, low_halves, 0)
    lo_u, cnt_lo = search16(k_f - cnt_gt, cnt_hi - cnt_gt)
    thr_raw = ((hi_u << 16) | lo_u) ^ jnp.int32(INT_MIN)
    short_row = thr_raw == jnp.int32(INT_MIN)
    thr = jnp.maximum(thr_raw, jnp.int32(INT_MIN + 1))
    cnt_ge = cnt_gt + cnt_lo
    tied = jnp.logical_and(jnp.logical_not(short_row), cnt_ge > k_f)
    any_tied = jnp.max(jnp.where(tied, 1.0, 0.0)) > 0.0

    _init_softmax(m_ref, acc_ref)

    def attend(c, selected, shift):
        bias = jnp.where(selected, 0.0 if shift is None else -shift, NEG).T
        off = pl.multiple_of(c * tq, tq)
        for h in range(GROUP_HEADS):
            q = q_ref[0, :, h * LANES:(h + 1) * LANES]
            k = k_ref[0, pl.ds(off, tq), h * LANES:(h + 1) * LANES]
            v = v_ref[0, pl.ds(off, tq), h * VEXT:(h + 1) * VEXT]
            _softmax_chunk(_dot_nt(q, k) + bias, v, m_ref, acc_ref, h, running_max=shift is None)

    def attend_untied(shift):
        def attend_chunk(c, carry):
            attend(c, key_ref[c] >= thr, shift)
            return carry

        lax.fori_loop(0, nkc, attend_chunk, 0)

    @pl.when(jnp.logical_not(any_tied))
    def _():
        _with_score_bound(bound_ref[0, 0], attend_untied)

    @pl.when(any_tied)
    def _():
        def eq_chunk(c, n):
            return n + jnp.sum(jnp.where(key_ref[c] == thr, 1.0, 0.0), axis=0, keepdims=True)

        n_eq = lax.fori_loop(0, nkc, eq_chunk, jnp.zeros((1, tq), F32))
        keep = jnp.where(short_row, 0.0, k_f - (cnt_ge - n_eq))

        def attend_chunk(c, seen):
            key = key_ref[c]
            eq = key == thr
            eq_f = jnp.where(eq, 1.0, 0.0)
            rank = seen + _dot(earlier_ref[...], eq_f.astype(BF16))
            attend(c, jnp.logical_or(key > thr, jnp.logical_and(eq, rank < keep)), None)
            return seen + jnp.sum(eq_f, axis=0, keepdims=True)

        lax.fori_loop(0, nkc, attend_chunk, jnp.zeros((1, tq), F32))

    for h in range(GROUP_HEADS):
        o_ref[0, :, h * LANES:(h + 1) * LANES] = _normalised(acc_ref, h).astype(o_ref.dtype)


def _attn_dsa(q, k, v, iq, ik, iw, score_bound, *, tq):
    b, s, _ = q.shape
    topk = min(TOPK_MAX, s // 4)
    earlier = (jnp.arange(tq)[:, None] > jnp.arange(tq)[None, :]).astype(BF16)
    q_map = lambda bi, qi: (bi, qi, 0)
    all_map = lambda bi, qi: (bi, 0, 0)
    return pl.pallas_call(
        functools.partial(_attn_dsa_kernel, tq=tq, topk=topk),
        grid=(b, s // tq),
        in_specs=[pl.BlockSpec(memory_space=pltpu.SMEM),
                  pl.BlockSpec((1, tq, GROUP_WIDTH), q_map),
                  pl.BlockSpec((1, s, GROUP_WIDTH), all_map),
                  pl.BlockSpec((1, s, GROUP_HEADS * VEXT), all_map),
                  pl.BlockSpec((1, tq, IDX_HEADS * IDX_DIM), q_map),
                  pl.BlockSpec((1, s, 2 * LANES), all_map),
                  pl.BlockSpec((1, tq, LANES), q_map),
                  pl.BlockSpec((tq, tq), lambda bi, qi: (0, 0))],
        out_specs=pl.BlockSpec((1, tq, GROUP_WIDTH), q_map),
        out_shape=jax.ShapeDtypeStruct((b, s, GROUP_WIDTH), BF16),
        scratch_shapes=[pltpu.VMEM((s // tq, tq, tq), jnp.int32),
                        pltpu.VMEM((-(-(s // tq) // COUNT_UNROLL) * COUNT_UNROLL, tq, tq), jnp.int16),
                        pltpu.VMEM((GROUP_HEADS, tq, LANES), F32),
                        pltpu.VMEM((GROUP_HEADS, tq, VEXT), F32)],
        compiler_params=_cparams(2),
        name="attn_dsa",
    )(score_bound, q, k, v, iq, ik, iw, earlier)


def _rope_tables(seq, ncols, group, start, rot):
    half = rot // 2
    inv = ROPE_THETA ** (-jnp.arange(half, dtype=F32) * 2.0 / rot)
    j = jnp.arange(ncols) % group - start
    in_lo = (j >= 0) & (j < half)
    in_hi = (j >= half) & (j < rot)
    idx = jnp.clip(jnp.where(in_hi, j - half, j), 0, half - 1)
    ang = jnp.arange(seq, dtype=jnp.int32).astype(F32)[:, None] * inv[idx][None, :]
    cos, sin = jnp.cos(ang), jnp.sin(ang)
    return jnp.stack([jnp.where(in_lo | in_hi, cos, 1.0), jnp.where(in_lo, -sin, 0.0), jnp.where(in_hi, sin, 0.0)])


def _pad_cols(a, width):
    return jnp.pad(a, ((0, 0), (0, width - a.shape[1])))


def _relayout_w_in(w):
    sizes = (512,) * 3 + (MLA_Q_RANK, MLA_KV_RANK, MLA_ROPE_DIM) + (512,) * 6 + (IDX_HEADS * IDX_DIM, IDX_DIM, IDX_HEADS)
    names = ("a_q", "a_k", "a_v", "b_cq", "b_ckv", "b_kr", "c_q", "c_k", "c_v", "d_q", "d_k", "d_v", "d_iq", "d_ik", "d_iw")
    parts, start = {}, 0
    for nme, sz in zip(names, sizes):
        parts[nme] = w[:, start:start + sz]
        start += sz
    order = sorted(_OFF, key=_OFF.get)
    ends = [_OFF[nme] for nme in order[1:]] + [Y_WIDTH]
    return jnp.concatenate([_pad_cols(parts[nme], end - _OFF[nme]) for nme, end in zip(order, ends)], axis=1).astype(BF16)


def _relayout_mla_q(w):
    w = w.reshape(w.shape[0], GROUP_HEADS, MLA_QK_DIM)
    w = jnp.pad(w, ((0, 0), (0, 0), (0, MLA_PAD_DIM - MLA_QK_DIM)))
    return w.reshape(w.shape[0], GROUP_HEADS * MLA_PAD_DIM)


def _score_bound(dim, gain_q, gain_k):
    bound = (dim ** 0.5 * LOG2E * SCORE_BOUND_MARGIN) * jnp.max(jnp.abs(gain_q)) * jnp.max(jnp.abs(gain_k))
    return bound.astype(F32).reshape(1, 1)


def _tile_row(g, reps):
    return jnp.tile(g.reshape(1, -1), (1, reps))


def kernel(x, attn_norm, w_in, diff_qk_norm, diff_lambda, diff_subln, mla_q_a_norm, mla_wq_b, mla_kv_a_norm,
           mla_wkv_b, mla_qk_norm, dsa_qk_norm, idx_k_norm, w_o, ffn_norm, w_gate, w_up, w_down):
    b, s, d = x.shape
    n = b * s
    depth = w_in.shape[0]
    tm = min(1024, n)
    tq = min(512, s)
    tabs = (_rope_tables(s, 512, DIFF_SUB_DIM, 0, DIFF_SUB_DIM // 4),
            _rope_tables(s, MLA_PAD_DIM, MLA_PAD_DIM, MLA_NOPE_DIM, MLA_ROPE_DIM),
            _rope_tables(s, 512, HEAD_DIM, 0, HEAD_DIM // 4))
    ones64 = jnp.kron(jnp.eye(512 // DIFF_SUB_DIM, dtype=F32), jnp.ones((DIFF_SUB_DIM, DIFF_SUB_DIM), F32)).astype(BF16)
    tk_stick = min(256, s)
    tri = (jnp.arange(tk_stick)[:, None] > jnp.arange(tk_stick)[None, :]).astype(BF16)

    xf = x.reshape(n, d)
    for l in range(depth):
        lam_init = 0.8 - 0.6 * math.exp(-0.3 * l)
        y = _rms_matmul(xf, attn_norm[l].reshape(1, d), _relayout_w_in(w_in[l]), tm=tm, tn=512, out_dtype=F32)

        wq = jnp.pad(_relayout_mla_q(mla_wq_b[l]), ((0, 512 - MLA_Q_RANK), (0, 0))).astype(BF16)
        wkv = mla_wkv_b[l].reshape(MLA_KV_RANK, GROUP_HEADS, 2, HEAD_DIM).transpose(0, 2, 1, 3)
        wkv = wkv.reshape(MLA_KV_RANK, 2 * GROUP_WIDTH).astype(BF16)
        gains = [
            _tile_row(diff_qk_norm[l, 0], 8), _tile_row(diff_qk_norm[l, 1], 8),
            _pad_cols(mla_q_a_norm[l].reshape(1, -1), 512), mla_kv_a_norm[l].reshape(1, -1),
            _relayout_mla_q(mla_qk_norm[l, 0].reshape(1, MLA_QK_DIM).repeat(GROUP_HEADS, 0).reshape(1, -1)),
            _pad_cols(mla_qk_norm[l, 1].reshape(1, -1), MLA_PAD_DIM),
            _tile_row(dsa_qk_norm[l, 0], 4), _tile_row(dsa_qk_norm[l, 1], 4),
            _pad_cols(idx_k_norm[l].reshape(1, -1), LANES),
        ]
        (qa, ka, va, qb, kb, vb, qc, kc, vc, qd, kd, vd, iq, ik, iw) = [
            a.reshape(b, s, a.shape[1]) for a in _prep(y, s, tabs, ones64, gains, wq, wkv, tm=min(256, s))]

        o_a = _resident_attention(
            functools.partial(_attn_diff_kernel, tq=tq, lam_init=lam_init), qa, ka, va,
            [diff_lambda[l], diff_subln[l].reshape(1, HEAD_DIM)], tq=tq,
            scratch=[pltpu.VMEM((8, tq, LANES), F32), pltpu.VMEM((8, tq, VEXT), F32)], name="attn_diff",
            score_bound=_score_bound(DIFF_SUB_DIM, diff_qk_norm[l, 0], diff_qk_norm[l, 1]))
        o_b = _resident_attention(
            functools.partial(_attn_mla_kernel, tq=tq), qb, kb, vb, [], tq=tq,
            scratch=[pltpu.VMEM((4, tq, LANES), F32), pltpu.VMEM((4, tq, VEXT), F32)], name="attn_mla",
            score_bound=_score_bound(MLA_QK_DIM, mla_qk_norm[l, 0], mla_qk_norm[l, 1]))
        o_c = _resident_attention(
            functools.partial(_attn_stick_kernel, tq=tq, tkc=tk_stick), qc, kc, vc, [tri], tq=tq,
            scratch=[pltpu.VMEM((4, tq, LANES), F32), pltpu.VMEM((4, tq, LANES), F32)], name="attn_stick")
        o_d = _attn_dsa(qd, kd, vd, iq, ik, iw, _score_bound(HEAD_DIM, dsa_qk_norm[l, 0], dsa_qk_norm[l, 1]), tq=tq)

        mixed = [o.reshape(n, GROUP_WIDTH) for o in (o_a, o_b, o_c, o_d)]
        xf = _matmul_residual(mixed, w_o[l].astype(BF16), xf, tm=tm, tn=512)
        act = _ffn_up(xf, ffn_norm[l].reshape(1, d), w_gate[l].astype(BF16), w_up[l].astype(BF16), tm=tm, tn=512)
        xf = _matmul_residual([act], w_down[l].astype(BF16), xf, tm=tm, tn=512)
    return xf.reshape(b, s, d)
```

```python
import functools
import math

import jax
import jax.numpy as jnp
from jax import lax
from jax.experimental import pallas as pl
from jax.experimental.pallas import tpu as pltpu

F32 = jnp.float32
BF16 = jnp.bfloat16

HEAD_DIM = 128
GROUP_HEADS = 4
GROUP_WIDTH = GROUP_HEADS * HEAD_DIM
ROPE_THETA = 500000.0
EPS = 1e-6
DIFF_SUB_DIM = 64
MLA_Q_RANK = 448
MLA_KV_RANK = 128
MLA_NOPE_DIM = 128
MLA_ROPE_DIM = 64
MLA_QK_DIM = MLA_NOPE_DIM + MLA_ROPE_DIM
MLA_PAD_DIM = 256
IDX_HEADS = 16
IDX_DIM = 64
TOPK_MAX = 256

LANES = 128
VEXT = 2 * LANES
Y_WIDTH = 13 * GROUP_WIDTH
NEG = -1e30
INT_MIN = -2147483648
LOG2E = 1.4426950408889634
COUNT_UNROLL = 2
COUNT_ACC_ROWS = 64
MAX_FIXED_SHIFT = 56.0
SCORE_BOUND_MARGIN = 1.02
STICK_DEAD_LOG2 = -160.0
VMEM_LIMIT = 56 * 1024 * 1024

_OFF = dict(a_q=0, a_k=512, a_v=1024, b_cq=1536, b_ckv=2048, b_kr=2176, d_ik=2304, d_iw=2432,
            c_q=2560, c_k=3072, c_v=3584, d_q=4096, d_k=4608, d_v=5120, d_iq=5632)


def _cparams(n_axes):
    return pltpu.CompilerParams(dimension_semantics=("arbitrary",) * n_axes, vmem_limit_bytes=VMEM_LIMIT)


def _dot(a, b):
    return jnp.dot(a, b, preferred_element_type=F32)


def _dot_nt(a, b):
    return lax.dot_general(a, b, (((1,), (1,)), ((), ())), preferred_element_type=F32)


def _rms_rows(x, g):
    ms = jnp.mean(x * x, axis=-1, keepdims=True)
    return x * lax.rsqrt(ms + EPS) * g


def _rms_matmul_kernel(x_ref, g_ref, w_ref, o_ref, h_ref):
    @pl.when(pl.program_id(1) == 0)
    def _():
        h_ref[...] = _rms_rows(x_ref[...], g_ref[...]).astype(BF16)

    o_ref[...] = _dot(h_ref[...], w_ref[...]).astype(o_ref.dtype)


def _rms_matmul(x, g, w, *, tm, tn, out_dtype):
    n, d = x.shape
    m = w.shape[1]
    return pl.pallas_call(
        _rms_matmul_kernel,
        grid=(n // tm, m // tn),
        in_specs=[pl.BlockSpec((tm, d), lambda i, j: (i, 0)),
                  pl.BlockSpec((1, d), lambda i, j: (0, 0)),
                  pl.BlockSpec((d, tn), lambda i, j: (0, j))],
        out_specs=pl.BlockSpec((tm, tn), lambda i, j: (i, j)),
        out_shape=jax.ShapeDtypeStruct((n, m), out_dtype),
        scratch_shapes=[pltpu.VMEM((tm, d), BF16)],
        compiler_params=_cparams(2),
        name="rms_matmul",
    )(x, g, w)


def _ffn_up_kernel(x_ref, g_ref, wg_ref, wu_ref, o_ref, h_ref):
    @pl.when(pl.program_id(1) == 0)
    def _():
        h_ref[...] = _rms_rows(x_ref[...], g_ref[...]).astype(BF16)

    h = h_ref[...]
    a = _dot(h, wg_ref[...].astype(BF16))
    b = _dot(h, wu_ref[...].astype(BF16))
    o_ref[...] = (jax.nn.silu(a) * b).astype(o_ref.dtype)


def _ffn_up(x, g, wg, wu, *, tm, tn):
    n, d = x.shape
    m = wg.shape[1]
    return pl.pallas_call(
        _ffn_up_kernel,
        grid=(n // tm, m // tn),
        in_specs=[pl.BlockSpec((tm, d), lambda i, j: (i, 0)),
                  pl.BlockSpec((1, d), lambda i, j: (0, 0)),
                  pl.BlockSpec((d, tn), lambda i, j: (0, j)),
                  pl.BlockSpec((d, tn), lambda i, j: (0, j))],
        out_specs=pl.BlockSpec((tm, tn), lambda i, j: (i, j)),
        out_shape=jax.ShapeDtypeStruct((n, m), BF16),
        scratch_shapes=[pltpu.VMEM((tm, d), BF16)],
        compiler_params=_cparams(2),
        name="ffn_up",
    )(x, g, wg, wu)


def _matmul_residual_kernel(*refs):
    a_refs, (w_ref, r_ref, o_ref) = refs[:-3], refs[-3:]
    acc = r_ref[...]
    row = 0
    for a_ref in a_refs:
        k = a_ref.shape[1]
        acc = acc + _dot(a_ref[...], w_ref[row:row + k, :])
        row += k
    o_ref[...] = acc


def _matmul_residual(parts, w, r, *, tm, tn):
    n = parts[0].shape[0]
    k, m = w.shape
    return pl.pallas_call(
        _matmul_residual_kernel,
        grid=(n // tm, m // tn),
        in_specs=[pl.BlockSpec((tm, a.shape[1]), lambda i, j: (i, 0)) for a in parts]
                 + [pl.BlockSpec((k, tn), lambda i, j: (0, j)),
                    pl.BlockSpec((tm, tn), lambda i, j: (i, j))],
        out_specs=pl.BlockSpec((tm, tn), lambda i, j: (i, j)),
        out_shape=jax.ShapeDtypeStruct((n, m), F32),
        compiler_params=_cparams(2),
        name="matmul_residual",
    )(*parts, w, r)


def _rope(x, t_ref, blocks=None):
    outs = []
    for b in range(x.shape[1] // LANES):
        xb = x[:, b * LANES:(b + 1) * LANES]
        if blocks is None or b in blocks:
            xb = xb * t_ref[0] + pltpu.roll(xb, LANES // 2, 1) * t_ref[1]
        outs.append(xb)
    return outs[0] if len(outs) == 1 else jnp.concatenate(outs, axis=1)


def _rms_lane_groups(x, width, count):
    outs = []
    for h in range(x.shape[1] // width):
        xh = x[:, h * width:(h + 1) * width]
        ms = jnp.sum(xh * xh, axis=-1, keepdims=True) * (1.0 / count)
        outs.append(xh * lax.rsqrt(ms + EPS))
    return outs[0] if len(outs) == 1 else jnp.concatenate(outs, axis=1)


def _rms_sub64(x, ones_ref):
    x2 = x * x
    hi = x2.astype(BF16)
    lo = (x2 - hi.astype(F32)).astype(BF16)
    ms = (_dot(hi, ones_ref[...]) + _dot(lo, ones_ref[...])) * (1.0 / DIFF_SUB_DIM)
    return x * lax.rsqrt(ms + EPS)


def _store_vext(ref, v):
    ones = jnp.ones((v.shape[0], LANES), BF16)
    for h in range(GROUP_HEADS):
        ref[:, h * VEXT:h * VEXT + LANES] = v[:, h * LANES:(h + 1) * LANES].astype(BF16)
        ref[:, h * VEXT + LANES:(h + 1) * VEXT] = ones


def _prep_kernel(y_ref, ta_ref, tb_ref, td_ref, ones_ref, gaq_ref, gak_ref, gcq_ref, gckv_ref, gbq_ref, gbk_ref,
                 gdq_ref, gdk_ref, gik_ref, wq_ref, wkv_ref,
                 qa_ref, ka_ref, va_ref, qb_ref, kb_ref, vb_ref, qc_ref, kc_ref, vc_ref,
                 qd_ref, kd_ref, vd_ref, iq_ref, ik_ref, iw_ref):
    def sec(name, width):
        return y_ref[:, _OFF[name]:_OFF[name] + width]

    tm = y_ref.shape[0]
    lane = lax.broadcasted_iota(jnp.int32, (tm, LANES), 1)
    first_half = (lane % (LANES // 2)) < DIFF_SUB_DIM // 2

    qa = _rope(_rms_sub64(sec("a_q", 512), ones_ref) * gaq_ref[...], ta_ref) * (DIFF_SUB_DIM ** -0.5 * LOG2E)
    for h in range(GROUP_HEADS):
        qh = qa[:, h * LANES:(h + 1) * LANES]
        qa_ref[:, (2 * h) * LANES:(2 * h + 1) * LANES] = jnp.where(first_half, qh, 0.0).astype(BF16)
        qa_ref[:, (2 * h + 1) * LANES:(2 * h + 2) * LANES] = jnp.where(first_half, 0.0, qh).astype(BF16)
    ka_ref[...] = _rope(_rms_sub64(sec("a_k", 512), ones_ref) * gak_ref[...], ta_ref).astype(BF16)
    _store_vext(va_ref, sec("a_v", 512))

    cq = sec("b_cq", 512)
    cq = cq * lax.rsqrt(jnp.sum(cq * cq, axis=-1, keepdims=True) * (1.0 / MLA_Q_RANK) + EPS) * gcq_ref[...]
    qb = _dot(cq.astype(BF16), wq_ref[...])
    qb = _rms_lane_groups(qb, MLA_PAD_DIM, MLA_QK_DIM) * gbq_ref[...]
    ckv = _rms_lane_groups(sec("b_ckv", 128), 128, MLA_KV_RANK) * gckv_ref[...]
    kv = _dot(ckv.astype(BF16), wkv_ref[...])
    kr = sec("b_kr", 128)
    kr_ss = jnp.sum(kr * kr, axis=-1, keepdims=True)
    for h in range(GROUP_HEADS):
        lo, hi = h * MLA_PAD_DIM, (h + 1) * MLA_PAD_DIM
        qb_ref[:, lo:hi] = (_rope(qb[:, lo:hi], tb_ref, blocks=(1,)) * (MLA_QK_DIM ** -0.5 * LOG2E)).astype(BF16)
        kn = kv[:, h * LANES:(h + 1) * LANES]
        ms = (jnp.sum(kn * kn, axis=-1, keepdims=True) + kr_ss) * (1.0 / MLA_QK_DIM)
        kh = jnp.concatenate([kn, kr], axis=1) * lax.rsqrt(ms + EPS) * gbk_ref[...]
        kb_ref[:, lo:hi] = _rope(kh, tb_ref, blocks=(1,)).astype(BF16)
    _store_vext(vb_ref, kv[:, GROUP_WIDTH:])

    qc_ref[...] = (sec("c_q", 512) * (HEAD_DIM ** -0.5 * LOG2E)).astype(BF16)
    kc_ref[...] = sec("c_k", 512).astype(BF16)
    vc_ref[...] = sec("c_v", 512).astype(BF16)

    qd = _rope(_rms_lane_groups(sec("d_q", 512), HEAD_DIM, HEAD_DIM) * gdq_ref[...], td_ref)
    qd_ref[...] = (qd * (HEAD_DIM ** -0.5 * LOG2E)).astype(BF16)
    kd_ref[...] = _rope(_rms_lane_groups(sec("d_k", 512), HEAD_DIM, HEAD_DIM) * gdk_ref[...], td_ref).astype(BF16)
    _store_vext(vd_ref, sec("d_v", 512))
    for half in range(2):
        iq = y_ref[:, _OFF["d_iq"] + half * 512:_OFF["d_iq"] + (half + 1) * 512]
        iq_ref[:, half * 512:(half + 1) * 512] = (_rope(iq, ta_ref) * (IDX_DIM ** -0.5)).astype(BF16)
    ik = _rope(_rms_lane_groups(sec("d_ik", 128), 128, IDX_DIM) * gik_ref[...], ta_ref)
    ik_ref[:, :LANES] = ik.astype(BF16)
    ik_ref[:, LANES:] = pltpu.roll(ik, IDX_DIM // 2, 1).astype(BF16)
    iw_ref[...] = sec("d_iw", 128) * (IDX_HEADS ** -0.5)


def _prep(y, seq, tabs, ones64, gains, wq, wkv, *, tm):
    n = y.shape[0]
    nblk_seq = seq // tm
    ta, tb, td = tabs

    def rows(width):
        return pl.BlockSpec((tm, width), lambda i: (i, 0))

    def table():
        return pl.BlockSpec((2, tm, LANES), lambda i: (0, i % nblk_seq, 0))

    def whole(a):
        return pl.BlockSpec(a.shape, lambda i: (0,) * a.ndim)

    widths = [1024, 512, 1024, 1024, 1024, 1024, 512, 512, 512, 512, 512, 1024, 1024, 256]
    out_shape = [jax.ShapeDtypeStruct((n, w), BF16) for w in widths] + [jax.ShapeDtypeStruct((n, LANES), F32)]
    out_specs = [rows(w) for w in widths] + [rows(LANES)]
    return pl.pallas_call(
        _prep_kernel,
        grid=(n // tm,),
        in_specs=[rows(Y_WIDTH), table(), table(), table(), whole(ones64)]
                 + [whole(g) for g in gains] + [whole(wq), whole(wkv)],
        out_specs=out_specs,
        out_shape=out_shape,
        compiler_params=_cparams(1),
        name="prep",
    )(y, ta, tb, td, ones64, *gains, wq, wkv)


def _softmax_chunk(s, vext, m_ref, acc_ref, idx, running_max):
    if not running_max:
        acc_ref[idx] += _dot(jnp.exp2(s).astype(BF16), vext)
        return
    m_prev = m_ref[idx]
    m_new = jnp.maximum(m_prev, jnp.max(s, axis=1, keepdims=True))
    p = jnp.exp2(s - jnp.tile(m_new, (1, s.shape[1] // LANES)))
    alpha = jnp.exp2(m_prev - m_new)
    acc_ref[idx] = jnp.tile(alpha, (1, VEXT // LANES)) * acc_ref[idx] + _dot(p.astype(BF16), vext)
    m_ref[idx] = m_new


def _init_softmax(m_ref, acc_ref):
    m_ref[...] = jnp.full(m_ref.shape, NEG, F32)
    acc_ref[...] = jnp.zeros(acc_ref.shape, F32)


def _with_score_bound(bound, attend):
    @pl.when(bound <= MAX_FIXED_SHIFT)
    def _():
        attend(bound)

    @pl.when(jnp.logical_not(bound <= MAX_FIXED_SHIFT))
    def _():
        attend(None)


def _normalised(acc_ref, idx):
    acc = acc_ref[idx]
    return acc[:, :LANES] / acc[:, LANES:]


def _causal_mask(row0, col0, tq, tkc, strict=False):
    rows = row0 + lax.broadcasted_iota(jnp.int32, (tq, tkc), 0)
    cols = col0 + lax.broadcasted_iota(jnp.int32, (tq, tkc), 1)
    return cols < rows if strict else cols <= rows


def _attn_diff_kernel(bound_ref, lam_ref, sub_ref, q_ref, k_ref, v_ref, o_ref, m_ref, acc_ref, *, tq, lam_init):
    qi = pl.program_id(1)
    _init_softmax(m_ref, acc_ref)

    def attend(shift):
        def chunk(off, mask):
            for mp in range(2 * GROUP_HEADS):
                h = mp // 2
                q = q_ref[0, :, mp * LANES:(mp + 1) * LANES]
                k = k_ref[0, pl.ds(off, tq), h * LANES:(h + 1) * LANES]
                v = v_ref[0, pl.ds(off, tq), h * VEXT:(h + 1) * VEXT]
                s = _dot_nt(q, k)
                if shift is not None:
                    s = s - shift
                if mask is not None:
                    s = jnp.where(mask, s, NEG)
                _softmax_chunk(s, v, m_ref, acc_ref, mp, running_max=shift is None)

        def full_chunk(c, carry):
            chunk(pl.multiple_of(c * tq, tq), None)
            return carry

        lax.fori_loop(0, qi, full_chunk, 0)
        chunk(pl.multiple_of(qi * tq, tq), _causal_mask(0, 0, tq, tq))

    _with_score_bound(bound_ref[0, 0], attend)

    lv = lam_ref[...]
    lam = (jnp.exp(jnp.sum(lv[0:1] * lv[1:2], axis=-1, keepdims=True))
           - jnp.exp(jnp.sum(lv[2:3] * lv[3:4], axis=-1, keepdims=True)) + lam_init)
    for h in range(GROUP_HEADS):
        o = _normalised(acc_ref, 2 * h) - lam * _normalised(acc_ref, 2 * h + 1)
        o = _rms_rows(o, sub_ref[...]) * (1.0 - lam_init)
        o_ref[0, :, h * LANES:(h + 1) * LANES] = o.astype(o_ref.dtype)


def _attn_mla_kernel(bound_ref, q_ref, k_ref, v_ref, o_ref, m_ref, acc_ref, *, tq):
    qi = pl.program_id(1)
    _init_softmax(m_ref, acc_ref)

    def attend(shift):
        def chunk(off, mask):
            for h in range(GROUP_HEADS):
                q = q_ref[0, :, h * MLA_PAD_DIM:(h + 1) * MLA_PAD_DIM]
                k = k_ref[0, pl.ds(off, tq), h * MLA_PAD_DIM:(h + 1) * MLA_PAD_DIM]
                v = v_ref[0, pl.ds(off, tq), h * VEXT:(h + 1) * VEXT]
                s = _dot_nt(q, k)
                if shift is not None:
                    s = s - shift
                if mask is not None:
                    s = jnp.where(mask, s, NEG)
                _softmax_chunk(s, v, m_ref, acc_ref, h, running_max=shift is None)

        def full_chunk(c, carry):
            chunk(pl.multiple_of(c * tq, tq), None)
            return carry

        lax.fori_loop(0, qi, full_chunk, 0)
        chunk(pl.multiple_of(qi * tq, tq), _causal_mask(0, 0, tq, tq))

    _with_score_bound(bound_ref[0, 0], attend)
    for h in range(GROUP_HEADS):
        o_ref[0, :, h * LANES:(h + 1) * LANES] = _normalised(acc_ref, h).astype(o_ref.dtype)


def _attn_stick_kernel(tri_ref, q_ref, k_ref, v_ref, o_ref, carry_ref, acc_ref, *, tq, tkc):
    qi = pl.program_id(1)
    carry_ref[...] = jnp.zeros(carry_ref.shape, F32)
    acc_ref[...] = jnp.zeros(acc_ref.shape, F32)

    def chunk(off, mask):
        tri = tri_ref[...]
        for h in range(GROUP_HEADS):
            q = q_ref[0, :, h * LANES:(h + 1) * LANES]
            k = k_ref[0, pl.ds(off, tkc), h * LANES:(h + 1) * LANES]
            v = v_ref[0, pl.ds(off, tkc), h * LANES:(h + 1) * LANES]
            z = _dot_nt(q, k)
            log_b = jnp.minimum(z, 0.0) - jnp.log2(1.0 + jnp.exp2(-jnp.abs(z)))
            log_1mb = log_b - z
            if mask is not None:
                log_1mb = jnp.where(mask, log_1mb, 0.0)
            hi = log_1mb.astype(BF16)
            lo = (log_1mb - hi.astype(F32)).astype(BF16)
            inner = _dot(hi, tri) + _dot(lo, tri)
            carry = carry_ref[h]
            a = jnp.exp2(log_b + inner + jnp.tile(carry, (1, tkc // LANES)))
            if mask is not None:
                a = jnp.where(mask, a, 0.0)
            acc_ref[h] += _dot(a.astype(BF16), v)
            carry_ref[h] = carry + (inner[:, 0:1] + log_1mb[:, 0:1])

    n_diag = tq // tkc
    for j in range(n_diag):
        col0 = (n_diag - 1 - j) * tkc
        chunk(pl.multiple_of(qi * tq + col0, tkc), _causal_mask(0, col0, tq, tkc, strict=True))

    n_full = qi * n_diag

    def full_chunk(state):
        c, _ = state
        chunk(pl.multiple_of((n_full - 1 - c) * tkc, tkc), None)
        return c + 1, (jnp.max(carry_ref[...]) > STICK_DEAD_LOG2).astype(jnp.int32)

    lax.while_loop(lambda state: jnp.logical_and(state[0] < n_full, state[1] > 0), full_chunk,
                   (jnp.int32(0), jnp.int32(1)))
    for h in range(GROUP_HEADS):
        o_ref[0, :, h * LANES:(h + 1) * LANES] = acc_ref[h].astype(o_ref.dtype)


def _resident_attention(kernel, q, k, v, extra, *, tq, scratch, name, score_bound=None):
    b, s, _ = q.shape
    q_map = lambda bi, qi: (bi, qi, 0)
    all_map = lambda bi, qi: (bi, 0, 0)
    extra_specs = [pl.BlockSpec(e.shape, lambda bi, qi, nd=e.ndim: (0,) * nd) for e in extra]
    if score_bound is not None:
        extra = [score_bound] + list(extra)
        extra_specs = [pl.BlockSpec(memory_space=pltpu.SMEM)] + extra_specs
    return pl.pallas_call(
        kernel,
        grid=(b, s // tq),
        in_specs=extra_specs + [pl.BlockSpec((1, tq, q.shape[2]), q_map),
                                pl.BlockSpec((1, s, k.shape[2]), all_map),
                                pl.BlockSpec((1, s, v.shape[2]), all_map)],
        out_specs=pl.BlockSpec((1, tq, GROUP_WIDTH), q_map),
        out_shape=jax.ShapeDtypeStruct((b, s, GROUP_WIDTH), BF16),
        scratch_shapes=scratch,
        compiler_params=_cparams(2),
        name=name,
    )(*extra, q, k, v)


def _attn_dsa_kernel(bound_ref, q_ref, k_ref, v_ref, iq_ref, ik_ref, iw_ref, earlier_ref, o_ref, key_ref, half_ref,
                     m_ref, acc_ref, *, tq, topk):
    qi = pl.program_id(1)
    nkc = qi + 1
    iw_t = iw_ref[0].T

    def score_chunk(c, mask):
        off = pl.multiple_of(c * tq, tq)
        ikc = ik_ref[0, pl.ds(off, tq), :]
        isc = jnp.zeros((tq, tq), F32)
        for hp in range(IDX_HEADS // 2):
            iqp = iq_ref[0, :, hp * LANES:(hp + 1) * LANES]
            for e in range(2):
                hh = 2 * hp + e
                s = _dot_nt(ikc[:, e * LANES:(e + 1) * LANES], iqp)
                isc = isc + jnp.maximum(s, 0.0) * iw_t[hh:hh + 1, :]
        bits = lax.bitcast_convert_type(isc, jnp.int32)
        key = jnp.where(bits < 0, bits ^ jnp.int32(0x7FFFFFFF), bits)
        if mask is not None:
            key = jnp.where(mask, key, INT_MIN)
        key_ref[c] = key
        half_ref[c] = (key >> 16).astype(jnp.int16)

    def full_score_chunk(c, carry):
        score_chunk(c, None)
        return carry

    lax.fori_loop(0, qi, full_score_chunk, 0)
    key_pos = lax.broadcasted_iota(jnp.int32, (tq, tq), 0)
    query_pos = lax.broadcasted_iota(jnp.int32, (tq, tq), 1)
    score_chunk(qi, key_pos <= query_pos)

    one16, zero16, lowest16 = jnp.int16(1), jnp.int16(0), jnp.int16(-32768)

    @pl.when(nkc % COUNT_UNROLL == 1)
    def _():
        half_ref[nkc] = jnp.full((tq, tq), lowest16, jnp.int16)

    def row_count(bound, strict):
        def body(t, part):
            for u in range(COUNT_UNROLL):
                for r in range(tq // COUNT_ACC_ROWS):
                    half = half_ref[COUNT_UNROLL * t + u, r * COUNT_ACC_ROWS:(r + 1) * COUNT_ACC_ROWS, :]
                    part = part + jnp.where(half > bound if strict else half >= bound, one16, zero16)
            return part

        part = lax.fori_loop(0, (nkc + COUNT_UNROLL - 1) // COUNT_UNROLL, body,
                             jnp.zeros((COUNT_ACC_ROWS, tq), jnp.int16))
        return jnp.sum(part.astype(F32), axis=0, keepdims=True)

    def search16(rank, cnt_init):
        def bit(i, carry):
            thr_u, cnt_thr = carry
            cand_u = thr_u | lax.shift_left(jnp.int32(1), 15 - i)
            cnt = row_count((cand_u - 32768).astype(jnp.int16), strict=False)
            take = cnt >= rank
            return jnp.where(take, cand_u, thr_u), jnp.where(take, cnt, cnt_thr)

        return lax.fori_loop(0, 16, bit, (jnp.zeros((1, tq), jnp.int32), cnt_init))

    k_f = jnp.full((1, tq), float(topk), F32)
    hi_u, cnt_hi = search16(k_f, jnp.zeros((1, tq), F32))
    hi_thr = (hi_u - 32768).astype(jnp.int16)
    cnt_gt = row_count(hi_thr, strict=True)

    def low_halves(c, carry):
        lo = ((key_ref[c] & 0xFFFF) - 32768).astype(jnp.int16)
        half_ref[c] = jnp.where(half_ref[c] == hi_thr, lo, lowest16)
        return carry

    lax.fori_loop(0, nkc, low_halves, 0)
    lo_u, cnt_lo = search16(k_f - cnt_gt, cnt_hi - cnt_gt)
    thr_raw = ((hi_u << 16) | lo_u) ^ jnp.int32(INT_MIN)
    short_row = thr_raw == jnp.int32(INT_MIN)
    thr = jnp.maximum(thr_raw, jnp.int32(INT_MIN + 1))
    cnt_ge = cnt_gt + cnt_lo
    tied = jnp.logical_and(jnp.logical_not(short_row), cnt_ge > k_f)
    any_tied = jnp.max(jnp.where(tied, 1.0, 0.0)) > 0.0

    _init_softmax(m_ref, acc_ref)

    def attend(c, selected, shift):
        bias = jnp.where(selected, 0.0 if shift is None else -shift, NEG).T
        off = pl.multiple_of(c * tq, tq)
        for h in range(GROUP_HEADS):
            q = q_ref[0, :, h * LANES:(h + 1) * LANES]
            k = k_ref[0, pl.ds(off, tq), h * LANES:(h + 1) * LANES]
            v = v_ref[0, pl.ds(off, tq), h * VEXT:(h + 1) * VEXT]
            _softmax_chunk(_dot_nt(q, k) + bias, v, m_ref, acc_ref, h, running_max=shift is None)

    def attend_untied(shift):
        def attend_chunk(c, carry):
            attend(c, key_ref[c] >= thr, shift)
            return carry

        lax.fori_loop(0, nkc, attend_chunk, 0)

    @pl.when(jnp.logical_not(any_tied))
    def _():
        _with_score_bound(bound_ref[0, 0], attend_untied)

    @pl.when(any_tied)
    def _():
        def eq_chunk(c, n):
            return n + jnp.sum(jnp.where(key_ref[c] == thr, 1.0, 0.0), axis=0, keepdims=True)

        n_eq = lax.fori_loop(0, nkc, eq_chunk, jnp.zeros((1, tq), F32))
        keep = jnp.where(short_row, 0.0, k_f - (cnt_ge - n_eq))

        def attend_chunk(c, seen):
            key = key_ref[c]
            eq = key == thr
            eq_f = jnp.where(eq, 1.0, 0.0)
            rank = seen + _dot(earlier_ref[...], eq_f.astype(BF16))
            attend(c, jnp.logical_or(key > thr, jnp.logical_and(eq, rank < keep)), None)
            return seen + jnp.sum(eq_f, axis=0, keepdims=True)

        lax.fori_loop(0, nkc, attend_chunk, jnp.zeros((1, tq), F32))

    for h in range(GROUP_HEADS):
        o_ref[0, :, h * LANES:(h + 1) * LANES] = _normalised(acc_ref, h).astype(o_ref.dtype)


def _attn_dsa(q, k, v, iq, ik, iw, score_bound, *, tq):
    b, s, _ = q.shape
    topk = min(TOPK_MAX, s // 4)
    earlier = (jnp.arange(tq)[:, None] > jnp.arange(tq)[None, :]).astype(BF16)
    q_map = lambda bi, qi: (bi, qi, 0)
    all_map = lambda bi, qi: (bi, 0, 0)
    return pl.pallas_call(
        functools.partial(_attn_dsa_kernel, tq=tq, topk=topk),
        grid=(b, s // tq),
        in_specs=[pl.BlockSpec(memory_space=pltpu.SMEM),
                  pl.BlockSpec((1, tq, GROUP_WIDTH), q_map),
                  pl.BlockSpec((1, s, GROUP_WIDTH), all_map),
                  pl.BlockSpec((1, s, GROUP_HEADS * VEXT), all_map),
                  pl.BlockSpec((1, tq, IDX_HEADS * IDX_DIM), q_map),
                  pl.BlockSpec((1, s, 2 * LANES), all_map),
                  pl.BlockSpec((1, tq, LANES), q_map),
                  pl.BlockSpec((tq, tq), lambda bi, qi: (0, 0))],
        out_specs=pl.BlockSpec((1, tq, GROUP_WIDTH), q_map),
        out_shape=jax.ShapeDtypeStruct((b, s, GROUP_WIDTH), BF16),
        scratch_shapes=[pltpu.VMEM((s // tq, tq, tq), jnp.int32),
                        pltpu.VMEM((-(-(s // tq) // COUNT_UNROLL) * COUNT_UNROLL, tq, tq), jnp.int16),
                        pltpu.VMEM((GROUP_HEADS, tq, LANES), F32),
                        pltpu.VMEM((GROUP_HEADS, tq, VEXT), F32)],
        compiler_params=_cparams(2),
        name="attn_dsa",
    )(score_bound, q, k, v, iq, ik, iw, earlier)


def _rope_tables(seq, period, half):
    inv = ROPE_THETA ** (-jnp.arange(half, dtype=F32) / half)
    lane = jnp.arange(LANES)
    idx = lane % period
    active = idx < half
    ang = jnp.arange(seq, dtype=jnp.int32).astype(F32)[:, None] * inv[jnp.minimum(idx, half - 1)][None, :]
    cos, sin = jnp.cos(ang), jnp.sin(ang)
    return jnp.stack([jnp.where(active, cos, 1.0), jnp.where(active, jnp.where(lane < LANES // 2, -sin, sin), 0.0)])


def _half_split_layout(dims, half, start=0):
    first = list(range(half)) + list(range(2 * half, 2 * half + (dims - 2 * half) // 2))
    second = list(range(half, 2 * half)) + list(range(2 * half + (dims - 2 * half) // 2, dims))
    lanes = [-1] * LANES
    lanes[start:start + len(first)] = first
    lanes[start + LANES // 2:start + LANES // 2 + len(second)] = second
    return lanes


_HEAD_LAYOUT = _half_split_layout(HEAD_DIM, HEAD_DIM // 8)
_SUB_LAYOUT = _half_split_layout(DIFF_SUB_DIM, DIFF_SUB_DIM // 8)
_PAIR_LAYOUT = [a if a >= 0 else (DIFF_SUB_DIM + b if b >= 0 else -1)
                for a, b in zip(_SUB_LAYOUT, _half_split_layout(DIFF_SUB_DIM, DIFF_SUB_DIM // 8, start=32))]
_ROPE64_LAYOUT = _half_split_layout(MLA_ROPE_DIM, MLA_ROPE_DIM // 2)


def _take_cols(a, layout):
    src = max(layout) + 1
    padded = jnp.concatenate([a.reshape(a.shape[0], -1, src), jnp.zeros((a.shape[0], a.shape[1] // src, 1), a.dtype)],
                             axis=2)
    idx = jnp.asarray([i if i >= 0 else src for i in layout])
    return padded[:, :, idx].reshape(a.shape[0], -1)


def _pad_cols(a, width):
    return jnp.pad(a, ((0, 0), (0, width - a.shape[1])))


def _relayout_w_in(w):
    sizes = (512,) * 3 + (MLA_Q_RANK, MLA_KV_RANK, MLA_ROPE_DIM) + (512,) * 6 + (IDX_HEADS * IDX_DIM, IDX_DIM, IDX_HEADS)
    names = ("a_q", "a_k", "a_v", "b_cq", "b_ckv", "b_kr", "c_q", "c_k", "c_v", "d_q", "d_k", "d_v", "d_iq", "d_ik", "d_iw")
    parts, start = {}, 0
    for nme, sz in zip(names, sizes):
        parts[nme] = w[:, start:start + sz]
        start += sz
    for nme in ("a_q", "a_k", "d_iq"):
        parts[nme] = _take_cols(parts[nme], _PAIR_LAYOUT)
    for nme in ("d_q", "d_k"):
        parts[nme] = _take_cols(parts[nme], _HEAD_LAYOUT)
    parts["d_ik"] = _take_cols(parts["d_ik"], _SUB_LAYOUT)
    parts["b_kr"] = _take_cols(parts["b_kr"], _ROPE64_LAYOUT)
    order = sorted(_OFF, key=_OFF.get)
    ends = [_OFF[nme] for nme in order[1:]] + [Y_WIDTH]
    return jnp.concatenate([_pad_cols(parts[nme], end - _OFF[nme]) for nme, end in zip(order, ends)], axis=1).astype(BF16)


def _relayout_mla_q(w):
    w = w.reshape(w.shape[0], -1, MLA_QK_DIM)
    rope = _take_cols(w[:, :, MLA_NOPE_DIM:].reshape(w.shape[0], -1), _ROPE64_LAYOUT).reshape(w.shape[0], -1, LANES)
    return jnp.concatenate([w[:, :, :MLA_NOPE_DIM], rope], axis=2).reshape(w.shape[0], -1)


def _score_bound(dim, gain_q, gain_k):
    bound = (dim ** 0.5 * LOG2E * SCORE_BOUND_MARGIN) * jnp.max(jnp.abs(gain_q)) * jnp.max(jnp.abs(gain_k))
    return bound.astype(F32).reshape(1, 1)


def _tile_row(g, reps):
    return jnp.tile(g.reshape(1, -1), (1, reps))


def kernel(x, attn_norm, w_in, diff_qk_norm, diff_lambda, diff_subln, mla_q_a_norm, mla_wq_b, mla_kv_a_norm,
           mla_wkv_b, mla_qk_norm, dsa_qk_norm, idx_k_norm, w_o, ffn_norm, w_gate, w_up, w_down):
    b, s, d = x.shape
    n = b * s
    depth = w_in.shape[0]
    tm = min(1024, n)
    tq = min(512, s)
    tabs = (_rope_tables(s, LANES // 4, DIFF_SUB_DIM // 8),
            _rope_tables(s, LANES // 2, MLA_ROPE_DIM // 2),
            _rope_tables(s, LANES // 2, HEAD_DIM // 8))
    sub_head = jnp.arange(GROUP_WIDTH) // LANES * 2 + (jnp.arange(GROUP_WIDTH) % (LANES // 2)) // (LANES // 4)
    ones64 = (sub_head[:, None] == sub_head[None, :]).astype(BF16)
    tk_stick = min(256, s)
    tri = (jnp.arange(tk_stick)[:, None] > jnp.arange(tk_stick)[None, :]).astype(BF16)

    xf = x.reshape(n, d)
    for l in range(depth):
        lam_init = 0.8 - 0.6 * math.exp(-0.3 * l)
        y = _rms_matmul(xf, attn_norm[l].reshape(1, d), _relayout_w_in(w_in[l]), tm=tm, tn=512, out_dtype=F32)

        wq = jnp.pad(_relayout_mla_q(mla_wq_b[l]), ((0, 512 - MLA_Q_RANK), (0, 0))).astype(BF16)
        wkv = mla_wkv_b[l].reshape(MLA_KV_RANK, GROUP_HEADS, 2, HEAD_DIM).transpose(0, 2, 1, 3)
        wkv = wkv.reshape(MLA_KV_RANK, 2 * GROUP_WIDTH).astype(BF16)
        gains = [
            _take_cols(_tile_row(diff_qk_norm[l, 0], 8), _PAIR_LAYOUT),
            _take_cols(_tile_row(diff_qk_norm[l, 1], 8), _PAIR_LAYOUT),
            _pad_cols(mla_q_a_norm[l].reshape(1, -1), 512), mla_kv_a_norm[l].reshape(1, -1),
            _relayout_mla_q(mla_qk_norm[l, 0].reshape(1, MLA_QK_DIM).repeat(GROUP_HEADS, 0).reshape(1, -1)),
            _relayout_mla_q(mla_qk_norm[l, 1].reshape(1, -1)),
            _take_cols(_tile_row(dsa_qk_norm[l, 0], 4), _HEAD_LAYOUT),
            _take_cols(_tile_row(dsa_qk_norm[l, 1], 4), _HEAD_LAYOUT),
            _take_cols(idx_k_norm[l].reshape(1, -1), _SUB_LAYOUT),
        ]
        (qa, ka, va, qb, kb, vb, qc, kc, vc, qd, kd, vd, iq, ik, iw) = [
            a.reshape(b, s, a.shape[1]) for a in _prep(y, s, tabs, ones64, gains, wq, wkv, tm=min(256, s))]

        o_a = _resident_attention(
            functools.partial(_attn_diff_kernel, tq=tq, lam_init=lam_init), qa, ka, va,
            [diff_lambda[l], diff_subln[l].reshape(1, HEAD_DIM)], tq=tq,
            scratch=[pltpu.VMEM((8, tq, LANES), F32), pltpu.VMEM((8, tq, VEXT), F32)], name="attn_diff",
            score_bound=_score_bound(DIFF_SUB_DIM, diff_qk_norm[l, 0], diff_qk_norm[l, 1]))
        o_b = _resident_attention(
            functools.partial(_attn_mla_kernel, tq=tq), qb, kb, vb, [], tq=tq,
            scratch=[pltpu.VMEM((4, tq, LANES), F32), pltpu.VMEM((4, tq, VEXT), F32)], name="attn_mla",
            score_bound=_score_bound(MLA_QK_DIM, mla_qk_norm[l, 0], mla_qk_norm[l, 1]))
        o_c = _resident_attention(
            functools.partial(_attn_stick_kernel, tq=tq, tkc=tk_stick), qc, kc, vc, [tri], tq=tq,
            scratch=[pltpu.VMEM((4, tq, LANES), F32), pltpu.VMEM((4, tq, LANES), F32)], name="attn_stick")
        o_d = _attn_dsa(qd, kd, vd, iq, ik, iw, _score_bound(HEAD_DIM, dsa_qk_norm[l, 0], dsa_qk_norm[l, 1]), tq=tq)

        mixed = [o.reshape(n, GROUP_WIDTH) for o in (o_a, o_b, o_c, o_d)]
        xf = _matmul_residual(mixed, w_o[l].astype(BF16), xf, tm=min(512, n), tn=d)
        act = _ffn_up(xf, ffn_norm[l].reshape(1, d), w_gate[l], w_up[l], tm=tm, tn=512)
        xf = _matmul_residual([act], w_down[l].astype(BF16), xf, tm=tm, tn=512)
    return xf.reshape(b, s, d)
```

```python
import functools
import math

import jax
import jax.numpy as jnp
import numpy as np
from jax import lax
from jax.experimental import pallas as pl
from jax.experimental.pallas import tpu as pltpu

F32 = jnp.float32
BF16 = jnp.bfloat16

HEAD_DIM = 128
GROUP_HEADS = 4
GROUP_WIDTH = GROUP_HEADS * HEAD_DIM
ROPE_THETA = 500000.0
EPS = 1e-6
DIFF_SUB_DIM = 64
MLA_Q_RANK = 448
MLA_KV_RANK = 128
MLA_NOPE_DIM = 128
MLA_ROPE_DIM = 64
MLA_QK_DIM = MLA_NOPE_DIM + MLA_ROPE_DIM
MLA_PAD_DIM = 256
IDX_HEADS = 16
IDX_DIM = 64
TOPK_MAX = 256

LANES = 128
VEXT = 2 * LANES
Y_WIDTH = 13 * GROUP_WIDTH
NEG = -1e30
INT_MIN = -2147483648
LOG2E = 1.4426950408889634
COUNT_UNROLL = 2
COUNT_ACC_ROWS = 64
MAX_FIXED_SHIFT = 56.0
SCORE_BOUND_MARGIN = 1.02
STICK_DEAD_LOG2 = -160.0
VMEM_LIMIT = 56 * 1024 * 1024

_OFF = dict(a_q=0, a_k=512, a_v=1024, b_cq=1536, b_ckv=2048, b_kr=2176, d_ik=2304, d_iw=2432,
            c_q=2560, c_k=3072, c_v=3584, d_q=4096, d_k=4608, d_v=5120, d_iq=5632)


def _cparams(n_axes):
    return pltpu.CompilerParams(dimension_semantics=("arbitrary",) * n_axes, vmem_limit_bytes=VMEM_LIMIT)


def _dot(a, b):
    return jnp.dot(a, b, preferred_element_type=F32)


def _dot_nt(a, b):
    return lax.dot_general(a, b, (((1,), (1,)), ((), ())), preferred_element_type=F32)


def _rms_rows(x, g):
    ms = jnp.mean(x * x, axis=-1, keepdims=True)
    return x * lax.rsqrt(ms + EPS) * g


def _rms_matmul_kernel(x_ref, g_ref, w_ref, o_ref, h_ref):
    @pl.when(pl.program_id(1) == 0)
    def _():
        h_ref[...] = _rms_rows(x_ref[...], g_ref[...]).astype(BF16)

    o_ref[...] = _dot(h_ref[...], w_ref[...]).astype(o_ref.dtype)


def _rms_matmul(x, g, w, layer, *, tm, tn, out_dtype):
    n, d = x.shape
    m = w.shape[2]
    return pl.pallas_call(
        _rms_matmul_kernel,
        grid=(n // tm, m // tn),
        in_specs=[pl.BlockSpec((tm, d), lambda i, j: (i, 0)),
                  pl.BlockSpec((None, 1, d), lambda i, j: (layer, 0, 0)),
                  pl.BlockSpec((None, d, tn), lambda i, j: (layer, 0, j))],
        out_specs=pl.BlockSpec((tm, tn), lambda i, j: (i, j)),
        out_shape=jax.ShapeDtypeStruct((n, m), out_dtype),
        scratch_shapes=[pltpu.VMEM((tm, d), BF16)],
        compiler_params=_cparams(2),
        name="rms_matmul",
    )(x, g, w)


def _ffn_up_kernel(x_ref, g_ref, wg_ref, wu_ref, o_ref, h_ref):
    @pl.when(pl.program_id(1) == 0)
    def _():
        h_ref[...] = _rms_rows(x_ref[...], g_ref[...]).astype(BF16)

    h = h_ref[...]
    a = _dot(h, wg_ref[...].astype(BF16))
    b = _dot(h, wu_ref[...].astype(BF16))
    o_ref[...] = (jax.nn.silu(a) * b).astype(o_ref.dtype)


def _ffn_up(x, g, wg, wu, layer, *, tm, tn):
    n, d = x.shape
    m = wg.shape[2]
    return pl.pallas_call(
        _ffn_up_kernel,
        grid=(n // tm, m // tn),
        in_specs=[pl.BlockSpec((tm, d), lambda i, j: (i, 0)),
                  pl.BlockSpec((None, 1, d), lambda i, j: (layer, 0, 0)),
                  pl.BlockSpec((None, d, tn), lambda i, j: (layer, 0, j)),
                  pl.BlockSpec((None, d, tn), lambda i, j: (layer, 0, j))],
        out_specs=pl.BlockSpec((tm, tn), lambda i, j: (i, j)),
        out_shape=jax.ShapeDtypeStruct((n, m), BF16),
        scratch_shapes=[pltpu.VMEM((tm, d), BF16)],
        compiler_params=_cparams(2),
        name="ffn_up",
    )(x, g, wg, wu)


def _matmul_residual_kernel(*refs):
    a_refs, (w_ref, r_ref, o_ref) = refs[:-3], refs[-3:]
    acc = r_ref[...]
    row = 0
    for a_ref in a_refs:
        k = a_ref.shape[1]
        acc = acc + _dot(a_ref[...], w_ref[row:row + k, :])
        row += k
    o_ref[...] = acc


def _matmul_residual(parts, w, r, layer, *, tm, tn):
    n = parts[0].shape[0]
    _, k, m = w.shape
    return pl.pallas_call(
        _matmul_residual_kernel,
        grid=(n // tm, m // tn),
        in_specs=[pl.BlockSpec((tm, a.shape[1]), lambda i, j: (i, 0)) for a in parts]
                 + [pl.BlockSpec((None, k, tn), lambda i, j: (layer, 0, j)),
                    pl.BlockSpec((tm, tn), lambda i, j: (i, j))],
        out_specs=pl.BlockSpec((tm, tn), lambda i, j: (i, j)),
        out_shape=jax.ShapeDtypeStruct((n, m), F32),
        compiler_params=_cparams(2),
        name="matmul_residual",
    )(*parts, w, r)


def _rope(x, t_ref, blocks=None):
    outs = []
    for b in range(x.shape[1] // LANES):
        xb = x[:, b * LANES:(b + 1) * LANES]
        if blocks is None or b in blocks:
            xb = xb * t_ref[0] + pltpu.roll(xb, LANES // 2, 1) * t_ref[1]
        outs.append(xb)
    return outs[0] if len(outs) == 1 else jnp.concatenate(outs, axis=1)


def _rms_lane_groups(x, width, count):
    outs = []
    for h in range(x.shape[1] // width):
        xh = x[:, h * width:(h + 1) * width]
        ms = jnp.sum(xh * xh, axis=-1, keepdims=True) * (1.0 / count)
        outs.append(xh * lax.rsqrt(ms + EPS))
    return outs[0] if len(outs) == 1 else jnp.concatenate(outs, axis=1)


def _rms_sub64(x, ones_ref):
    x2 = x * x
    hi = x2.astype(BF16)
    lo = (x2 - hi.astype(F32)).astype(BF16)
    ms = (_dot(hi, ones_ref[...]) + _dot(lo, ones_ref[...])) * (1.0 / DIFF_SUB_DIM)
    return x * lax.rsqrt(ms + EPS)


def _store_vext(ref, v):
    ones = jnp.ones((v.shape[0], LANES), BF16)
    for h in range(GROUP_HEADS):
        ref[:, h * VEXT:h * VEXT + LANES] = v[:, h * LANES:(h + 1) * LANES].astype(BF16)
        ref[:, h * VEXT + LANES:(h + 1) * VEXT] = ones


def _prep_kernel(y_ref, ta_ref, tb_ref, td_ref, ones_ref, g_ref, wq_ref, wkv_ref,
                 qa_ref, ka_ref, va_ref, qb_ref, kb_ref, vb_ref, qc_ref, kc_ref, vc_ref,
                 qd_ref, kd_ref, vd_ref, iq_ref, ik_ref, iw_ref):
    def sec(name, width):
        return y_ref[:, _OFF[name]:_OFF[name] + width]

    def gain(name, width):
        row = _GAIN_ROWS.index(name)
        return g_ref[row:row + 1, :width]

    tm = y_ref.shape[0]
    lane = lax.broadcasted_iota(jnp.int32, (tm, LANES), 1)
    first_half = (lane % (LANES // 2)) < DIFF_SUB_DIM // 2

    qa = _rope(_rms_sub64(sec("a_q", 512), ones_ref) * gain("a_q", 512), ta_ref) * (DIFF_SUB_DIM ** -0.5 * LOG2E)
    for h in range(GROUP_HEADS):
        qh = qa[:, h * LANES:(h + 1) * LANES]
        qa_ref[:, (2 * h) * LANES:(2 * h + 1) * LANES] = jnp.where(first_half, qh, 0.0).astype(BF16)
        qa_ref[:, (2 * h + 1) * LANES:(2 * h + 2) * LANES] = jnp.where(first_half, 0.0, qh).astype(BF16)
    ka_ref[...] = _rope(_rms_sub64(sec("a_k", 512), ones_ref) * gain("a_k", 512), ta_ref).astype(BF16)
    _store_vext(va_ref, sec("a_v", 512))

    cq = sec("b_cq", 512)
    cq = cq * lax.rsqrt(jnp.sum(cq * cq, axis=-1, keepdims=True) * (1.0 / MLA_Q_RANK) + EPS) * gain("b_cq", 512)
    qb = _dot(cq.astype(BF16), wq_ref[...])
    qb = _rms_lane_groups(qb, MLA_PAD_DIM, MLA_QK_DIM) * gain("b_q", 1024)
    ckv = _rms_lane_groups(sec("b_ckv", 128), 128, MLA_KV_RANK) * gain("b_ckv", 128)
    kv = _dot(ckv.astype(BF16), wkv_ref[...])
    kr = sec("b_kr", 128)
    kr_ss = jnp.sum(kr * kr, axis=-1, keepdims=True)
    for h in range(GROUP_HEADS):
        lo, hi = h * MLA_PAD_DIM, (h + 1) * MLA_PAD_DIM
        qb_ref[:, lo:hi] = (_rope(qb[:, lo:hi], tb_ref, blocks=(1,)) * (MLA_QK_DIM ** -0.5 * LOG2E)).astype(BF16)
        kn = kv[:, h * LANES:(h + 1) * LANES]
        ms = (jnp.sum(kn * kn, axis=-1, keepdims=True) + kr_ss) * (1.0 / MLA_QK_DIM)
        kh = jnp.concatenate([kn, kr], axis=1) * lax.rsqrt(ms + EPS) * gain("b_k", MLA_PAD_DIM)
        kb_ref[:, lo:hi] = _rope(kh, tb_ref, blocks=(1,)).astype(BF16)
    _store_vext(vb_ref, kv[:, GROUP_WIDTH:])

    qc_ref[...] = (sec("c_q", 512) * (HEAD_DIM ** -0.5 * LOG2E)).astype(BF16)
    kc_ref[...] = sec("c_k", 512).astype(BF16)
    vc_ref[...] = sec("c_v", 512).astype(BF16)

    qd = _rope(_rms_lane_groups(sec("d_q", 512), HEAD_DIM, HEAD_DIM) * gain("d_q", 512), td_ref)
    qd_ref[...] = (qd * (HEAD_DIM ** -0.5 * LOG2E)).astype(BF16)
    kd_ref[...] = _rope(_rms_lane_groups(sec("d_k", 512), HEAD_DIM, HEAD_DIM) * gain("d_k", 512), td_ref).astype(BF16)
    _store_vext(vd_ref, sec("d_v", 512))
    for half in range(2):
        iq = y_ref[:, _OFF["d_iq"] + half * 512:_OFF["d_iq"] + (half + 1) * 512]
        iq_ref[:, half * 512:(half + 1) * 512] = (_rope(iq, ta_ref) * (IDX_DIM ** -0.5)).astype(BF16)
    ik = _rope(_rms_lane_groups(sec("d_ik", 128), 128, IDX_DIM) * gain("d_ik", 128), ta_ref)
    ik_ref[:, :LANES] = ik.astype(BF16)
    ik_ref[:, LANES:] = pltpu.roll(ik, IDX_DIM // 2, 1).astype(BF16)
    iw_ref[...] = sec("d_iw", 128) * (IDX_HEADS ** -0.5)


def _prep(y, seq, tabs, ones64, gains, wq, wkv, layer, *, tm):
    n = y.shape[0]
    nblk_seq = seq // tm
    ta, tb, td = tabs

    def rows(width):
        return pl.BlockSpec((tm, width), lambda i: (i, 0))

    def table():
        return pl.BlockSpec((2, tm, LANES), lambda i: (0, i % nblk_seq, 0))

    def whole(a):
        return pl.BlockSpec(a.shape, lambda i: (0,) * a.ndim)

    def of_layer(a):
        return pl.BlockSpec((None,) + a.shape[1:], lambda i: (layer,) + (0,) * (a.ndim - 1))

    widths = [1024, 512, 1024, 1024, 1024, 1024, 512, 512, 512, 512, 512, 1024, 1024, 256]
    out_shape = [jax.ShapeDtypeStruct((n, w), BF16) for w in widths] + [jax.ShapeDtypeStruct((n, LANES), F32)]
    out_specs = [rows(w) for w in widths] + [rows(LANES)]
    return pl.pallas_call(
        _prep_kernel,
        grid=(n // tm,),
        in_specs=[rows(Y_WIDTH), table(), table(), table(), whole(ones64), of_layer(gains), of_layer(wq), of_layer(wkv)],
        out_specs=out_specs,
        out_shape=out_shape,
        compiler_params=_cparams(1),
        name="prep",
    )(y, ta, tb, td, ones64, gains, wq, wkv)


def _softmax_chunk(s, vext, m_ref, acc_ref, idx, running_max):
    if not running_max:
        acc_ref[idx] += _dot(jnp.exp2(s).astype(BF16), vext)
        return
    m_prev = m_ref[idx]
    m_new = jnp.maximum(m_prev, jnp.max(s, axis=1, keepdims=True))
    p = jnp.exp2(s - jnp.tile(m_new, (1, s.shape[1] // LANES)))
    alpha = jnp.exp2(m_prev - m_new)
    acc_ref[idx] = jnp.tile(alpha, (1, VEXT // LANES)) * acc_ref[idx] + _dot(p.astype(BF16), vext)
    m_ref[idx] = m_new


def _init_softmax(m_ref, acc_ref):
    m_ref[...] = jnp.full(m_ref.shape, NEG, F32)
    acc_ref[...] = jnp.zeros(acc_ref.shape, F32)


def _with_score_bound(bound, attend):
    @pl.when(bound <= MAX_FIXED_SHIFT)
    def _():
        attend(bound)

    @pl.when(jnp.logical_not(bound <= MAX_FIXED_SHIFT))
    def _():
        attend(None)


def _normalised(acc_ref, idx):
    acc = acc_ref[idx]
    return acc[:, :LANES] / acc[:, LANES:]


def _causal_mask(row0, col0, tq, tkc, strict=False):
    rows = row0 + lax.broadcasted_iota(jnp.int32, (tq, tkc), 0)
    cols = col0 + lax.broadcasted_iota(jnp.int32, (tq, tkc), 1)
    return cols < rows if strict else cols <= rows


def _attn_diff_kernel(bound_ref, lam_ref, sub_ref, q_ref, k_ref, v_ref, o_ref, m_ref, acc_ref, *, tq, layer, lam_init):
    qi = pl.program_id(1)
    _init_softmax(m_ref, acc_ref)

    def attend(shift):
        def chunk(off, mask):
            for mp in range(2 * GROUP_HEADS):
                h = mp // 2
                q = q_ref[0, :, mp * LANES:(mp + 1) * LANES]
                k = k_ref[0, pl.ds(off, tq), h * LANES:(h + 1) * LANES]
                v = v_ref[0, pl.ds(off, tq), h * VEXT:(h + 1) * VEXT]
                s = _dot_nt(q, k)
                if shift is not None:
                    s = s - shift
                if mask is not None:
                    s = jnp.where(mask, s, NEG)
                _softmax_chunk(s, v, m_ref, acc_ref, mp, running_max=shift is None)

        def full_chunk(c, carry):
            chunk(pl.multiple_of(c * tq, tq), None)
            return carry

        lax.fori_loop(0, qi, full_chunk, 0)
        chunk(pl.multiple_of(qi * tq, tq), _causal_mask(0, 0, tq, tq))

    _with_score_bound(bound_ref[layer, _BOUND_COLS.index("diff")], attend)

    lv = lam_ref[...]
    lam = (jnp.exp(jnp.sum(lv[0:1] * lv[1:2], axis=-1, keepdims=True))
           - jnp.exp(jnp.sum(lv[2:3] * lv[3:4], axis=-1, keepdims=True)) + lam_init)
    for h in range(GROUP_HEADS):
        o = _normalised(acc_ref, 2 * h) - lam * _normalised(acc_ref, 2 * h + 1)
        o = _rms_rows(o, sub_ref[...]) * (1.0 - lam_init)
        o_ref[0, :, h * LANES:(h + 1) * LANES] = o.astype(o_ref.dtype)


def _attn_mla_kernel(bound_ref, q_ref, k_ref, v_ref, o_ref, m_ref, acc_ref, *, tq, layer):
    qi = pl.program_id(1)
    _init_softmax(m_ref, acc_ref)

    def attend(shift):
        def chunk(off, mask):
            for h in range(GROUP_HEADS):
                q = q_ref[0, :, h * MLA_PAD_DIM:(h + 1) * MLA_PAD_DIM]
                k = k_ref[0, pl.ds(off, tq), h * MLA_PAD_DIM:(h + 1) * MLA_PAD_DIM]
                v = v_ref[0, pl.ds(off, tq), h * VEXT:(h + 1) * VEXT]
                s = _dot_nt(q, k)
                if shift is not None:
                    s = s - shift
                if mask is not None:
                    s = jnp.where(mask, s, NEG)
                _softmax_chunk(s, v, m_ref, acc_ref, h, running_max=shift is None)

        def full_chunk(c, carry):
            chunk(pl.multiple_of(c * tq, tq), None)
            return carry

        lax.fori_loop(0, qi, full_chunk, 0)
        chunk(pl.multiple_of(qi * tq, tq), _causal_mask(0, 0, tq, tq))

    _with_score_bound(bound_ref[layer, _BOUND_COLS.index("mla")], attend)
    for h in range(GROUP_HEADS):
        o_ref[0, :, h * LANES:(h + 1) * LANES] = _normalised(acc_ref, h).astype(o_ref.dtype)


def _attn_stick_kernel(tri_ref, q_ref, k_ref, v_ref, o_ref, carry_ref, acc_ref, *, tq, tkc):
    qi = pl.program_id(1)
    carry_ref[...] = jnp.zeros(carry_ref.shape, F32)
    acc_ref[...] = jnp.zeros(acc_ref.shape, F32)

    def chunk(off, mask):
        tri = tri_ref[...]
        for h in range(GROUP_HEADS):
            q = q_ref[0, :, h * LANES:(h + 1) * LANES]
            k = k_ref[0, pl.ds(off, tkc), h * LANES:(h + 1) * LANES]
            v = v_ref[0, pl.ds(off, tkc), h * LANES:(h + 1) * LANES]
            z = _dot_nt(q, k)
            log_b = jnp.minimum(z, 0.0) - jnp.log2(1.0 + jnp.exp2(-jnp.abs(z)))
            log_1mb = log_b - z
            if mask is not None:
                log_1mb = jnp.where(mask, log_1mb, 0.0)
            hi = log_1mb.astype(BF16)
            lo = (log_1mb - hi.astype(F32)).astype(BF16)
            inner = _dot(hi, tri) + _dot(lo, tri)
            carry = carry_ref[h]
            a = jnp.exp2(log_b + inner + jnp.tile(carry, (1, tkc // LANES)))
            if mask is not None:
                a = jnp.where(mask, a, 0.0)
            acc_ref[h] += _dot(a.astype(BF16), v)
            carry_ref[h] = carry + (inner[:, 0:1] + log_1mb[:, 0:1])

    n_diag = tq // tkc
    for j in range(n_diag):
        col0 = (n_diag - 1 - j) * tkc
        chunk(pl.multiple_of(qi * tq + col0, tkc), _causal_mask(0, col0, tq, tkc, strict=True))

    n_full = qi * n_diag

    def full_chunk(state):
        c, _ = state
        chunk(pl.multiple_of((n_full - 1 - c) * tkc, tkc), None)
        return c + 1, (jnp.max(carry_ref[...]) > STICK_DEAD_LOG2).astype(jnp.int32)

    lax.while_loop(lambda state: jnp.logical_and(state[0] < n_full, state[1] > 0), full_chunk,
                   (jnp.int32(0), jnp.int32(1)))
    for h in range(GROUP_HEADS):
        o_ref[0, :, h * LANES:(h + 1) * LANES] = acc_ref[h].astype(o_ref.dtype)


def _resident_attention(kernel, q, k, v, extra, layer_extra, layer, *, tq, scratch, name, score_bounds=None):
    b, s, _ = q.shape
    q_map = lambda bi, qi: (bi, qi, 0)
    all_map = lambda bi, qi: (bi, 0, 0)
    extra_specs = [pl.BlockSpec(e.shape, lambda bi, qi, nd=e.ndim: (0,) * nd) for e in extra]
    extra_specs += [pl.BlockSpec((None,) + e.shape[1:], lambda bi, qi, nd=e.ndim: (layer,) + (0,) * (nd - 1))
                    for e in layer_extra]
    extra = list(extra) + list(layer_extra)
    if score_bounds is not None:
        extra = [score_bounds] + extra
        extra_specs = [pl.BlockSpec(memory_space=pltpu.SMEM)] + extra_specs
    return pl.pallas_call(
        kernel,
        grid=(b, s // tq),
        in_specs=extra_specs + [pl.BlockSpec((1, tq, q.shape[2]), q_map),
                                pl.BlockSpec((1, s, k.shape[2]), all_map),
                                pl.BlockSpec((1, s, v.shape[2]), all_map)],
        out_specs=pl.BlockSpec((1, tq, GROUP_WIDTH), q_map),
        out_shape=jax.ShapeDtypeStruct((b, s, GROUP_WIDTH), BF16),
        scratch_shapes=scratch,
        compiler_params=_cparams(2),
        name=name,
    )(*extra, q, k, v)


def _attn_dsa_kernel(bound_ref, q_ref, k_ref, v_ref, iq_ref, ik_ref, iw_ref, earlier_ref, o_ref, key_ref, half_ref,
                     m_ref, acc_ref, *, tq, topk, layer):
    qi = pl.program_id(1)
    nkc = qi + 1
    iw_t = iw_ref[0].T

    def score_chunk(c, mask):
        off = pl.multiple_of(c * tq, tq)
        ikc = ik_ref[0, pl.ds(off, tq), :]
        isc = jnp.zeros((tq, tq), F32)
        for hp in range(IDX_HEADS // 2):
            iqp = iq_ref[0, :, hp * LANES:(hp + 1) * LANES]
            for e in range(2):
                hh = 2 * hp + e
                s = _dot_nt(ikc[:, e * LANES:(e + 1) * LANES], iqp)
                isc = isc + jnp.maximum(s, 0.0) * iw_t[hh:hh + 1, :]
        bits = lax.bitcast_convert_type(isc, jnp.int32)
        key = jnp.where(bits < 0, bits ^ jnp.int32(0x7FFFFFFF), bits)
        if mask is not None:
            key = jnp.where(mask, key, INT_MIN)
        key_ref[c] = key
        half_ref[c] = (key >> 16).astype(jnp.int16)

    def full_score_chunk(c, carry):
        score_chunk(c, None)
        return carry

    lax.fori_loop(0, qi, full_score_chunk, 0)
    key_pos = lax.broadcasted_iota(jnp.int32, (tq, tq), 0)
    query_pos = lax.broadcasted_iota(jnp.int32, (tq, tq), 1)
    score_chunk(qi, key_pos <= query_pos)

    one16, zero16, lowest16 = jnp.int16(1), jnp.int16(0), jnp.int16(-32768)

    @pl.when(nkc % COUNT_UNROLL == 1)
    def _():
        half_ref[nkc] = jnp.full((tq, tq), lowest16, jnp.int16)

    def row_count(bound, strict):
        def body(t, part):
            for u in range(COUNT_UNROLL):
                for r in range(tq // COUNT_ACC_ROWS):
                    half = half_ref[COUNT_UNROLL * t + u, r * COUNT_ACC_ROWS:(r + 1) * COUNT_ACC_ROWS, :]
                    part = part + jnp.where(half > bound if strict else half >= bound, one16, zero16)
            return part

        part = lax.fori_loop(0, (nkc + COUNT_UNROLL - 1) // COUNT_UNROLL, body,
                             jnp.zeros((COUNT_ACC_ROWS, tq), jnp.int16))
        return jnp.sum(part.astype(F32), axis=0, keepdims=True)

    def search16(rank, cnt_init):
        def bit(i, carry):
            thr_u, cnt_thr = carry
            cand_u = thr_u | lax.shift_left(jnp.int32(1), 15 - i)
            cnt = row_count((cand_u - 32768).astype(jnp.int16), strict=False)
            take = cnt >= rank
            return jnp.where(take, cand_u, thr_u), jnp.where(take, cnt, cnt_thr)

        return lax.fori_loop(0, 16, bit, (jnp.zeros((1, tq), jnp.int32), cnt_init))

    k_f = jnp.full((1, tq), float(topk), F32)
    hi_u, cnt_hi = search16(k_f, jnp.zeros((1, tq), F32))
    hi_thr = (hi_u - 32768).astype(jnp.int16)
    cnt_gt = row_count(hi_thr, strict=True)

    def low_halves(c, carry):
        lo = ((key_ref[c] & 0xFFFF) - 32768).astype(jnp.int16)
        half_ref[c] = jnp.where(half_ref[c] == hi_thr, lo, lowest16)
        return carry

    lax.fori_loop(0, nkc, low_halves, 0)
    lo_u, cnt_lo = search16(k_f - cnt_gt, cnt_hi - cnt_gt)
    thr_raw = ((hi_u << 16) | lo_u) ^ jnp.int32(INT_MIN)
    short_row = thr_raw == jnp.int32(INT_MIN)
    thr = jnp.maximum(thr_raw, jnp.int32(INT_MIN + 1))
    cnt_ge = cnt_gt + cnt_lo
    tied = jnp.logical_and(jnp.logical_not(short_row), cnt_ge > k_f)
    any_tied = jnp.max(jnp.where(tied, 1.0, 0.0)) > 0.0

    _init_softmax(m_ref, acc_ref)

    def attend(c, selected, shift):
        bias = jnp.where(selected, 0.0 if shift is None else -shift, NEG).T
        off = pl.multiple_of(c * tq, tq)
        for h in range(GROUP_HEADS):
            q = q_ref[0, :, h * LANES:(h + 1) * LANES]
            k = k_ref[0, pl.ds(off, tq), h * LANES:(h + 1) * LANES]
            v = v_ref[0, pl.ds(off, tq), h * VEXT:(h + 1) * VEXT]
            _softmax_chunk(_dot_nt(q, k) + bias, v, m_ref, acc_ref, h, running_max=shift is None)

    def attend_untied(shift):
        def attend_chunk(c, carry):
            attend(c, key_ref[c] >= thr, shift)
            return carry

        lax.fori_loop(0, nkc, attend_chunk, 0)

    @pl.when(jnp.logical_not(any_tied))
    def _():
        _with_score_bound(bound_ref[layer, _BOUND_COLS.index("dsa")], attend_untied)

    @pl.when(any_tied)
    def _():
        def eq_chunk(c, n):
            return n + jnp.sum(jnp.where(key_ref[c] == thr, 1.0, 0.0), axis=0, keepdims=True)

        n_eq = lax.fori_loop(0, nkc, eq_chunk, jnp.zeros((1, tq), F32))
        keep = jnp.where(short_row, 0.0, k_f - (cnt_ge - n_eq))

        def attend_chunk(c, seen):
            key = key_ref[c]
            eq = key == thr
            eq_f = jnp.where(eq, 1.0, 0.0)
            rank = seen + _dot(earlier_ref[...], eq_f.astype(BF16))
            attend(c, jnp.logical_or(key > thr, jnp.logical_and(eq, rank < keep)), None)
            return seen + jnp.sum(eq_f, axis=0, keepdims=True)

        lax.fori_loop(0, nkc, attend_chunk, jnp.zeros((1, tq), F32))

    for h in range(GROUP_HEADS):
        o_ref[0, :, h * LANES:(h + 1) * LANES] = _normalised(acc_ref, h).astype(o_ref.dtype)


def _attn_dsa(q, k, v, iq, ik, iw, score_bounds, layer, *, tq):
    b, s, _ = q.shape
    topk = min(TOPK_MAX, s // 4)
    earlier = np.asarray(np.arange(tq)[:, None] > np.arange(tq)[None, :], dtype=BF16)
    q_map = lambda bi, qi: (bi, qi, 0)
    all_map = lambda bi, qi: (bi, 0, 0)
    return pl.pallas_call(
        functools.partial(_attn_dsa_kernel, tq=tq, topk=topk, layer=layer),
        grid=(b, s // tq),
        in_specs=[pl.BlockSpec(memory_space=pltpu.SMEM),
                  pl.BlockSpec((1, tq, GROUP_WIDTH), q_map),
                  pl.BlockSpec((1, s, GROUP_WIDTH), all_map),
                  pl.BlockSpec((1, s, GROUP_HEADS * VEXT), all_map),
                  pl.BlockSpec((1, tq, IDX_HEADS * IDX_DIM), q_map),
                  pl.BlockSpec((1, s, 2 * LANES), all_map),
                  pl.BlockSpec((1, tq, LANES), q_map),
                  pl.BlockSpec((tq, tq), lambda bi, qi: (0, 0))],
        out_specs=pl.BlockSpec((1, tq, GROUP_WIDTH), q_map),
        out_shape=jax.ShapeDtypeStruct((b, s, GROUP_WIDTH), BF16),
        scratch_shapes=[pltpu.VMEM((s // tq, tq, tq), jnp.int32),
                        pltpu.VMEM((-(-(s // tq) // COUNT_UNROLL) * COUNT_UNROLL, tq, tq), jnp.int16),
                        pltpu.VMEM((GROUP_HEADS, tq, LANES), F32),
                        pltpu.VMEM((GROUP_HEADS, tq, VEXT), F32)],
        compiler_params=_cparams(2),
        name="attn_dsa",
    )(score_bounds, q, k, v, iq, ik, iw, earlier)


def _rope_tables(seq, period, half):
    inv = np.float32(ROPE_THETA) ** (-np.arange(half, dtype=np.float32) / np.float32(half))
    lane = np.arange(LANES)
    idx = lane % period
    active = idx < half
    ang = np.arange(seq, dtype=np.float32)[:, None] * inv[np.minimum(idx, half - 1)][None, :]
    cos, sin = np.cos(ang), np.sin(ang)
    return np.stack([np.where(active, cos, 1.0), np.where(active, np.where(lane < LANES // 2, -sin, sin), 0.0)]
                    ).astype(np.float32)


def _half_split_layout(dims, half, start=0):
    first = list(range(half)) + list(range(2 * half, 2 * half + (dims - 2 * half) // 2))
    second = list(range(half, 2 * half)) + list(range(2 * half + (dims - 2 * half) // 2, dims))
    lanes = [-1] * LANES
    lanes[start:start + len(first)] = first
    lanes[start + LANES // 2:start + LANES // 2 + len(second)] = second
    return lanes


_HEAD_LAYOUT = _half_split_layout(HEAD_DIM, HEAD_DIM // 8)
_SUB_LAYOUT = _half_split_layout(DIFF_SUB_DIM, DIFF_SUB_DIM // 8)
_PAIR_LAYOUT = [a if a >= 0 else (DIFF_SUB_DIM + b if b >= 0 else -1)
                for a, b in zip(_SUB_LAYOUT, _half_split_layout(DIFF_SUB_DIM, DIFF_SUB_DIM // 8, start=32))]
_ROPE64_LAYOUT = _half_split_layout(MLA_ROPE_DIM, MLA_ROPE_DIM // 2)


def _take_cols(a, layout):
    src = max(layout) + 1
    blocks = a.reshape(a.shape[:-1] + (-1, src))
    pieces, lane = [], 0
    while lane < LANES:
        end = lane + 1
        while end < LANES and (layout[end] == layout[end - 1] + 1 if layout[lane] >= 0 else layout[end] < 0):
            end += 1
        if layout[lane] >= 0:
            pieces.append(blocks[..., layout[lane]:layout[lane] + end - lane])
        else:
            pieces.append(jnp.zeros(blocks.shape[:-1] + (end - lane,), a.dtype))
        lane = end
    return jnp.concatenate(pieces, axis=-1).reshape(a.shape[:-1] + (-1,))


def _pad_cols(a, width):
    return jnp.pad(a, ((0, 0),) * (a.ndim - 1) + ((0, width - a.shape[-1]),))


def _relayout_w_in(w):
    sizes = (512,) * 3 + (MLA_Q_RANK, MLA_KV_RANK, MLA_ROPE_DIM) + (512,) * 6 + (IDX_HEADS * IDX_DIM, IDX_DIM, IDX_HEADS)
    names = ("a_q", "a_k", "a_v", "b_cq", "b_ckv", "b_kr", "c_q", "c_k", "c_v", "d_q", "d_k", "d_v", "d_iq", "d_ik", "d_iw")
    parts, start = {}, 0
    for nme, sz in zip(names, sizes):
        parts[nme] = w[..., start:start + sz]
        start += sz
    for nme in ("a_q", "a_k", "d_iq"):
        parts[nme] = _take_cols(parts[nme], _PAIR_LAYOUT)
    for nme in ("d_q", "d_k"):
        parts[nme] = _take_cols(parts[nme], _HEAD_LAYOUT)
    parts["d_ik"] = _take_cols(parts["d_ik"], _SUB_LAYOUT)
    parts["b_kr"] = _take_cols(parts["b_kr"], _ROPE64_LAYOUT)
    order = sorted(_OFF, key=_OFF.get)
    ends = [_OFF[nme] for nme in order[1:]] + [Y_WIDTH]
    return jnp.concatenate([_pad_cols(parts[nme], end - _OFF[nme]) for nme, end in zip(order, ends)],
                           axis=-1).astype(BF16)


def _relayout_mla_q(w):
    lead = w.shape[:-1]
    w = w.reshape(lead + (-1, MLA_QK_DIM))
    rope = _take_cols(w[..., MLA_NOPE_DIM:], _ROPE64_LAYOUT)
    return jnp.concatenate([w[..., :MLA_NOPE_DIM], rope], axis=-1).reshape(lead + (-1,))


_BOUND_COLS = ("diff", "mla", "dsa", "unused")
_GAIN_ROWS = ("a_q", "a_k", "b_cq", "b_ckv", "b_q", "b_k", "d_q", "d_k", "d_ik")


def _score_bounds(diff_qk_norm, mla_qk_norm, dsa_qk_norm):
    def bound(dim, qk_norm):
        peak = jnp.max(jnp.abs(qk_norm), axis=-1)
        return (dim ** 0.5 * LOG2E * SCORE_BOUND_MARGIN) * peak[:, 0] * peak[:, 1]

    cols = [bound(DIFF_SUB_DIM, diff_qk_norm), bound(MLA_QK_DIM, mla_qk_norm), bound(HEAD_DIM, dsa_qk_norm)]
    return jnp.stack(cols + [jnp.zeros_like(cols[0])], axis=1).astype(F32)


def _packed_gains(diff_qk_norm, mla_q_a_norm, mla_kv_a_norm, mla_qk_norm, dsa_qk_norm, idx_k_norm):
    def tiled(g, reps):
        return jnp.tile(g, (1, reps))

    rows = dict(
        a_q=_take_cols(tiled(diff_qk_norm[:, 0], 8), _PAIR_LAYOUT),
        a_k=_take_cols(tiled(diff_qk_norm[:, 1], 8), _PAIR_LAYOUT),
        b_cq=mla_q_a_norm, b_ckv=mla_kv_a_norm,
        b_q=_relayout_mla_q(tiled(mla_qk_norm[:, 0], GROUP_HEADS)),
        b_k=_relayout_mla_q(mla_qk_norm[:, 1]),
        d_q=_take_cols(tiled(dsa_qk_norm[:, 0], 4), _HEAD_LAYOUT),
        d_k=_take_cols(tiled(dsa_qk_norm[:, 1], 4), _HEAD_LAYOUT),
        d_ik=_take_cols(idx_k_norm, _SUB_LAYOUT))
    packed = jnp.stack([_pad_cols(rows[nme], 1024) for nme in _GAIN_ROWS], axis=1)
    return jnp.pad(packed, ((0, 0), (0, 16 - len(_GAIN_ROWS)), (0, 0)))


def kernel(x, attn_norm, w_in, diff_qk_norm, diff_lambda, diff_subln, mla_q_a_norm, mla_wq_b, mla_kv_a_norm,
           mla_wkv_b, mla_qk_norm, dsa_qk_norm, idx_k_norm, w_o, ffn_norm, w_gate, w_up, w_down):
    b, s, d = x.shape
    n = b * s
    depth = w_in.shape[0]
    tm = min(1024, n)
    tq = min(512, s)
    tabs = (_rope_tables(s, LANES // 4, DIFF_SUB_DIM // 8),
            _rope_tables(s, LANES // 2, MLA_ROPE_DIM // 2),
            _rope_tables(s, LANES // 2, HEAD_DIM // 8))
    sub_head = np.arange(GROUP_WIDTH) // LANES * 2 + (np.arange(GROUP_WIDTH) % (LANES // 2)) // (LANES // 4)
    ones64 = np.asarray(sub_head[:, None] == sub_head[None, :], dtype=BF16)
    tk_stick = min(256, s)
    tri = np.asarray(np.arange(tk_stick)[:, None] > np.arange(tk_stick)[None, :], dtype=BF16)

    w_in_r = _relayout_w_in(w_in)
    wq = jnp.pad(_relayout_mla_q(mla_wq_b), ((0, 0), (0, 512 - MLA_Q_RANK), (0, 0))).astype(BF16)
    wkv = mla_wkv_b.reshape(depth, MLA_KV_RANK, GROUP_HEADS, 2, HEAD_DIM).transpose(0, 1, 3, 2, 4)
    wkv = wkv.reshape(depth, MLA_KV_RANK, 2 * GROUP_WIDTH).astype(BF16)
    gains = _packed_gains(diff_qk_norm, mla_q_a_norm, mla_kv_a_norm, mla_qk_norm, dsa_qk_norm, idx_k_norm)
    bounds = _score_bounds(diff_qk_norm, mla_qk_norm, dsa_qk_norm)
    attn_gain, ffn_gain = attn_norm.reshape(depth, 1, d), ffn_norm.reshape(depth, 1, d)
    subln = diff_subln.reshape(depth, 1, HEAD_DIM)
    w_o_b, w_down_b = w_o.astype(BF16), w_down.astype(BF16)

    xf = x.reshape(n, d)
    for l in range(depth):
        lam_init = 0.8 - 0.6 * math.exp(-0.3 * l)
        y = _rms_matmul(xf, attn_gain, w_in_r, l, tm=tm, tn=512, out_dtype=F32)
        (qa, ka, va, qb, kb, vb, qc, kc, vc, qd, kd, vd, iq, ik, iw) = [
            a.reshape(b, s, a.shape[1]) for a in _prep(y, s, tabs, ones64, gains, wq, wkv, l, tm=min(256, s))]

        o_a = _resident_attention(
            functools.partial(_attn_diff_kernel, tq=tq, layer=l, lam_init=lam_init), qa, ka, va,
            [], [diff_lambda, subln], l, tq=tq,
            scratch=[pltpu.VMEM((8, tq, LANES), F32), pltpu.VMEM((8, tq, VEXT), F32)], name="attn_diff",
            score_bounds=bounds)
        o_b = _resident_attention(
            functools.partial(_attn_mla_kernel, tq=tq, layer=l), qb, kb, vb, [], [], l, tq=tq,
            scratch=[pltpu.VMEM((4, tq, LANES), F32), pltpu.VMEM((4, tq, VEXT), F32)], name="attn_mla",
            score_bounds=bounds)
        o_c = _resident_attention(
            functools.partial(_attn_stick_kernel, tq=tq, tkc=tk_stick), qc, kc, vc, [tri], [], l, tq=tq,
            scratch=[pltpu.VMEM((4, tq, LANES), F32), pltpu.VMEM((4, tq, LANES), F32)], name="attn_stick")
        o_d = _attn_dsa(qd, kd, vd, iq, ik, iw, bounds, l, tq=tq)

        mixed = [o.reshape(n, GROUP_WIDTH) for o in (o_a, o_b, o_c, o_d)]
        xf = _matmul_residual(mixed, w_o_b, xf, l, tm=min(512, n), tn=d)
        act = _ffn_up(xf, ffn_gain, w_gate, w_up, l, tm=tm, tn=512)
        xf = _matmul_residual([act], w_down_b, xf, l, tm=tm, tn=512)
    return xf.reshape(b, s, d)
```

```python
import functools
import math

import jax
import jax.numpy as jnp
import numpy as np
from jax import lax
from jax.experimental import pallas as pl
from jax.experimental.pallas import tpu as pltpu

F32 = jnp.float32
BF16 = jnp.bfloat16

HEAD_DIM = 128
GROUP_HEADS = 4
GROUP_WIDTH = GROUP_HEADS * HEAD_DIM
ROPE_THETA = 500000.0
EPS = 1e-6
DIFF_SUB_DIM = 64
MLA_Q_RANK = 448
MLA_KV_RANK = 128
MLA_NOPE_DIM = 128
MLA_ROPE_DIM = 64
MLA_QK_DIM = MLA_NOPE_DIM + MLA_ROPE_DIM
MLA_PAD_DIM = 256
IDX_HEADS = 16
IDX_DIM = 64
TOPK_MAX = 256

LANES = 128
VEXT = 2 * LANES
Y_WIDTH = 13 * GROUP_WIDTH
NEG = -1e30
INT_MIN = -2147483648
LOG2E = 1.4426950408889634
COUNT_UNROLL = 2
COUNT_ACC_ROWS = 64
MAX_FIXED_SHIFT = 56.0
SCORE_BOUND_MARGIN = 1.02
STICK_DEAD_LOG2 = -160.0
VMEM_LIMIT = 56 * 1024 * 1024

_OFF = dict(a_q=0, a_k=512, a_v=1024, b_cq=1536, b_ckv=2048, b_kr=2176, d_ik=2304, d_iw=2432,
            c_q=2560, c_k=3072, c_v=3584, d_q=4096, d_k=4608, d_v=5120, d_iq=5632)


def _cparams(n_axes):
    return pltpu.CompilerParams(dimension_semantics=("arbitrary",) * n_axes, vmem_limit_bytes=VMEM_LIMIT)


def _dot(a, b):
    return jnp.dot(a, b, preferred_element_type=F32)


def _dot_nt(a, b):
    return lax.dot_general(a, b, (((1,), (1,)), ((), ())), preferred_element_type=F32)


def _rms_rows(x, g):
    ms = jnp.mean(x * x, axis=-1, keepdims=True)
    return x * lax.rsqrt(ms + EPS) * g


def _rms_matmul_kernel(x_ref, g_ref, w_ref, o_ref, h_ref):
    @pl.when(pl.program_id(1) == 0)
    def _():
        h_ref[...] = _rms_rows(x_ref[...], g_ref[...]).astype(BF16)

    o_ref[...] = _dot(h_ref[...], w_ref[...]).astype(o_ref.dtype)


def _rms_matmul(x, g, w, layer, *, tm, tn, out_dtype):
    n, d = x.shape
    m = w.shape[2]
    return pl.pallas_call(
        _rms_matmul_kernel,
        grid=(n // tm, m // tn),
        in_specs=[pl.BlockSpec((tm, d), lambda i, j: (i, 0)),
                  pl.BlockSpec((None, 1, d), lambda i, j: (layer, 0, 0)),
                  pl.BlockSpec((None, d, tn), lambda i, j: (layer, 0, j))],
        out_specs=pl.BlockSpec((tm, tn), lambda i, j: (i, j)),
        out_shape=jax.ShapeDtypeStruct((n, m), out_dtype),
        scratch_shapes=[pltpu.VMEM((tm, d), BF16)],
        compiler_params=_cparams(2),
        name="rms_matmul",
    )(x, g, w)


def _ffn_up_kernel(x_ref, g_ref, wg_ref, wu_ref, o_ref, h_ref):
    @pl.when(pl.program_id(1) == 0)
    def _():
        h_ref[...] = _rms_rows(x_ref[...], g_ref[...]).astype(BF16)

    h = h_ref[...]
    a = _dot(h, wg_ref[...].astype(BF16))
    b = _dot(h, wu_ref[...].astype(BF16))
    o_ref[...] = (jax.nn.silu(a) * b).astype(o_ref.dtype)


def _ffn_up(x, g, wg, wu, layer, *, tm, tn):
    n, d = x.shape
    m = wg.shape[2]
    return pl.pallas_call(
        _ffn_up_kernel,
        grid=(n // tm, m // tn),
        in_specs=[pl.BlockSpec((tm, d), lambda i, j: (i, 0)),
                  pl.BlockSpec((None, 1, d), lambda i, j: (layer, 0, 0)),
                  pl.BlockSpec((None, d, tn), lambda i, j: (layer, 0, j)),
                  pl.BlockSpec((None, d, tn), lambda i, j: (layer, 0, j))],
        out_specs=pl.BlockSpec((tm, tn), lambda i, j: (i, j)),
        out_shape=jax.ShapeDtypeStruct((n, m), BF16),
        scratch_shapes=[pltpu.VMEM((tm, d), BF16)],
        compiler_params=_cparams(2),
        name="ffn_up",
    )(x, g, wg, wu)


def _matmul_residual_kernel(*refs):
    a_refs, (w_ref, r_ref, o_ref) = refs[:-3], refs[-3:]
    acc = r_ref[...]
    row = 0
    for a_ref in a_refs:
        k = a_ref.shape[1]
        acc = acc + _dot(a_ref[...], w_ref[row:row + k, :])
        row += k
    o_ref[...] = acc


def _matmul_residual(parts, w, r, layer, *, tm, tn):
    n = parts[0].shape[0]
    _, k, m = w.shape
    return pl.pallas_call(
        _matmul_residual_kernel,
        grid=(n // tm, m // tn),
        in_specs=[pl.BlockSpec((tm, a.shape[1]), lambda i, j: (i, 0)) for a in parts]
                 + [pl.BlockSpec((None, k, tn), lambda i, j: (layer, 0, j)),
                    pl.BlockSpec((tm, tn), lambda i, j: (i, j))],
        out_specs=pl.BlockSpec((tm, tn), lambda i, j: (i, j)),
        out_shape=jax.ShapeDtypeStruct((n, m), F32),
        compiler_params=_cparams(2),
        name="matmul_residual",
    )(*parts, w, r)


def _rope(x, t_ref, blocks=None):
    outs = []
    for b in range(x.shape[1] // LANES):
        xb = x[:, b * LANES:(b + 1) * LANES]
        if blocks is None or b in blocks:
            xb = xb * t_ref[0] + pltpu.roll(xb, LANES // 2, 1) * t_ref[1]
        outs.append(xb)
    return outs[0] if len(outs) == 1 else jnp.concatenate(outs, axis=1)


def _rms_lane_groups(x, width, count):
    outs = []
    for h in range(x.shape[1] // width):
        xh = x[:, h * width:(h + 1) * width]
        ms = jnp.sum(xh * xh, axis=-1, keepdims=True) * (1.0 / count)
        outs.append(xh * lax.rsqrt(ms + EPS))
    return outs[0] if len(outs) == 1 else jnp.concatenate(outs, axis=1)


def _rms_sub64(x, ones_ref):
    x2 = x * x
    hi = x2.astype(BF16)
    lo = (x2 - hi.astype(F32)).astype(BF16)
    ms = (_dot(hi, ones_ref[...]) + _dot(lo, ones_ref[...])) * (1.0 / DIFF_SUB_DIM)
    return x * lax.rsqrt(ms + EPS)


def _store_vext(ref, v):
    ones = jnp.ones((v.shape[0], LANES), BF16)
    for h in range(GROUP_HEADS):
        ref[:, h * VEXT:h * VEXT + LANES] = v[:, h * LANES:(h + 1) * LANES].astype(BF16)
        ref[:, h * VEXT + LANES:(h + 1) * VEXT] = ones


def _prep_kernel(y_ref, ta_ref, tb_ref, td_ref, ones_ref, g_ref, wq_ref, wkv_ref,
                 qa_ref, ka_ref, va_ref, qb_ref, kb_ref, vb_ref, qc_ref, kc_ref, vc_ref,
                 qd_ref, kd_ref, vd_ref, iq_ref, ik_ref, iw_ref):
    def sec(name, width):
        return y_ref[:, _OFF[name]:_OFF[name] + width]

    def gain(name, width):
        row = _GAIN_ROWS.index(name)
        return g_ref[row:row + 1, :width]

    tm = y_ref.shape[0]
    lane = lax.broadcasted_iota(jnp.int32, (tm, LANES), 1)
    first_half = (lane % (LANES // 2)) < DIFF_SUB_DIM // 2

    qa = _rope(_rms_sub64(sec("a_q", 512), ones_ref) * gain("a_q", 512), ta_ref) * (DIFF_SUB_DIM ** -0.5 * LOG2E)
    for h in range(GROUP_HEADS):
        qh = qa[:, h * LANES:(h + 1) * LANES]
        qa_ref[:, (2 * h) * LANES:(2 * h + 1) * LANES] = jnp.where(first_half, qh, 0.0).astype(BF16)
        qa_ref[:, (2 * h + 1) * LANES:(2 * h + 2) * LANES] = jnp.where(first_half, 0.0, qh).astype(BF16)
    ka_ref[...] = _rope(_rms_sub64(sec("a_k", 512), ones_ref) * gain("a_k", 512), ta_ref).astype(BF16)
    _store_vext(va_ref, sec("a_v", 512))

    cq = sec("b_cq", 512)
    cq = cq * lax.rsqrt(jnp.sum(cq * cq, axis=-1, keepdims=True) * (1.0 / MLA_Q_RANK) + EPS) * gain("b_cq", 512)
    qb = _dot(cq.astype(BF16), wq_ref[...])
    qb = _rms_lane_groups(qb, MLA_PAD_DIM, MLA_QK_DIM) * gain("b_q", 1024)
    ckv = _rms_lane_groups(sec("b_ckv", 128), 128, MLA_KV_RANK) * gain("b_ckv", 128)
    kv = _dot(ckv.astype(BF16), wkv_ref[...])
    kr = sec("b_kr", 128)
    kr_ss = jnp.sum(kr * kr, axis=-1, keepdims=True)
    for h in range(GROUP_HEADS):
        lo, hi = h * MLA_PAD_DIM, (h + 1) * MLA_PAD_DIM
        qb_ref[:, lo:hi] = (_rope(qb[:, lo:hi], tb_ref, blocks=(1,)) * (MLA_QK_DIM ** -0.5 * LOG2E)).astype(BF16)
        kn = kv[:, h * LANES:(h + 1) * LANES]
        ms = (jnp.sum(kn * kn, axis=-1, keepdims=True) + kr_ss) * (1.0 / MLA_QK_DIM)
        kh = jnp.concatenate([kn, kr], axis=1) * lax.rsqrt(ms + EPS) * gain("b_k", MLA_PAD_DIM)
        kb_ref[:, lo:hi] = _rope(kh, tb_ref, blocks=(1,)).astype(BF16)
    _store_vext(vb_ref, kv[:, GROUP_WIDTH:])

    qc_ref[...] = (sec("c_q", 512) * (HEAD_DIM ** -0.5 * LOG2E)).astype(BF16)
    kc_ref[...] = sec("c_k", 512).astype(BF16)
    vc_ref[...] = sec("c_v", 512).astype(BF16)

    qd = _rope(_rms_lane_groups(sec("d_q", 512), HEAD_DIM, HEAD_DIM) * gain("d_q", 512), td_ref)
    qd_ref[...] = (qd * (HEAD_DIM ** -0.5 * LOG2E)).astype(BF16)
    kd_ref[...] = _rope(_rms_lane_groups(sec("d_k", 512), HEAD_DIM, HEAD_DIM) * gain("d_k", 512), td_ref).astype(BF16)
    _store_vext(vd_ref, sec("d_v", 512))
    for half in range(2):
        iq = y_ref[:, _OFF["d_iq"] + half * 512:_OFF["d_iq"] + (half + 1) * 512]
        iq_ref[:, half * 512:(half + 1) * 512] = (_rope(iq, ta_ref) * (IDX_DIM ** -0.5)).astype(BF16)
    ik = _rope(_rms_lane_groups(sec("d_ik", 128), 128, IDX_DIM) * gain("d_ik", 128), ta_ref)
    ik_ref[:, :LANES] = ik.astype(BF16)
    ik_ref[:, LANES:] = pltpu.roll(ik, IDX_DIM // 2, 1).astype(BF16)
    iw_ref[...] = sec("d_iw", 128) * (IDX_HEADS ** -0.5)


def _prep(y, seq, tabs, ones64, gains, wq, wkv, layer, *, tm):
    n = y.shape[0]
    nblk_seq = seq // tm
    ta, tb, td = tabs

    def rows(width):
        return pl.BlockSpec((tm, width), lambda i: (i, 0))

    def table():
        return pl.BlockSpec((2, tm, LANES), lambda i: (0, i % nblk_seq, 0))

    def whole(a):
        return pl.BlockSpec(a.shape, lambda i: (0,) * a.ndim)

    def of_layer(a):
        return pl.BlockSpec((None,) + a.shape[1:], lambda i: (layer,) + (0,) * (a.ndim - 1))

    widths = [1024, 512, 1024, 1024, 1024, 1024, 512, 512, 512, 512, 512, 1024, 1024, 256]
    out_shape = [jax.ShapeDtypeStruct((n, w), BF16) for w in widths] + [jax.ShapeDtypeStruct((n, LANES), F32)]
    out_specs = [rows(w) for w in widths] + [rows(LANES)]
    return pl.pallas_call(
        _prep_kernel,
        grid=(n // tm,),
        in_specs=[rows(Y_WIDTH), table(), table(), table(), whole(ones64), of_layer(gains), of_layer(wq), of_layer(wkv)],
        out_specs=out_specs,
        out_shape=out_shape,
        compiler_params=_cparams(1),
        name="prep",
    )(y, ta, tb, td, ones64, gains, wq, wkv)


def _softmax_chunk(s, vext, m_ref, acc_ref, idx, running_max):
    if not running_max:
        acc_ref[idx] += _dot(jnp.exp2(s).astype(BF16), vext)
        return
    m_prev = m_ref[idx]
    m_new = jnp.maximum(m_prev, jnp.max(s, axis=1, keepdims=True))
    p = jnp.exp2(s - jnp.tile(m_new, (1, s.shape[1] // LANES)))
    alpha = jnp.exp2(m_prev - m_new)
    acc_ref[idx] = jnp.tile(alpha, (1, VEXT // LANES)) * acc_ref[idx] + _dot(p.astype(BF16), vext)
    m_ref[idx] = m_new


def _init_softmax(m_ref, acc_ref):
    m_ref[...] = jnp.full(m_ref.shape, NEG, F32)
    acc_ref[...] = jnp.zeros(acc_ref.shape, F32)


def _with_score_bound(bound, attend):
    @pl.when(bound <= MAX_FIXED_SHIFT)
    def _():
        attend(bound)

    @pl.when(jnp.logical_not(bound <= MAX_FIXED_SHIFT))
    def _():
        attend(None)


def _normalised(acc_ref, idx):
    acc = acc_ref[idx]
    return acc[:, :LANES] / acc[:, LANES:]


def _causal_mask(row0, col0, tq, tkc, strict=False):
    rows = row0 + lax.broadcasted_iota(jnp.int32, (tq, tkc), 0)
    cols = col0 + lax.broadcasted_iota(jnp.int32, (tq, tkc), 1)
    return cols < rows if strict else cols <= rows


def _attn_diff_kernel(bound_ref, lam_ref, sub_ref, q_ref, k_ref, v_ref, o_ref, m_ref, acc_ref, *, tq, layer, lam_init):
    qi = pl.program_id(1)
    _init_softmax(m_ref, acc_ref)

    def attend(shift):
        def chunk(off, mask):
            for mp in range(2 * GROUP_HEADS):
                h = mp // 2
                q = q_ref[0, :, mp * LANES:(mp + 1) * LANES]
                k = k_ref[0, pl.ds(off, tq), h * LANES:(h + 1) * LANES]
                v = v_ref[0, pl.ds(off, tq), h * VEXT:(h + 1) * VEXT]
                s = _dot_nt(q, k)
                if shift is not None:
                    s = s - shift
                if mask is not None:
                    s = jnp.where(mask, s, NEG)
                _softmax_chunk(s, v, m_ref, acc_ref, mp, running_max=shift is None)

        def full_chunk(c, carry):
            chunk(pl.multiple_of(c * tq, tq), None)
            return carry

        lax.fori_loop(0, qi, full_chunk, 0)
        chunk(pl.multiple_of(qi * tq, tq), _causal_mask(0, 0, tq, tq))

    _with_score_bound(bound_ref[layer, _BOUND_COLS.index("diff")], attend)

    lv = lam_ref[...]
    lam = (jnp.exp(jnp.sum(lv[0:1] * lv[1:2], axis=-1, keepdims=True))
           - jnp.exp(jnp.sum(lv[2:3] * lv[3:4], axis=-1, keepdims=True)) + lam_init)
    for h in range(GROUP_HEADS):
        o = _normalised(acc_ref, 2 * h) - lam * _normalised(acc_ref, 2 * h + 1)
        o = _rms_rows(o, sub_ref[...]) * (1.0 - lam_init)
        o_ref[0, :, h * LANES:(h + 1) * LANES] = o.astype(o_ref.dtype)


def _attn_mla_kernel(bound_ref, q_ref, k_ref, v_ref, o_ref, m_ref, acc_ref, *, tq, layer):
    qi = pl.program_id(1)
    _init_softmax(m_ref, acc_ref)

    def attend(shift):
        def chunk(off, mask):
            for h in range(GROUP_HEADS):
                q = q_ref[0, :, h * MLA_PAD_DIM:(h + 1) * MLA_PAD_DIM]
                k = k_ref[0, pl.ds(off, tq), h * MLA_PAD_DIM:(h + 1) * MLA_PAD_DIM]
                v = v_ref[0, pl.ds(off, tq), h * VEXT:(h + 1) * VEXT]
                s = _dot_nt(q, k)
                if shift is not None:
                    s = s - shift
                if mask is not None:
                    s = jnp.where(mask, s, NEG)
                _softmax_chunk(s, v, m_ref, acc_ref, h, running_max=shift is None)

        def full_chunk(c, carry):
            chunk(pl.multiple_of(c * tq, tq), None)
            return carry

        lax.fori_loop(0, qi, full_chunk, 0)
        chunk(pl.multiple_of(qi * tq, tq), _causal_mask(0, 0, tq, tq))

    _with_score_bound(bound_ref[layer, _BOUND_COLS.index("mla")], attend)
    for h in range(GROUP_HEADS):
        o_ref[0, :, h * LANES:(h + 1) * LANES] = _normalised(acc_ref, h).astype(o_ref.dtype)


def _attn_stick_kernel(tri_ref, q_ref, k_ref, v_ref, o_ref, carry_ref, acc_ref, *, tq, tkc):
    qi = pl.program_id(1)
    carry_ref[...] = jnp.zeros(carry_ref.shape, F32)
    acc_ref[...] = jnp.zeros(acc_ref.shape, F32)

    def chunk(off, mask):
        tri = tri_ref[...]
        for h in range(GROUP_HEADS):
            q = q_ref[0, :, h * LANES:(h + 1) * LANES]
            k = k_ref[0, pl.ds(off, tkc), h * LANES:(h + 1) * LANES]
            v = v_ref[0, pl.ds(off, tkc), h * LANES:(h + 1) * LANES]
            z = _dot_nt(q, k)
            log_b = jnp.minimum(z, 0.0) - jnp.log2(1.0 + jnp.exp2(-jnp.abs(z)))
            log_1mb = log_b - z
            if mask is not None:
                log_1mb = jnp.where(mask, log_1mb, 0.0)
            hi = log_1mb.astype(BF16)
            lo = (log_1mb - hi.astype(F32)).astype(BF16)
            inner = _dot(hi, tri) + _dot(lo, tri)
            carry = carry_ref[h]
            a = jnp.exp2(log_b + inner + jnp.tile(carry, (1, tkc // LANES)))
            if mask is not None:
                a = jnp.where(mask, a, 0.0)
            acc_ref[h] += _dot(a.astype(BF16), v)
            carry_ref[h] = carry + (inner[:, 0:1] + log_1mb[:, 0:1])

    n_diag = tq // tkc
    for j in range(n_diag):
        col0 = (n_diag - 1 - j) * tkc
        chunk(pl.multiple_of(qi * tq + col0, tkc), _causal_mask(0, col0, tq, tkc, strict=True))

    n_full = qi * n_diag

    def full_chunk(state):
        c, _ = state
        chunk(pl.multiple_of((n_full - 1 - c) * tkc, tkc), None)
        return c + 1, (jnp.max(carry_ref[...]) > STICK_DEAD_LOG2).astype(jnp.int32)

    lax.while_loop(lambda state: jnp.logical_and(state[0] < n_full, state[1] > 0), full_chunk,
                   (jnp.int32(0), jnp.int32(1)))
    for h in range(GROUP_HEADS):
        o_ref[0, :, h * LANES:(h + 1) * LANES] = acc_ref[h].astype(o_ref.dtype)


def _resident_attention(kernel, q, k, v, extra, layer_extra, layer, *, tq, scratch, name, score_bounds=None):
    b, s, _ = q.shape
    q_map = lambda bi, qi: (bi, qi, 0)
    all_map = lambda bi, qi: (bi, 0, 0)
    extra_specs = [pl.BlockSpec(e.shape, lambda bi, qi, nd=e.ndim: (0,) * nd) for e in extra]
    extra_specs += [pl.BlockSpec((None,) + e.shape[1:], lambda bi, qi, nd=e.ndim: (layer,) + (0,) * (nd - 1))
                    for e in layer_extra]
    extra = list(extra) + list(layer_extra)
    if score_bounds is not None:
        extra = [score_bounds] + extra
        extra_specs = [pl.BlockSpec(memory_space=pltpu.SMEM)] + extra_specs
    return pl.pallas_call(
        kernel,
        grid=(b, s // tq),
        in_specs=extra_specs + [pl.BlockSpec((1, tq, q.shape[2]), q_map),
                                pl.BlockSpec((1, s, k.shape[2]), all_map),
                                pl.BlockSpec((1, s, v.shape[2]), all_map)],
        out_specs=pl.BlockSpec((1, tq, GROUP_WIDTH), q_map),
        out_shape=jax.ShapeDtypeStruct((b, s, GROUP_WIDTH), BF16),
        scratch_shapes=scratch,
        compiler_params=_cparams(2),
        name=name,
    )(*extra, q, k, v)


def _attn_dsa_kernel(bound_ref, q_ref, k_ref, v_ref, iq_ref, ik_ref, iw_ref, earlier_ref, o_ref, key_ref, half_ref,
                     m_ref, acc_ref, *, tq, topk, layer):
    qi = pl.program_id(1)
    nkc = qi + 1
    iw_t = iw_ref[0].T

    def score_chunk(c, mask):
        off = pl.multiple_of(c * tq, tq)
        ikc = ik_ref[0, pl.ds(off, tq), :]
        isc = jnp.zeros((tq, tq), F32)
        for hp in range(IDX_HEADS // 2):
            iqp = iq_ref[0, :, hp * LANES:(hp + 1) * LANES]
            for e in range(2):
                hh = 2 * hp + e
                s = _dot_nt(ikc[:, e * LANES:(e + 1) * LANES], iqp)
                isc = isc + jnp.maximum(s, 0.0) * iw_t[hh:hh + 1, :]
        bits = lax.bitcast_convert_type(isc, jnp.int32)
        key = jnp.where(bits < 0, bits ^ jnp.int32(0x7FFFFFFF), bits)
        if mask is not None:
            key = jnp.where(mask, key, INT_MIN)
        key_ref[c] = key
        half_ref[c] = (key >> 16).astype(jnp.int16)

    def full_score_chunk(c, carry):
        score_chunk(c, None)
        return carry

    lax.fori_loop(0, qi, full_score_chunk, 0)
    key_pos = lax.broadcasted_iota(jnp.int32, (tq, tq), 0)
    query_pos = lax.broadcasted_iota(jnp.int32, (tq, tq), 1)
    score_chunk(qi, key_pos <= query_pos)

    one16, zero16, lowest16 = jnp.int16(1), jnp.int16(0), jnp.int16(-32768)

    @pl.when(nkc % COUNT_UNROLL == 1)
    def _():
        half_ref[nkc] = jnp.full((tq, tq), lowest16, jnp.int16)

    def row_count(bound, strict):
        def body(t, part):
            for u in range(COUNT_UNROLL):
                for r in range(tq // COUNT_ACC_ROWS):
                    half = half_ref[COUNT_UNROLL * t + u, r * COUNT_ACC_ROWS:(r + 1) * COUNT_ACC_ROWS, :]
                    part = part + jnp.where(half > bound if strict else half >= bound, one16, zero16)
            return part

        part = lax.fori_loop(0, (nkc + COUNT_UNROLL - 1) // COUNT_UNROLL, body,
                             jnp.zeros((COUNT_ACC_ROWS, tq), jnp.int16))
        return jnp.sum(part.astype(F32), axis=0, keepdims=True)

    def search16(rank, cnt_init):
        def bit(i, carry):
            thr_u, cnt_thr = carry
            cand_u = thr_u | lax.shift_left(jnp.int32(1), 15 - i)
            cnt = row_count((cand_u - 32768).astype(jnp.int16), strict=False)
            take = cnt >= rank
            return jnp.where(take, cand_u, thr_u), jnp.where(take, cnt, cnt_thr)

        return lax.fori_loop(0, 16, bit, (jnp.zeros((1, tq), jnp.int32), cnt_init))

    k_f = jnp.full((1, tq), float(topk), F32)
    hi_u, cnt_hi = search16(k_f, jnp.zeros((1, tq), F32))
    hi_thr = (hi_u - 32768).astype(jnp.int16)
    cnt_gt = row_count(hi_thr, strict=True)

    def low_halves(c, carry):
        lo = ((key_ref[c] & 0xFFFF) - 32768).astype(jnp.int16)
        half_ref[c] = jnp.where(half_ref[c] == hi_thr, lo, lowest16)
        return carry

    lax.fori_loop(0, nkc, low_halves, 0)
    lo_u, cnt_lo = search16(k_f - cnt_gt, cnt_hi - cnt_gt)
    thr_raw = ((hi_u << 16) | lo_u) ^ jnp.int32(INT_MIN)
    short_row = thr_raw == jnp.int32(INT_MIN)
    thr = jnp.maximum(thr_raw, jnp.int32(INT_MIN + 1))
    cnt_ge = cnt_gt + cnt_lo
    tied = jnp.logical_and(jnp.logical_not(short_row), cnt_ge > k_f)
    any_tied = jnp.max(jnp.where(tied, 1.0, 0.0)) > 0.0

    _init_softmax(m_ref, acc_ref)

    def attend(c, selected, shift):
        bias = jnp.where(selected, 0.0 if shift is None else -shift, NEG).T
        off = pl.multiple_of(c * tq, tq)
        for h in range(GROUP_HEADS):
            q = q_ref[0, :, h * LANES:(h + 1) * LANES]
            k = k_ref[0, pl.ds(off, tq), h * LANES:(h + 1) * LANES]
            v = v_ref[0, pl.ds(off, tq), h * VEXT:(h + 1) * VEXT]
            _softmax_chunk(_dot_nt(q, k) + bias, v, m_ref, acc_ref, h, running_max=shift is None)

    def attend_untied(shift):
        def attend_chunk(c, carry):
            attend(c, key_ref[c] >= thr, shift)
            return carry

        lax.fori_loop(0, nkc, attend_chunk, 0)

    @pl.when(jnp.logical_not(any_tied))
    def _():
        _with_score_bound(bound_ref[layer, _BOUND_COLS.index("dsa")], attend_untied)

    @pl.when(any_tied)
    def _():
        def eq_chunk(c, n):
            return n + jnp.sum(jnp.where(key_ref[c] == thr, 1.0, 0.0), axis=0, keepdims=True)

        n_eq = lax.fori_loop(0, nkc, eq_chunk, jnp.zeros((1, tq), F32))
        keep = jnp.where(short_row, 0.0, k_f - (cnt_ge - n_eq))

        def attend_chunk(c, seen):
            key = key_ref[c]
            eq = key == thr
            eq_f = jnp.where(eq, 1.0, 0.0)
            rank = seen + _dot(earlier_ref[...], eq_f.astype(BF16))
            attend(c, jnp.logical_or(key > thr, jnp.logical_and(eq, rank < keep)), None)
            return seen + jnp.sum(eq_f, axis=0, keepdims=True)

        lax.fori_loop(0, nkc, attend_chunk, jnp.zeros((1, tq), F32))

    for h in range(GROUP_HEADS):
        o_ref[0, :, h * LANES:(h + 1) * LANES] = _normalised(acc_ref, h).astype(o_ref.dtype)


def _attn_dsa(q, k, v, iq, ik, iw, score_bounds, layer, *, tq):
    b, s, _ = q.shape
    topk = min(TOPK_MAX, s // 4)
    earlier = np.asarray(np.arange(tq)[:, None] > np.arange(tq)[None, :], dtype=BF16)
    q_map = lambda bi, qi: (bi, qi, 0)
    all_map = lambda bi, qi: (bi, 0, 0)
    return pl.pallas_call(
        functools.partial(_attn_dsa_kernel, tq=tq, topk=topk, layer=layer),
        grid=(b, s // tq),
        in_specs=[pl.BlockSpec(memory_space=pltpu.SMEM),
                  pl.BlockSpec((1, tq, GROUP_WIDTH), q_map),
                  pl.BlockSpec((1, s, GROUP_WIDTH), all_map),
                  pl.BlockSpec((1, s, GROUP_HEADS * VEXT), all_map),
                  pl.BlockSpec((1, tq, IDX_HEADS * IDX_DIM), q_map),
                  pl.BlockSpec((1, s, 2 * LANES), all_map),
                  pl.BlockSpec((1, tq, LANES), q_map),
                  pl.BlockSpec((tq, tq), lambda bi, qi: (0, 0))],
        out_specs=pl.BlockSpec((1, tq, GROUP_WIDTH), q_map),
        out_shape=jax.ShapeDtypeStruct((b, s, GROUP_WIDTH), BF16),
        scratch_shapes=[pltpu.VMEM((s // tq, tq, tq), jnp.int32),
                        pltpu.VMEM((-(-(s // tq) // COUNT_UNROLL) * COUNT_UNROLL, tq, tq), jnp.int16),
                        pltpu.VMEM((GROUP_HEADS, tq, LANES), F32),
                        pltpu.VMEM((GROUP_HEADS, tq, VEXT), F32)],
        compiler_params=_cparams(2),
        name="attn_dsa",
    )(score_bounds, q, k, v, iq, ik, iw, earlier)


def _rope_tables(seq, period, half):
    inv = np.float32(ROPE_THETA) ** (-np.arange(half, dtype=np.float32) / np.float32(half))
    lane = np.arange(LANES)
    idx = lane % period
    active = idx < half
    ang = np.arange(seq, dtype=np.float32)[:, None] * inv[np.minimum(idx, half - 1)][None, :]
    cos, sin = np.cos(ang), np.sin(ang)
    return np.stack([np.where(active, cos, 1.0), np.where(active, np.where(lane < LANES // 2, -sin, sin), 0.0)]
                    ).astype(np.float32)


def _half_split_layout(dims, half, start=0):
    first = list(range(half)) + list(range(2 * half, 2 * half + (dims - 2 * half) // 2))
    second = list(range(half, 2 * half)) + list(range(2 * half + (dims - 2 * half) // 2, dims))
    lanes = [-1] * LANES
    lanes[start:start + len(first)] = first
    lanes[start + LANES // 2:start + LANES // 2 + len(second)] = second
    return lanes


_HEAD_LAYOUT = _half_split_layout(HEAD_DIM, HEAD_DIM // 8)
_SUB_LAYOUT = _half_split_layout(DIFF_SUB_DIM, DIFF_SUB_DIM // 8)
_PAIR_LAYOUT = [a if a >= 0 else (DIFF_SUB_DIM + b if b >= 0 else -1)
                for a, b in zip(_SUB_LAYOUT, _half_split_layout(DIFF_SUB_DIM, DIFF_SUB_DIM // 8, start=32))]
_ROPE64_LAYOUT = _half_split_layout(MLA_ROPE_DIM, MLA_ROPE_DIM // 2)


def _take_cols(a, layout):
    src = max(layout) + 1
    place = np.zeros((src, LANES), np.float32)
    for lane, i in enumerate(layout):
        if i >= 0:
            place[i, lane] = 1.0
    blocks = a.reshape(a.shape[:-1] + (-1, src))
    precision = lax.Precision.HIGHEST if a.dtype == F32 else None
    out = jnp.einsum("...k,kj->...j", blocks, jnp.asarray(place, a.dtype), precision=precision,
                     preferred_element_type=a.dtype)
    return out.reshape(a.shape[:-1] + (-1,))


def _pad_cols(a, width):
    return jnp.pad(a, ((0, 0),) * (a.ndim - 1) + ((0, width - a.shape[-1]),))


def _relayout_w_in(w):
    sizes = (512,) * 3 + (MLA_Q_RANK, MLA_KV_RANK, MLA_ROPE_DIM) + (512,) * 6 + (IDX_HEADS * IDX_DIM, IDX_DIM, IDX_HEADS)
    names = ("a_q", "a_k", "a_v", "b_cq", "b_ckv", "b_kr", "c_q", "c_k", "c_v", "d_q", "d_k", "d_v", "d_iq", "d_ik", "d_iw")
    w = w.astype(BF16)
    parts, start = {}, 0
    for nme, sz in zip(names, sizes):
        parts[nme] = w[..., start:start + sz]
        start += sz
    for nme in ("a_q", "a_k", "d_iq"):
        parts[nme] = _take_cols(parts[nme], _PAIR_LAYOUT)
    for nme in ("d_q", "d_k"):
        parts[nme] = _take_cols(parts[nme], _HEAD_LAYOUT)
    parts["d_ik"] = _take_cols(parts["d_ik"], _SUB_LAYOUT)
    parts["b_kr"] = _take_cols(parts["b_kr"], _ROPE64_LAYOUT)
    order = sorted(_OFF, key=_OFF.get)
    ends = [_OFF[nme] for nme in order[1:]] + [Y_WIDTH]
    return jnp.concatenate([_pad_cols(parts[nme], end - _OFF[nme]) for nme, end in zip(order, ends)], axis=-1)


def _relayout_mla_q(w):
    lead = w.shape[:-1]
    w = w.reshape(lead + (-1, MLA_QK_DIM))
    rope = _take_cols(w[..., MLA_NOPE_DIM:], _ROPE64_LAYOUT)
    return jnp.concatenate([w[..., :MLA_NOPE_DIM], rope], axis=-1).reshape(lead + (-1,))


_BOUND_COLS = ("diff", "mla", "dsa", "unused")
_GAIN_ROWS = ("a_q", "a_k", "b_cq", "b_ckv", "b_q", "b_k", "d_q", "d_k", "d_ik")


def _score_bounds(diff_qk_norm, mla_qk_norm, dsa_qk_norm):
    def bound(dim, qk_norm):
        peak = jnp.max(jnp.abs(qk_norm), axis=-1)
        return (dim ** 0.5 * LOG2E * SCORE_BOUND_MARGIN) * peak[:, 0] * peak[:, 1]

    cols = [bound(DIFF_SUB_DIM, diff_qk_norm), bound(MLA_QK_DIM, mla_qk_norm), bound(HEAD_DIM, dsa_qk_norm)]
    return jnp.stack(cols + [jnp.zeros_like(cols[0])], axis=1).astype(F32)


def _packed_gains(diff_qk_norm, mla_q_a_norm, mla_kv_a_norm, mla_qk_norm, dsa_qk_norm, idx_k_norm):
    def tiled(g, reps):
        return jnp.tile(g, (1, reps))

    rows = dict(
        a_q=_take_cols(tiled(diff_qk_norm[:, 0], 8), _PAIR_LAYOUT),
        a_k=_take_cols(tiled(diff_qk_norm[:, 1], 8), _PAIR_LAYOUT),
        b_cq=mla_q_a_norm, b_ckv=mla_kv_a_norm,
        b_q=_relayout_mla_q(tiled(mla_qk_norm[:, 0], GROUP_HEADS)),
        b_k=_relayout_mla_q(mla_qk_norm[:, 1]),
        d_q=_take_cols(tiled(dsa_qk_norm[:, 0], 4), _HEAD_LAYOUT),
        d_k=_take_cols(tiled(dsa_qk_norm[:, 1], 4), _HEAD_LAYOUT),
        d_ik=_take_cols(idx_k_norm, _SUB_LAYOUT))
    packed = jnp.stack([_pad_cols(rows[nme], 1024) for nme in _GAIN_ROWS], axis=1)
    return jnp.pad(packed, ((0, 0), (0, 16 - len(_GAIN_ROWS)), (0, 0)))


def kernel(x, attn_norm, w_in, diff_qk_norm, diff_lambda, diff_subln, mla_q_a_norm, mla_wq_b, mla_kv_a_norm,
           mla_wkv_b, mla_qk_norm, dsa_qk_norm, idx_k_norm, w_o, ffn_norm, w_gate, w_up, w_down):
    b, s, d = x.shape
    n = b * s
    depth = w_in.shape[0]
    tm = min(1024, n)
    tq = min(512, s)
    tabs = (_rope_tables(s, LANES // 4, DIFF_SUB_DIM // 8),
            _rope_tables(s, LANES // 2, MLA_ROPE_DIM // 2),
            _rope_tables(s, LANES // 2, HEAD_DIM // 8))
    sub_head = np.arange(GROUP_WIDTH) // LANES * 2 + (np.arange(GROUP_WIDTH) % (LANES // 2)) // (LANES // 4)
    ones64 = np.asarray(sub_head[:, None] == sub_head[None, :], dtype=BF16)
    tk_stick = min(256, s)
    tri = np.asarray(np.arange(tk_stick)[:, None] > np.arange(tk_stick)[None, :], dtype=BF16)

    w_in_r = _relayout_w_in(w_in)
    wq = jnp.pad(_relayout_mla_q(mla_wq_b.astype(BF16)), ((0, 0), (0, 512 - MLA_Q_RANK), (0, 0)))
    wkv = mla_wkv_b.reshape(depth, MLA_KV_RANK, GROUP_HEADS, 2, HEAD_DIM).transpose(0, 1, 3, 2, 4)
    wkv = wkv.reshape(depth, MLA_KV_RANK, 2 * GROUP_WIDTH).astype(BF16)
    gains = _packed_gains(diff_qk_norm, mla_q_a_norm, mla_kv_a_norm, mla_qk_norm, dsa_qk_norm, idx_k_norm)
    bounds = _score_bounds(diff_qk_norm, mla_qk_norm, dsa_qk_norm)
    attn_gain, ffn_gain = attn_norm.reshape(depth, 1, d), ffn_norm.reshape(depth, 1, d)
    subln = diff_subln.reshape(depth, 1, HEAD_DIM)
    w_o_b, w_down_b = w_o.astype(BF16), w_down.astype(BF16)

    xf = x.reshape(n, d)
    for l in range(depth):
        lam_init = 0.8 - 0.6 * math.exp(-0.3 * l)
        y = _rms_matmul(xf, attn_gain, w_in_r, l, tm=tm, tn=512, out_dtype=F32)
        (qa, ka, va, qb, kb, vb, qc, kc, vc, qd, kd, vd, iq, ik, iw) = [
            a.reshape(b, s, a.shape[1]) for a in _prep(y, s, tabs, ones64, gains, wq, wkv, l, tm=min(256, s))]

        o_a = _resident_attention(
            functools.partial(_attn_diff_kernel, tq=tq, layer=l, lam_init=lam_init), qa, ka, va,
            [], [diff_lambda, subln], l, tq=tq,
            scratch=[pltpu.VMEM((8, tq, LANES), F32), pltpu.VMEM((8, tq, VEXT), F32)], name="attn_diff",
            score_bounds=bounds)
        o_b = _resident_attention(
            functools.partial(_attn_mla_kernel, tq=tq, layer=l), qb, kb, vb, [], [], l, tq=tq,
            scratch=[pltpu.VMEM((4, tq, LANES), F32), pltpu.VMEM((4, tq, VEXT), F32)], name="attn_mla",
            score_bounds=bounds)
        o_c = _resident_attention(
            functools.partial(_attn_stick_kernel, tq=tq, tkc=tk_stick), qc, kc, vc, [tri], [], l, tq=tq,
            scratch=[pltpu.VMEM((4, tq, LANES), F32), pltpu.VMEM((4, tq, LANES), F32)], name="attn_stick")
        o_d = _attn_dsa(qd, kd, vd, iq, ik, iw, bounds, l, tq=tq)

        mixed = [o.reshape(n, GROUP_WIDTH) for o in (o_a, o_b, o_c, o_d)]
        xf = _matmul_residual(mixed, w_o_b, xf, l, tm=min(512, n), tn=d)
        act = _ffn_up(xf, ffn_gain, w_gate, w_up, l, tm=tm, tn=512)
        xf = _matmul_residual([act], w_down_b, xf, l, tm=tm, tn=512)
    return xf.reshape(b, s, d)
```

```python
import functools
import math

import jax
import jax.numpy as jnp
import numpy as np
from jax import lax
from jax.experimental import pallas as pl
from jax.experimental.pallas import tpu as pltpu

F32 = jnp.float32
BF16 = jnp.bfloat16

HEAD_DIM = 128
GROUP_HEADS = 4
GROUP_WIDTH = GROUP_HEADS * HEAD_DIM
ROPE_THETA = 500000.0
EPS = 1e-6
DIFF_SUB_DIM = 64
MLA_Q_RANK = 448
MLA_KV_RANK = 128
MLA_NOPE_DIM = 128
MLA_ROPE_DIM = 64
MLA_QK_DIM = MLA_NOPE_DIM + MLA_ROPE_DIM
MLA_PAD_DIM = 256
IDX_HEADS = 16
IDX_DIM = 64
TOPK_MAX = 256

LANES = 128
VEXT = 2 * LANES
Y_WIDTH = 13 * GROUP_WIDTH
NEG = -1e30
INT_MIN = -2147483648
LOG2E = 1.4426950408889634
COUNT_UNROLL = 2
COUNT_ACC_ROWS = 64
MAX_FIXED_SHIFT = 56.0
SCORE_BOUND_MARGIN = 1.02
STICK_DEAD_LOG2 = -160.0
VMEM_LIMIT = 56 * 1024 * 1024

_OFF = dict(a_q=0, a_k=512, a_v=1024, b_cq=1536, b_ckv=2048, b_kr=2176, d_ik=2304, d_iw=2432,
            c_q=2560, c_k=3072, c_v=3584, d_q=4096, d_k=4608, d_v=5120, d_iq=5632)


def _cparams(n_axes):
    return pltpu.CompilerParams(dimension_semantics=("arbitrary",) * n_axes, vmem_limit_bytes=VMEM_LIMIT)


def _dot(a, b):
    return jnp.dot(a, b, preferred_element_type=F32)


def _dot_nt(a, b):
    return lax.dot_general(a, b, (((1,), (1,)), ((), ())), preferred_element_type=F32)


def _rms_rows(x, g):
    ms = jnp.mean(x * x, axis=-1, keepdims=True)
    return x * lax.rsqrt(ms + EPS) * g


def _rms_matmul_kernel(x_ref, g_ref, w_ref, o_ref, h_ref):
    @pl.when(pl.program_id(1) == 0)
    def _():
        h_ref[...] = _rms_rows(x_ref[...], g_ref[...]).astype(BF16)

    o_ref[...] = _dot(h_ref[...], w_ref[...]).astype(o_ref.dtype)


def _rms_matmul(x, g, w, layer, *, tm, tn, out_dtype):
    n, d = x.shape
    m = w.shape[2]
    return pl.pallas_call(
        _rms_matmul_kernel,
        grid=(n // tm, m // tn),
        in_specs=[pl.BlockSpec((tm, d), lambda i, j: (i, 0)),
                  pl.BlockSpec((None, 1, d), lambda i, j: (layer, 0, 0)),
                  pl.BlockSpec((None, d, tn), lambda i, j: (layer, 0, j))],
        out_specs=pl.BlockSpec((tm, tn), lambda i, j: (i, j)),
        out_shape=jax.ShapeDtypeStruct((n, m), out_dtype),
        scratch_shapes=[pltpu.VMEM((tm, d), BF16)],
        compiler_params=_cparams(2),
        name="rms_matmul",
    )(x, g, w)


def _ffn_up_kernel(x_ref, g_ref, wg_ref, wu_ref, o_ref, h_ref):
    @pl.when(pl.program_id(1) == 0)
    def _():
        h_ref[...] = _rms_rows(x_ref[...], g_ref[...]).astype(BF16)

    h = h_ref[...]
    a = _dot(h, wg_ref[...].astype(BF16))
    b = _dot(h, wu_ref[...].astype(BF16))
    o_ref[...] = (jax.nn.silu(a) * b).astype(o_ref.dtype)


def _ffn_up(x, g, wg, wu, layer, *, tm, tn):
    n, d = x.shape
    m = wg.shape[2]
    return pl.pallas_call(
        _ffn_up_kernel,
        grid=(n // tm, m // tn),
        in_specs=[pl.BlockSpec((tm, d), lambda i, j: (i, 0)),
                  pl.BlockSpec((None, 1, d), lambda i, j: (layer, 0, 0)),
                  pl.BlockSpec((None, d, tn), lambda i, j: (layer, 0, j)),
                  pl.BlockSpec((None, d, tn), lambda i, j: (layer, 0, j))],
        out_specs=pl.BlockSpec((tm, tn), lambda i, j: (i, j)),
        out_shape=jax.ShapeDtypeStruct((n, m), BF16),
        scratch_shapes=[pltpu.VMEM((tm, d), BF16)],
        compiler_params=_cparams(2),
        name="ffn_up",
    )(x, g, wg, wu)


def _matmul_residual_kernel(*refs):
    a_refs, (w_ref, r_ref, o_ref) = refs[:-3], refs[-3:]
    acc = r_ref[...]
    row = 0
    for a_ref in a_refs:
        k = a_ref.shape[1]
        acc = acc + _dot(a_ref[...], w_ref[row:row + k, :])
        row += k
    o_ref[...] = acc


def _matmul_residual(parts, w, r, layer, *, tm, tn):
    n = parts[0].shape[0]
    _, k, m = w.shape
    return pl.pallas_call(
        _matmul_residual_kernel,
        grid=(n // tm, m // tn),
        in_specs=[pl.BlockSpec((tm, a.shape[1]), lambda i, j: (i, 0)) for a in parts]
                 + [pl.BlockSpec((None, k, tn), lambda i, j: (layer, 0, j)),
                    pl.BlockSpec((tm, tn), lambda i, j: (i, j))],
        out_specs=pl.BlockSpec((tm, tn), lambda i, j: (i, j)),
        out_shape=jax.ShapeDtypeStruct((n, m), F32),
        compiler_params=_cparams(2),
        name="matmul_residual",
    )(*parts, w, r)


def _rope(x, t_ref, blocks=None):
    outs = []
    for b in range(x.shape[1] // LANES):
        xb = x[:, b * LANES:(b + 1) * LANES]
        if blocks is None or b in blocks:
            xb = xb * t_ref[0] + pltpu.roll(xb, LANES // 2, 1) * t_ref[1]
        outs.append(xb)
    return outs[0] if len(outs) == 1 else jnp.concatenate(outs, axis=1)


def _rms_lane_groups(x, width, count):
    outs = []
    for h in range(x.shape[1] // width):
        xh = x[:, h * width:(h + 1) * width]
        ms = jnp.sum(xh * xh, axis=-1, keepdims=True) * (1.0 / count)
        outs.append(xh * lax.rsqrt(ms + EPS))
    return outs[0] if len(outs) == 1 else jnp.concatenate(outs, axis=1)


def _rms_sub64(x, ones_ref):
    x2 = x * x
    hi = x2.astype(BF16)
    lo = (x2 - hi.astype(F32)).astype(BF16)
    ms = (_dot(hi, ones_ref[...]) + _dot(lo, ones_ref[...])) * (1.0 / DIFF_SUB_DIM)
    return x * lax.rsqrt(ms + EPS)


def _store_vext(ref, v):
    ones = jnp.ones((v.shape[0], LANES), BF16)
    for h in range(GROUP_HEADS):
        ref[:, h * VEXT:h * VEXT + LANES] = v[:, h * LANES:(h + 1) * LANES].astype(BF16)
        ref[:, h * VEXT + LANES:(h + 1) * VEXT] = ones


def _prep_kernel(y_ref, ta_ref, tb_ref, td_ref, ones_ref, g_ref, wq_ref, wkv_ref,
                 qa_ref, ka_ref, va_ref, qb_ref, kb_ref, vb_ref, qc_ref, kc_ref, vc_ref,
                 qd_ref, kd_ref, vd_ref, iq_ref, ik_ref, iw_ref):
    def sec(name, width):
        return y_ref[:, _OFF[name]:_OFF[name] + width]

    def gain(name, width):
        row = _GAIN_ROWS.index(name)
        return g_ref[row:row + 1, :width]

    tm = y_ref.shape[0]
    lane = lax.broadcasted_iota(jnp.int32, (tm, LANES), 1)
    first_half = (lane % (LANES // 2)) < DIFF_SUB_DIM // 2

    qa = _rope(_rms_sub64(sec("a_q", 512), ones_ref) * gain("a_q", 512), ta_ref) * (DIFF_SUB_DIM ** -0.5 * LOG2E)
    for h in range(GROUP_HEADS):
        qh = qa[:, h * LANES:(h + 1) * LANES]
        qa_ref[:, (2 * h) * LANES:(2 * h + 1) * LANES] = jnp.where(first_half, qh, 0.0).astype(BF16)
        qa_ref[:, (2 * h + 1) * LANES:(2 * h + 2) * LANES] = jnp.where(first_half, 0.0, qh).astype(BF16)
    ka_ref[...] = _rope(_rms_sub64(sec("a_k", 512), ones_ref) * gain("a_k", 512), ta_ref).astype(BF16)
    _store_vext(va_ref, sec("a_v", 512))

    cq = sec("b_cq", 512)
    cq = cq * lax.rsqrt(jnp.sum(cq * cq, axis=-1, keepdims=True) * (1.0 / MLA_Q_RANK) + EPS) * gain("b_cq", 512)
    qb = _dot(cq.astype(BF16), wq_ref[...])
    qb = _rms_lane_groups(qb, MLA_PAD_DIM, MLA_QK_DIM) * gain("b_q", 1024)
    ckv = _rms_lane_groups(sec("b_ckv", 128), 128, MLA_KV_RANK) * gain("b_ckv", 128)
    kv = _dot(ckv.astype(BF16), wkv_ref[...])
    kr = sec("b_kr", 128)
    kr_ss = jnp.sum(kr * kr, axis=-1, keepdims=True)
    for h in range(GROUP_HEADS):
        lo, hi = h * MLA_PAD_DIM, (h + 1) * MLA_PAD_DIM
        qb_ref[:, lo:hi] = (_rope(qb[:, lo:hi], tb_ref, blocks=(1,)) * (MLA_QK_DIM ** -0.5 * LOG2E)).astype(BF16)
        kn = kv[:, h * LANES:(h + 1) * LANES]
        ms = (jnp.sum(kn * kn, axis=-1, keepdims=True) + kr_ss) * (1.0 / MLA_QK_DIM)
        kh = jnp.concatenate([kn, kr], axis=1) * lax.rsqrt(ms + EPS) * gain("b_k", MLA_PAD_DIM)
        kb_ref[:, lo:hi] = _rope(kh, tb_ref, blocks=(1,)).astype(BF16)
    _store_vext(vb_ref, kv[:, GROUP_WIDTH:])

    qc_ref[...] = (sec("c_q", 512) * (HEAD_DIM ** -0.5 * LOG2E)).astype(BF16)
    kc_ref[...] = sec("c_k", 512).astype(BF16)
    vc_ref[...] = sec("c_v", 512).astype(BF16)

    qd = _rope(_rms_lane_groups(sec("d_q", 512), HEAD_DIM, HEAD_DIM) * gain("d_q", 512), td_ref)
    qd_ref[...] = (qd * (HEAD_DIM ** -0.5 * LOG2E)).astype(BF16)
    kd_ref[...] = _rope(_rms_lane_groups(sec("d_k", 512), HEAD_DIM, HEAD_DIM) * gain("d_k", 512), td_ref).astype(BF16)
    _store_vext(vd_ref, sec("d_v", 512))
    for half in range(2):
        iq = y_ref[:, _OFF["d_iq"] + half * 512:_OFF["d_iq"] + (half + 1) * 512]
        iq_ref[:, half * 512:(half + 1) * 512] = (_rope(iq, ta_ref) * (IDX_DIM ** -0.5)).astype(BF16)
    ik = _rope(_rms_lane_groups(sec("d_ik", 128), 128, IDX_DIM) * gain("d_ik", 128), ta_ref)
    ik_ref[:, :LANES] = ik.astype(BF16)
    ik_ref[:, LANES:] = pltpu.roll(ik, IDX_DIM // 2, 1).astype(BF16)
    iw_ref[...] = sec("d_iw", 128) * (IDX_HEADS ** -0.5)


def _prep(y, seq, tabs, ones64, gains, wq, wkv, layer, *, tm):
    n = y.shape[0]
    nblk_seq = seq // tm
    ta, tb, td = tabs

    def rows(width):
        return pl.BlockSpec((tm, width), lambda i: (i, 0))

    def table():
        return pl.BlockSpec((2, tm, LANES), lambda i: (0, i % nblk_seq, 0))

    def whole(a):
        return pl.BlockSpec(a.shape, lambda i: (0,) * a.ndim)

    def of_layer(a):
        return pl.BlockSpec((None,) + a.shape[1:], lambda i: (layer,) + (0,) * (a.ndim - 1))

    widths = [1024, 512, 1024, 1024, 1024, 1024, 512, 512, 512, 512, 512, 1024, 1024, 256]
    out_shape = [jax.ShapeDtypeStruct((n, w), BF16) for w in widths] + [jax.ShapeDtypeStruct((n, LANES), F32)]
    out_specs = [rows(w) for w in widths] + [rows(LANES)]
    return pl.pallas_call(
        _prep_kernel,
        grid=(n // tm,),
        in_specs=[rows(Y_WIDTH), table(), table(), table(), whole(ones64), of_layer(gains), of_layer(wq), of_layer(wkv)],
        out_specs=out_specs,
        out_shape=out_shape,
        compiler_params=_cparams(1),
        name="prep",
    )(y, ta, tb, td, ones64, gains, wq, wkv)


def _softmax_chunk(s, vext, m_ref, acc_ref, idx, running_max):
    if not running_max:
        acc_ref[idx] += _dot(jnp.exp2(s).astype(BF16), vext)
        return
    m_prev = m_ref[idx]
    m_new = jnp.maximum(m_prev, jnp.max(s, axis=1, keepdims=True))
    p = jnp.exp2(s - jnp.tile(m_new, (1, s.shape[1] // LANES)))
    alpha = jnp.exp2(m_prev - m_new)
    acc_ref[idx] = jnp.tile(alpha, (1, VEXT // LANES)) * acc_ref[idx] + _dot(p.astype(BF16), vext)
    m_ref[idx] = m_new


def _init_softmax(m_ref, acc_ref):
    m_ref[...] = jnp.full(m_ref.shape, NEG, F32)
    acc_ref[...] = jnp.zeros(acc_ref.shape, F32)


def _with_score_bound(bound, attend):
    @pl.when(bound <= MAX_FIXED_SHIFT)
    def _():
        attend(bound)

    @pl.when(jnp.logical_not(bound <= MAX_FIXED_SHIFT))
    def _():
        attend(None)


def _normalised(acc_ref, idx):
    acc = acc_ref[idx]
    return acc[:, :LANES] / acc[:, LANES:]


def _causal_mask(row0, col0, tq, tkc, strict=False):
    rows = row0 + lax.broadcasted_iota(jnp.int32, (tq, tkc), 0)
    cols = col0 + lax.broadcasted_iota(jnp.int32, (tq, tkc), 1)
    return cols < rows if strict else cols <= rows


def _attn_diff_kernel(bound_ref, lam_ref, sub_ref, q_ref, k_ref, v_ref, o_ref, m_ref, acc_ref, *, tq, layer, lam_init):
    qi = pl.program_id(1)
    _init_softmax(m_ref, acc_ref)

    def attend(shift):
        def chunk(off, mask):
            for mp in range(2 * GROUP_HEADS):
                h = mp // 2
                q = q_ref[0, :, mp * LANES:(mp + 1) * LANES]
                k = k_ref[0, pl.ds(off, tq), h * LANES:(h + 1) * LANES]
                v = v_ref[0, pl.ds(off, tq), h * VEXT:(h + 1) * VEXT]
                s = _dot_nt(q, k)
                if shift is not None:
                    s = s - shift
                if mask is not None:
                    s = jnp.where(mask, s, NEG)
                _softmax_chunk(s, v, m_ref, acc_ref, mp, running_max=shift is None)

        def full_chunk(c, carry):
            chunk(pl.multiple_of(c * tq, tq), None)
            return carry

        lax.fori_loop(0, qi, full_chunk, 0)
        chunk(pl.multiple_of(qi * tq, tq), _causal_mask(0, 0, tq, tq))

    _with_score_bound(bound_ref[layer, _BOUND_COLS.index("diff")], attend)

    lv = lam_ref[...]
    lam = (jnp.exp(jnp.sum(lv[0:1] * lv[1:2], axis=-1, keepdims=True))
           - jnp.exp(jnp.sum(lv[2:3] * lv[3:4], axis=-1, keepdims=True)) + lam_init)
    for h in range(GROUP_HEADS):
        o = _normalised(acc_ref, 2 * h) - lam * _normalised(acc_ref, 2 * h + 1)
        o = _rms_rows(o, sub_ref[...]) * (1.0 - lam_init)
        o_ref[0, :, h * LANES:(h + 1) * LANES] = o.astype(o_ref.dtype)


def _attn_mla_kernel(bound_ref, q_ref, k_ref, v_ref, o_ref, m_ref, acc_ref, *, tq, layer):
    qi = pl.program_id(1)
    _init_softmax(m_ref, acc_ref)

    def attend(shift):
        def chunk(off, mask):
            for h in range(GROUP_HEADS):
                q = q_ref[0, :, h * MLA_PAD_DIM:(h + 1) * MLA_PAD_DIM]
                k = k_ref[0, pl.ds(off, tq), h * MLA_PAD_DIM:(h + 1) * MLA_PAD_DIM]
                v = v_ref[0, pl.ds(off, tq), h * VEXT:(h + 1) * VEXT]
                s = _dot_nt(q, k)
                if shift is not None:
                    s = s - shift
                if mask is not None:
                    s = jnp.where(mask, s, NEG)
                _softmax_chunk(s, v, m_ref, acc_ref, h, running_max=shift is None)

        def full_chunk(c, carry):
            chunk(pl.multiple_of(c * tq, tq), None)
            return carry

        lax.fori_loop(0, qi, full_chunk, 0)
        chunk(pl.multiple_of(qi * tq, tq), _causal_mask(0, 0, tq, tq))

    _with_score_bound(bound_ref[layer, _BOUND_COLS.index("mla")], attend)
    for h in range(GROUP_HEADS):
        o_ref[0, :, h * LANES:(h + 1) * LANES] = _normalised(acc_ref, h).astype(o_ref.dtype)


def _attn_stick_kernel(tri_ref, q_ref, k_ref, v_ref, o_ref, carry_ref, acc_ref, *, tq, tkc):
    qi = pl.program_id(1)
    carry_ref[...] = jnp.zeros(carry_ref.shape, F32)
    acc_ref[...] = jnp.zeros(acc_ref.shape, F32)

    def chunk(off, mask):
        tri = tri_ref[...]
        for h in range(GROUP_HEADS):
            q = q_ref[0, :, h * LANES:(h + 1) * LANES]
            k = k_ref[0, pl.ds(off, tkc), h * LANES:(h + 1) * LANES]
            v = v_ref[0, pl.ds(off, tkc), h * LANES:(h + 1) * LANES]
            z = _dot_nt(q, k)
            log_b = jnp.minimum(z, 0.0) - jnp.log2(1.0 + jnp.exp2(-jnp.abs(z)))
            log_1mb = log_b - z
            if mask is not None:
                log_1mb = jnp.where(mask, log_1mb, 0.0)
            hi = log_1mb.astype(BF16)
            lo = (log_1mb - hi.astype(F32)).astype(BF16)
            inner = _dot(hi, tri) + _dot(lo, tri)
            carry = carry_ref[h]
            a = jnp.exp2(log_b + inner + jnp.tile(carry, (1, tkc // LANES)))
            if mask is not None:
                a = jnp.where(mask, a, 0.0)
            acc_ref[h] += _dot(a.astype(BF16), v)
            carry_ref[h] = carry + (inner[:, 0:1] + log_1mb[:, 0:1])

    n_diag = tq // tkc
    for j in range(n_diag):
        col0 = (n_diag - 1 - j) * tkc
        chunk(pl.multiple_of(qi * tq + col0, tkc), _causal_mask(0, col0, tq, tkc, strict=True))

    n_full = qi * n_diag

    def full_chunk(state):
        c, _ = state
        chunk(pl.multiple_of((n_full - 1 - c) * tkc, tkc), None)
        return c + 1, (jnp.max(carry_ref[...]) > STICK_DEAD_LOG2).astype(jnp.int32)

    lax.while_loop(lambda state: jnp.logical_and(state[0] < n_full, state[1] > 0), full_chunk,
                   (jnp.int32(0), jnp.int32(1)))
    for h in range(GROUP_HEADS):
        o_ref[0, :, h * LANES:(h + 1) * LANES] = acc_ref[h].astype(o_ref.dtype)


def _resident_attention(kernel, q, k, v, extra, layer_extra, layer, *, tq, scratch, name, score_bounds=None):
    b, s, _ = q.shape
    q_map = lambda bi, qi: (bi, qi, 0)
    all_map = lambda bi, qi: (bi, 0, 0)
    extra_specs = [pl.BlockSpec(e.shape, lambda bi, qi, nd=e.ndim: (0,) * nd) for e in extra]
    extra_specs += [pl.BlockSpec((None,) + e.shape[1:], lambda bi, qi, nd=e.ndim: (layer,) + (0,) * (nd - 1))
                    for e in layer_extra]
    extra = list(extra) + list(layer_extra)
    if score_bounds is not None:
        extra = [score_bounds] + extra
        extra_specs = [pl.BlockSpec(memory_space=pltpu.SMEM)] + extra_specs
    return pl.pallas_call(
        kernel,
        grid=(b, s // tq),
        in_specs=extra_specs + [pl.BlockSpec((1, tq, q.shape[2]), q_map),
                                pl.BlockSpec((1, s, k.shape[2]), all_map),
                                pl.BlockSpec((1, s, v.shape[2]), all_map)],
        out_specs=pl.BlockSpec((1, tq, GROUP_WIDTH), q_map),
        out_shape=jax.ShapeDtypeStruct((b, s, GROUP_WIDTH), BF16),
        scratch_shapes=scratch,
        compiler_params=_cparams(2),
        name=name,
    )(*extra, q, k, v)


def _attn_dsa_kernel(bound_ref, q_ref, k_ref, v_ref, iq_ref, ik_ref, iw_ref, earlier_ref, o_ref, key_ref, half_ref,
                     m_ref, acc_ref, *, tq, topk, layer):
    qi = pl.program_id(1)
    nkc = qi + 1
    iw_t = iw_ref[0].T

    def score_chunk(c, mask):
        off = pl.multiple_of(c * tq, tq)
        ikc = ik_ref[0, pl.ds(off, tq), :]
        isc = jnp.zeros((tq, tq), F32)
        for hp in range(IDX_HEADS // 2):
            iqp = iq_ref[0, :, hp * LANES:(hp + 1) * LANES]
            for e in range(2):
                hh = 2 * hp + e
                s = _dot_nt(ikc[:, e * LANES:(e + 1) * LANES], iqp)
                isc = isc + jnp.maximum(s, 0.0) * iw_t[hh:hh + 1, :]
        bits = lax.bitcast_convert_type(isc, jnp.int32)
        key = jnp.where(bits < 0, bits ^ jnp.int32(0x7FFFFFFF), bits)
        if mask is not None:
            key = jnp.where(mask, key, INT_MIN)
        key_ref[c] = key
        half_ref[c] = (key >> 16).astype(jnp.int16)

    def full_score_chunk(c, carry):
        score_chunk(c, None)
        return carry

    lax.fori_loop(0, qi, full_score_chunk, 0)
    key_pos = lax.broadcasted_iota(jnp.int32, (tq, tq), 0)
    query_pos = lax.broadcasted_iota(jnp.int32, (tq, tq), 1)
    score_chunk(qi, key_pos <= query_pos)

    one16, zero16, lowest16 = jnp.int16(1), jnp.int16(0), jnp.int16(-32768)

    @pl.when(nkc % COUNT_UNROLL == 1)
    def _():
        half_ref[nkc] = jnp.full((tq, tq), lowest16, jnp.int16)

    def row_count(bound, strict):
        def body(t, part):
            for u in range(COUNT_UNROLL):
                for r in range(tq // COUNT_ACC_ROWS):
                    half = half_ref[COUNT_UNROLL * t + u, r * COUNT_ACC_ROWS:(r + 1) * COUNT_ACC_ROWS, :]
                    part = part + jnp.where(half > bound if strict else half >= bound, one16, zero16)
            return part

        part = lax.fori_loop(0, (nkc + COUNT_UNROLL - 1) // COUNT_UNROLL, body,
                             jnp.zeros((COUNT_ACC_ROWS, tq), jnp.int16))
        return jnp.sum(part.astype(F32), axis=0, keepdims=True)

    def search16(rank, cnt_init):
        def bit(i, carry):
            thr_u, cnt_thr = carry
            cand_u = thr_u | lax.shift_left(jnp.int32(1), 15 - i)
            cnt = row_count((cand_u - 32768).astype(jnp.int16), strict=False)
            take = cnt >= rank
            return jnp.where(take, cand_u, thr_u), jnp.where(take, cnt, cnt_thr)

        return lax.fori_loop(0, 16, bit, (jnp.zeros((1, tq), jnp.int32), cnt_init))

    k_f = jnp.full((1, tq), float(topk), F32)
    hi_u, cnt_hi = search16(k_f, jnp.zeros((1, tq), F32))
    hi_thr = (hi_u - 32768).astype(jnp.int16)
    cnt_gt = row_count(hi_thr, strict=True)

    def low_halves(c, carry):
        lo = ((key_ref[c] & 0xFFFF) - 32768).astype(jnp.int16)
        half_ref[c] = jnp.where(half_ref[c] == hi_thr, lo, lowest16)
        return carry

    lax.fori_loop(0, nkc, low_halves, 0)
    lo_u, cnt_lo = search16(k_f - cnt_gt, cnt_hi - cnt_gt)
    thr_raw = ((hi_u << 16) | lo_u) ^ jnp.int32(INT_MIN)
    short_row = thr_raw == jnp.int32(INT_MIN)
    thr = jnp.maximum(thr_raw, jnp.int32(INT_MIN + 1))
    cnt_ge = cnt_gt + cnt_lo
    tied = jnp.logical_and(jnp.logical_not(short_row), cnt_ge > k_f)
    any_tied = jnp.max(jnp.where(tied, 1.0, 0.0)) > 0.0

    _init_softmax(m_ref, acc_ref)

    def attend(c, selected, shift):
        bias = jnp.where(selected, 0.0 if shift is None else -shift, NEG).T
        off = pl.multiple_of(c * tq, tq)
        for h in range(GROUP_HEADS):
            q = q_ref[0, :, h * LANES:(h + 1) * LANES]
            k = k_ref[0, pl.ds(off, tq), h * LANES:(h + 1) * LANES]
            v = v_ref[0, pl.ds(off, tq), h * VEXT:(h + 1) * VEXT]
            _softmax_chunk(_dot_nt(q, k) + bias, v, m_ref, acc_ref, h, running_max=shift is None)

    def attend_untied(shift):
        def attend_chunk(c, carry):
            attend(c, key_ref[c] >= thr, shift)
            return carry

        lax.fori_loop(0, nkc, attend_chunk, 0)

    @pl.when(jnp.logical_not(any_tied))
    def _():
        _with_score_bound(bound_ref[layer, _BOUND_COLS.index("dsa")], attend_untied)

    @pl.when(any_tied)
    def _():
        def eq_chunk(c, n):
            return n + jnp.sum(jnp.where(key_ref[c] == thr, 1.0, 0.0), axis=0, keepdims=True)

        n_eq = lax.fori_loop(0, nkc, eq_chunk, jnp.zeros((1, tq), F32))
        keep = jnp.where(short_row, 0.0, k_f - (cnt_ge - n_eq))

        def attend_chunk(c, seen):
            key = key_ref[c]
            eq = key == thr
            eq_f = jnp.where(eq, 1.0, 0.0)
            rank = seen + _dot(earlier_ref[...], eq_f.astype(BF16))
            attend(c, jnp.logical_or(key > thr, jnp.logical_and(eq, rank < keep)), None)
            return seen + jnp.sum(eq_f, axis=0, keepdims=True)

        lax.fori_loop(0, nkc, attend_chunk, jnp.zeros((1, tq), F32))

    for h in range(GROUP_HEADS):
        o_ref[0, :, h * LANES:(h + 1) * LANES] = _normalised(acc_ref, h).astype(o_ref.dtype)


def _attn_dsa(q, k, v, iq, ik, iw, score_bounds, layer, *, tq):
    b, s, _ = q.shape
    topk = min(TOPK_MAX, s // 4)
    earlier = np.asarray(np.arange(tq)[:, None] > np.arange(tq)[None, :], dtype=BF16)
    q_map = lambda bi, qi: (bi, qi, 0)
    all_map = lambda bi, qi: (bi, 0, 0)
    return pl.pallas_call(
        functools.partial(_attn_dsa_kernel, tq=tq, topk=topk, layer=layer),
        grid=(b, s // tq),
        in_specs=[pl.BlockSpec(memory_space=pltpu.SMEM),
                  pl.BlockSpec((1, tq, GROUP_WIDTH), q_map),
                  pl.BlockSpec((1, s, GROUP_WIDTH), all_map),
                  pl.BlockSpec((1, s, GROUP_HEADS * VEXT), all_map),
                  pl.BlockSpec((1, tq, IDX_HEADS * IDX_DIM), q_map),
                  pl.BlockSpec((1, s, 2 * LANES), all_map),
                  pl.BlockSpec((1, tq, LANES), q_map),
                  pl.BlockSpec((tq, tq), lambda bi, qi: (0, 0))],
        out_specs=pl.BlockSpec((1, tq, GROUP_WIDTH), q_map),
        out_shape=jax.ShapeDtypeStruct((b, s, GROUP_WIDTH), BF16),
        scratch_shapes=[pltpu.VMEM((s // tq, tq, tq), jnp.int32),
                        pltpu.VMEM((-(-(s // tq) // COUNT_UNROLL) * COUNT_UNROLL, tq, tq), jnp.int16),
                        pltpu.VMEM((GROUP_HEADS, tq, LANES), F32),
                        pltpu.VMEM((GROUP_HEADS, tq, VEXT), F32)],
        compiler_params=_cparams(2),
        name="attn_dsa",
    )(score_bounds, q, k, v, iq, ik, iw, earlier)


def _rope_tables(seq, period, half):
    inv = np.float32(ROPE_THETA) ** (-np.arange(half, dtype=np.float32) / np.float32(half))
    lane = np.arange(LANES)
    idx = lane % period
    active = idx < half
    ang = np.arange(seq, dtype=np.float32)[:, None] * inv[np.minimum(idx, half - 1)][None, :]
    cos, sin = np.cos(ang), np.sin(ang)
    return np.stack([np.where(active, cos, 1.0), np.where(active, np.where(lane < LANES // 2, -sin, sin), 0.0)]
                    ).astype(np.float32)


def _half_split_layout(dims, half, start=0):
    first = list(range(half)) + list(range(2 * half, 2 * half + (dims - 2 * half) // 2))
    second = list(range(half, 2 * half)) + list(range(2 * half + (dims - 2 * half) // 2, dims))
    lanes = [-1] * LANES
    lanes[start:start + len(first)] = first
    lanes[start + LANES // 2:start + LANES // 2 + len(second)] = second
    return lanes


_HEAD_LAYOUT = _half_split_layout(HEAD_DIM, HEAD_DIM // 8)
_SUB_LAYOUT = _half_split_layout(DIFF_SUB_DIM, DIFF_SUB_DIM // 8)
_PAIR_LAYOUT = [a if a >= 0 else (DIFF_SUB_DIM + b if b >= 0 else -1)
                for a, b in zip(_SUB_LAYOUT, _half_split_layout(DIFF_SUB_DIM, DIFF_SUB_DIM // 8, start=32))]
_ROPE64_LAYOUT = _half_split_layout(MLA_ROPE_DIM, MLA_ROPE_DIM // 2)


def _take_cols(a, layout):
    src = max(layout) + 1
    place = np.zeros((src, LANES), np.float32)
    for lane, i in enumerate(layout):
        if i >= 0:
            place[i, lane] = 1.0
    blocks = a.reshape(a.shape[:-1] + (-1, src))
    precision = lax.Precision.HIGHEST if a.dtype == F32 else None
    out = jnp.einsum("...k,kj->...j", blocks, jnp.asarray(place, a.dtype), precision=precision,
                     preferred_element_type=a.dtype)
    return out.reshape(a.shape[:-1] + (-1,))


def _pad_cols(a, width):
    return jnp.pad(a, ((0, 0),) * (a.ndim - 1) + ((0, width - a.shape[-1]),))


def _relayout_w_in(w):
    sizes = (512,) * 3 + (MLA_Q_RANK, MLA_KV_RANK, MLA_ROPE_DIM) + (512,) * 6 + (IDX_HEADS * IDX_DIM, IDX_DIM, IDX_HEADS)
    names = ("a_q", "a_k", "a_v", "b_cq", "b_ckv", "b_kr", "c_q", "c_k", "c_v", "d_q", "d_k", "d_v", "d_iq", "d_ik", "d_iw")
    w = w.astype(BF16)
    parts, start = {}, 0
    for nme, sz in zip(names, sizes):
        parts[nme] = w[..., start:start + sz]
        start += sz
    for nme in ("a_q", "a_k", "d_iq"):
        parts[nme] = _take_cols(parts[nme], _PAIR_LAYOUT)
    for nme in ("d_q", "d_k"):
        parts[nme] = _take_cols(parts[nme], _HEAD_LAYOUT)
    parts["d_ik"] = _take_cols(parts["d_ik"], _SUB_LAYOUT)
    parts["b_kr"] = _take_cols(parts["b_kr"], _ROPE64_LAYOUT)
    order = sorted(_OFF, key=_OFF.get)
    ends = [_OFF[nme] for nme in order[1:]] + [Y_WIDTH]
    return jnp.concatenate([_pad_cols(parts[nme], end - _OFF[nme]) for nme, end in zip(order, ends)], axis=-1)


def _relayout_mla_q(w):
    lead = w.shape[:-1]
    w = w.reshape(lead + (-1, MLA_QK_DIM))
    rope = _take_cols(w[..., MLA_NOPE_DIM:], _ROPE64_LAYOUT)
    return jnp.concatenate([w[..., :MLA_NOPE_DIM], rope], axis=-1).reshape(lead + (-1,))


_BOUND_COLS = ("diff", "mla", "dsa", "unused")
_GAIN_ROWS = ("a_q", "a_k", "b_cq", "b_ckv", "b_q", "b_k", "d_q", "d_k", "d_ik")


def _score_bounds(diff_qk_norm, mla_qk_norm, dsa_qk_norm):
    def bound(dim, qk_norm):
        peak = jnp.max(jnp.abs(qk_norm), axis=-1)
        return (dim ** 0.5 * LOG2E * SCORE_BOUND_MARGIN) * peak[:, 0] * peak[:, 1]

    cols = [bound(DIFF_SUB_DIM, diff_qk_norm), bound(MLA_QK_DIM, mla_qk_norm), bound(HEAD_DIM, dsa_qk_norm)]
    return jnp.stack(cols + [jnp.zeros_like(cols[0])], axis=1).astype(F32)


def _packed_gains(diff_qk_norm, mla_q_a_norm, mla_kv_a_norm, mla_qk_norm, dsa_qk_norm, idx_k_norm):
    def tiled(g, reps):
        return jnp.tile(g, (1, reps))

    rows = dict(
        a_q=_take_cols(tiled(diff_qk_norm[:, 0], 8), _PAIR_LAYOUT),
        a_k=_take_cols(tiled(diff_qk_norm[:, 1], 8), _PAIR_LAYOUT),
        b_cq=mla_q_a_norm, b_ckv=mla_kv_a_norm,
        b_q=_relayout_mla_q(tiled(mla_qk_norm[:, 0], GROUP_HEADS)),
        b_k=_relayout_mla_q(mla_qk_norm[:, 1]),
        d_q=_take_cols(tiled(dsa_qk_norm[:, 0], 4), _HEAD_LAYOUT),
        d_k=_take_cols(tiled(dsa_qk_norm[:, 1], 4), _HEAD_LAYOUT),
        d_ik=_take_cols(idx_k_norm, _SUB_LAYOUT))
    packed = jnp.stack([_pad_cols(rows[nme], 1024) for nme in _GAIN_ROWS], axis=1)
    return jnp.pad(packed, ((0, 0), (0, 16 - len(_GAIN_ROWS)), (0, 0)))


def kernel(x, attn_norm, w_in, diff_qk_norm, diff_lambda, diff_subln, mla_q_a_norm, mla_wq_b, mla_kv_a_norm,
           mla_wkv_b, mla_qk_norm, dsa_qk_norm, idx_k_norm, w_o, ffn_norm, w_gate, w_up, w_down):
    b, s, d = x.shape
    n = b * s
    depth = w_in.shape[0]
    tm = min(1024, n)
    tq = min(512, s)
    tabs = (_rope_tables(s, LANES // 4, DIFF_SUB_DIM // 8),
            _rope_tables(s, LANES // 2, MLA_ROPE_DIM // 2),
            _rope_tables(s, LANES // 2, HEAD_DIM // 8))
    sub_head = np.arange(GROUP_WIDTH) // LANES * 2 + (np.arange(GROUP_WIDTH) % (LANES // 2)) // (LANES // 4)
    ones64 = np.asarray(sub_head[:, None] == sub_head[None, :], dtype=BF16)
    tk_stick = min(256, s)
    tri = np.asarray(np.arange(tk_stick)[:, None] > np.arange(tk_stick)[None, :], dtype=BF16)

    w_in_r = _relayout_w_in(w_in)
    wq = jnp.pad(_relayout_mla_q(mla_wq_b.astype(BF16)), ((0, 0), (0, 512 - MLA_Q_RANK), (0, 0)))
    wkv = mla_wkv_b.reshape(depth, MLA_KV_RANK, GROUP_HEADS, 2, HEAD_DIM).transpose(0, 1, 3, 2, 4)
    wkv = wkv.reshape(depth, MLA_KV_RANK, 2 * GROUP_WIDTH).astype(BF16)
    gains = _packed_gains(diff_qk_norm, mla_q_a_norm, mla_kv_a_norm, mla_qk_norm, dsa_qk_norm, idx_k_norm)
    bounds = _score_bounds(diff_qk_norm, mla_qk_norm, dsa_qk_norm)
    attn_gain, ffn_gain = attn_norm.reshape(depth, 1, d), ffn_norm.reshape(depth, 1, d)
    subln = diff_subln.reshape(depth, 1, HEAD_DIM)
    w_o_b, w_down_b = w_o.astype(BF16), w_down.astype(BF16)

    xf = x.reshape(n, d)
    for l in range(depth):
        lam_init = 0.8 - 0.6 * math.exp(-0.3 * l)
        y = _rms_matmul(xf, attn_gain, w_in_r, l, tm=tm, tn=Y_WIDTH // 4, out_dtype=F32)
        (qa, ka, va, qb, kb, vb, qc, kc, vc, qd, kd, vd, iq, ik, iw) = [
            a.reshape(b, s, a.shape[1]) for a in _prep(y, s, tabs, ones64, gains, wq, wkv, l, tm=min(256, s))]

        o_a = _resident_attention(
            functools.partial(_attn_diff_kernel, tq=tq, layer=l, lam_init=lam_init), qa, ka, va,
            [], [diff_lambda, subln], l, tq=tq,
            scratch=[pltpu.VMEM((8, tq, LANES), F32), pltpu.VMEM((8, tq, VEXT), F32)], name="attn_diff",
            score_bounds=bounds)
        o_b = _resident_attention(
            functools.partial(_attn_mla_kernel, tq=tq, layer=l), qb, kb, vb, [], [], l, tq=tq,
            scratch=[pltpu.VMEM((4, tq, LANES), F32), pltpu.VMEM((4, tq, VEXT), F32)], name="attn_mla",
            score_bounds=bounds)
        o_c = _resident_attention(
            functools.partial(_attn_stick_kernel, tq=tq, tkc=tk_stick), qc, kc, vc, [tri], [], l, tq=tq,
            scratch=[pltpu.VMEM((4, tq, LANES), F32), pltpu.VMEM((4, tq, LANES), F32)], name="attn_stick")
        o_d = _attn_dsa(qd, kd, vd, iq, ik, iw, bounds, l, tq=tq)

        mixed = [o.reshape(n, GROUP_WIDTH) for o in (o_a, o_b, o_c, o_d)]
        xf = _matmul_residual(mixed, w_o_b, xf, l, tm=min(512, n), tn=d)
        act = _ffn_up(xf, ffn_gain, w_gate, w_up, l, tm=tm, tn=512)
        xf = _matmul_residual([act], w_down_b, xf, l, tm=tm, tn=512)
    return xf.reshape(b, s, d)
```

```python
import functools
import math

import jax
import jax.numpy as jnp
import numpy as np
from jax import lax
from jax.experimental import pallas as pl
from jax.experimental.pallas import tpu as pltpu

F32 = jnp.float32
BF16 = jnp.bfloat16

HEAD_DIM = 128
GROUP_HEADS = 4
GROUP_WIDTH = GROUP_HEADS * HEAD_DIM
ROPE_THETA = 500000.0
EPS = 1e-6
DIFF_SUB_DIM = 64
MLA_Q_RANK = 448
MLA_KV_RANK = 128
MLA_NOPE_DIM = 128
MLA_ROPE_DIM = 64
MLA_QK_DIM = MLA_NOPE_DIM + MLA_ROPE_DIM
MLA_PAD_DIM = 256
IDX_HEADS = 16
IDX_DIM = 64
TOPK_MAX = 256

LANES = 128
VEXT = 2 * LANES
Y_WIDTH = 13 * GROUP_WIDTH
NEG = -1e30
INT_MIN = -2147483648
LOG2E = 1.4426950408889634
COUNT_UNROLL = 2
COUNT_ACC_ROWS = 64
MAX_FIXED_SHIFT = 56.0
SCORE_BOUND_MARGIN = 1.02
STICK_DEAD_LOG2 = -160.0
VMEM_LIMIT = 56 * 1024 * 1024

_OFF = dict(a_q=0, a_k=512, a_v=1024, b_cq=1536, b_ckv=2048, b_kr=2176, d_ik=2304, d_iw=2432,
            c_q=2560, c_k=3072, c_v=3584, d_q=4096, d_k=4608, d_v=5120, d_iq=5632)


def _cparams(n_axes):
    return pltpu.CompilerParams(dimension_semantics=("arbitrary",) * n_axes, vmem_limit_bytes=VMEM_LIMIT)


def _dot(a, b):
    return jnp.dot(a, b, preferred_element_type=F32)


def _dot_nt(a, b):
    return lax.dot_general(a, b, (((1,), (1,)), ((), ())), preferred_element_type=F32)


def _rms_rows(x, g):
    ms = jnp.mean(x * x, axis=-1, keepdims=True)
    return x * lax.rsqrt(ms + EPS) * g


def _rms_matmul_kernel(x_ref, g_ref, w_ref, o_ref, h_ref):
    @pl.when(pl.program_id(1) == 0)
    def _():
        h_ref[...] = _rms_rows(x_ref[...], g_ref[...]).astype(BF16)

    o_ref[...] = _dot(h_ref[...], w_ref[...]).astype(o_ref.dtype)


def _rms_matmul(x, g, w, layer, *, tm, tn, out_dtype):
    n, d = x.shape
    m = w.shape[2]
    return pl.pallas_call(
        _rms_matmul_kernel,
        grid=(n // tm, m // tn),
        in_specs=[pl.BlockSpec((tm, d), lambda i, j: (i, 0)),
                  pl.BlockSpec((None, 1, d), lambda i, j: (layer, 0, 0)),
                  pl.BlockSpec((None, d, tn), lambda i, j: (layer, 0, j))],
        out_specs=pl.BlockSpec((tm, tn), lambda i, j: (i, j)),
        out_shape=jax.ShapeDtypeStruct((n, m), out_dtype),
        scratch_shapes=[pltpu.VMEM((tm, d), BF16)],
        compiler_params=_cparams(2),
        name="rms_matmul",
    )(x, g, w)


def _ffn_up_kernel(x_ref, g_ref, wg_ref, wu_ref, o_ref, h_ref):
    @pl.when(pl.program_id(1) == 0)
    def _():
        h_ref[...] = _rms_rows(x_ref[...], g_ref[...]).astype(BF16)

    h = h_ref[...]
    a = _dot(h, wg_ref[...].astype(BF16))
    b = _dot(h, wu_ref[...].astype(BF16))
    o_ref[...] = (jax.nn.silu(a) * b).astype(o_ref.dtype)


def _ffn_up(x, g, wg, wu, layer, *, tm, tn):
    n, d = x.shape
    m = wg.shape[2]
    return pl.pallas_call(
        _ffn_up_kernel,
        grid=(n // tm, m // tn),
        in_specs=[pl.BlockSpec((tm, d), lambda i, j: (i, 0)),
                  pl.BlockSpec((None, 1, d), lambda i, j: (layer, 0, 0)),
                  pl.BlockSpec((None, d, tn), lambda i, j: (layer, 0, j)),
                  pl.BlockSpec((None, d, tn), lambda i, j: (layer, 0, j))],
        out_specs=pl.BlockSpec((tm, tn), lambda i, j: (i, j)),
        out_shape=jax.ShapeDtypeStruct((n, m), BF16),
        scratch_shapes=[pltpu.VMEM((tm, d), BF16)],
        compiler_params=_cparams(2),
        name="ffn_up",
    )(x, g, wg, wu)


def _matmul_residual_kernel(*refs):
    a_refs, (w_ref, r_ref, o_ref) = refs[:-3], refs[-3:]
    acc = r_ref[...]
    row = 0
    for a_ref in a_refs:
        k = a_ref.shape[1]
        acc = acc + _dot(a_ref[...], w_ref[row:row + k, :])
        row += k
    o_ref[...] = acc


def _matmul_residual(parts, w, r, layer, *, tm, tn):
    n = parts[0].shape[0]
    _, k, m = w.shape
    return pl.pallas_call(
        _matmul_residual_kernel,
        grid=(n // tm, m // tn),
        in_specs=[pl.BlockSpec((tm, a.shape[1]), lambda i, j: (i, 0)) for a in parts]
                 + [pl.BlockSpec((None, k, tn), lambda i, j: (layer, 0, j)),
                    pl.BlockSpec((tm, tn), lambda i, j: (i, j))],
        out_specs=pl.BlockSpec((tm, tn), lambda i, j: (i, j)),
        out_shape=jax.ShapeDtypeStruct((n, m), F32),
        compiler_params=_cparams(2),
        name="matmul_residual",
    )(*parts, w, r)


def _rope(x, t_ref, blocks=None):
    outs = []
    for b in range(x.shape[1] // LANES):
        xb = x[:, b * LANES:(b + 1) * LANES]
        if blocks is None or b in blocks:
            xb = xb * t_ref[0] + pltpu.roll(xb, LANES // 2, 1) * t_ref[1]
        outs.append(xb)
    return outs[0] if len(outs) == 1 else jnp.concatenate(outs, axis=1)


def _rms_lane_groups(x, width, count):
    outs = []
    for h in range(x.shape[1] // width):
        xh = x[:, h * width:(h + 1) * width]
        ms = jnp.sum(xh * xh, axis=-1, keepdims=True) * (1.0 / count)
        outs.append(xh * lax.rsqrt(ms + EPS))
    return outs[0] if len(outs) == 1 else jnp.concatenate(outs, axis=1)


def _rms_sub64(x, ones_ref):
    x2 = x * x
    hi = x2.astype(BF16)
    lo = (x2 - hi.astype(F32)).astype(BF16)
    ms = (_dot(hi, ones_ref[...]) + _dot(lo, ones_ref[...])) * (1.0 / DIFF_SUB_DIM)
    return x * lax.rsqrt(ms + EPS)


def _store_vext(ref, v):
    ones = jnp.ones((v.shape[0], LANES), BF16)
    for h in range(GROUP_HEADS):
        ref[:, h * VEXT:h * VEXT + LANES] = v[:, h * LANES:(h + 1) * LANES].astype(BF16)
        ref[:, h * VEXT + LANES:(h + 1) * VEXT] = ones


def _prep_kernel(y_ref, ta_ref, tb_ref, td_ref, ones_ref, g_ref, wq_ref, wkv_ref,
                 qa_ref, ka_ref, va_ref, qb_ref, kb_ref, vb_ref, qc_ref, kc_ref, vc_ref,
                 qd_ref, kd_ref, vd_ref, iq_ref, ik_ref, iw_ref):
    def sec(name, width):
        return y_ref[:, _OFF[name]:_OFF[name] + width]

    def gain(name, width):
        row = _GAIN_ROWS.index(name)
        return g_ref[row:row + 1, :width]

    tm = y_ref.shape[0]
    lane = lax.broadcasted_iota(jnp.int32, (tm, LANES), 1)
    first_half = (lane % (LANES // 2)) < DIFF_SUB_DIM // 2

    qa = _rope(_rms_sub64(sec("a_q", 512), ones_ref) * gain("a_q", 512), ta_ref) * (DIFF_SUB_DIM ** -0.5 * LOG2E)
    for h in range(GROUP_HEADS):
        qh = qa[:, h * LANES:(h + 1) * LANES]
        qa_ref[:, (2 * h) * LANES:(2 * h + 1) * LANES] = jnp.where(first_half, qh, 0.0).astype(BF16)
        qa_ref[:, (2 * h + 1) * LANES:(2 * h + 2) * LANES] = jnp.where(first_half, 0.0, qh).astype(BF16)
    ka_ref[...] = _rope(_rms_sub64(sec("a_k", 512), ones_ref) * gain("a_k", 512), ta_ref).astype(BF16)
    _store_vext(va_ref, sec("a_v", 512))

    cq = sec("b_cq", 512)
    cq = cq * lax.rsqrt(jnp.sum(cq * cq, axis=-1, keepdims=True) * (1.0 / MLA_Q_RANK) + EPS) * gain("b_cq", 512)
    qb = _dot(cq.astype(BF16), wq_ref[...])
    qb = _rms_lane_groups(qb, MLA_PAD_DIM, MLA_QK_DIM) * gain("b_q", 1024)
    ckv = _rms_lane_groups(sec("b_ckv", 128), 128, MLA_KV_RANK) * gain("b_ckv", 128)
    kv = _dot(ckv.astype(BF16), wkv_ref[...])
    kr = sec("b_kr", 128)
    kr_ss = jnp.sum(kr * kr, axis=-1, keepdims=True)
    for h in range(GROUP_HEADS):
        lo, hi = h * MLA_PAD_DIM, (h + 1) * MLA_PAD_DIM
        qb_ref[:, lo:hi] = (_rope(qb[:, lo:hi], tb_ref, blocks=(1,)) * (MLA_QK_DIM ** -0.5 * LOG2E)).astype(BF16)
        kn = kv[:, h * LANES:(h + 1) * LANES]
        ms = (jnp.sum(kn * kn, axis=-1, keepdims=True) + kr_ss) * (1.0 / MLA_QK_DIM)
        kh = jnp.concatenate([kn, kr], axis=1) * lax.rsqrt(ms + EPS) * gain("b_k", MLA_PAD_DIM)
        kb_ref[:, lo:hi] = _rope(kh, tb_ref, blocks=(1,)).astype(BF16)
    _store_vext(vb_ref, kv[:, GROUP_WIDTH:])

    qc_ref[...] = (sec("c_q", 512) * (HEAD_DIM ** -0.5 * LOG2E)).astype(BF16)
    kc_ref[...] = sec("c_k", 512).astype(BF16)
    vc_ref[...] = sec("c_v", 512).astype(BF16)

    qd = _rope(_rms_lane_groups(sec("d_q", 512), HEAD_DIM, HEAD_DIM) * gain("d_q", 512), td_ref)
    qd_ref[...] = (qd * (HEAD_DIM ** -0.5 * LOG2E)).astype(BF16)
    kd_ref[...] = _rope(_rms_lane_groups(sec("d_k", 512), HEAD_DIM, HEAD_DIM) * gain("d_k", 512), td_ref).astype(BF16)
    _store_vext(vd_ref, sec("d_v", 512))
    for half in range(2):
        iq = y_ref[:, _OFF["d_iq"] + half * 512:_OFF["d_iq"] + (half + 1) * 512]
        iq_ref[:, half * 512:(half + 1) * 512] = (_rope(iq, ta_ref) * (IDX_DIM ** -0.5)).astype(BF16)
    ik = _rope(_rms_lane_groups(sec("d_ik", 128), 128, IDX_DIM) * gain("d_ik", 128), ta_ref)
    ik_ref[:, :LANES] = ik.astype(BF16)
    ik_ref[:, LANES:] = pltpu.roll(ik, IDX_DIM // 2, 1).astype(BF16)
    iw_ref[...] = sec("d_iw", 128) * (IDX_HEADS ** -0.5)


def _prep(y, seq, tabs, ones64, gains, wq, wkv, layer, *, tm):
    n = y.shape[0]
    nblk_seq = seq // tm
    ta, tb, td = tabs

    def rows(width):
        return pl.BlockSpec((tm, width), lambda i: (i, 0))

    def table():
        return pl.BlockSpec((2, tm, LANES), lambda i: (0, i % nblk_seq, 0))

    def whole(a):
        return pl.BlockSpec(a.shape, lambda i: (0,) * a.ndim)

    def of_layer(a):
        return pl.BlockSpec((None,) + a.shape[1:], lambda i: (layer,) + (0,) * (a.ndim - 1))

    widths = [1024, 512, 1024, 1024, 1024, 1024, 512, 512, 512, 512, 512, 1024, 1024, 256]
    out_shape = [jax.ShapeDtypeStruct((n, w), BF16) for w in widths] + [jax.ShapeDtypeStruct((n, LANES), F32)]
    out_specs = [rows(w) for w in widths] + [rows(LANES)]
    return pl.pallas_call(
        _prep_kernel,
        grid=(n // tm,),
        in_specs=[rows(Y_WIDTH), table(), table(), table(), whole(ones64), of_layer(gains), of_layer(wq), of_layer(wkv)],
        out_specs=out_specs,
        out_shape=out_shape,
        compiler_params=_cparams(1),
        name="prep",
    )(y, ta, tb, td, ones64, gains, wq, wkv)


def _softmax_chunk(s, vext, m_ref, acc_ref, idx, running_max, row0=0):
    rows = slice(row0, row0 + s.shape[0])
    if not running_max:
        acc_ref[idx, rows] += _dot(jnp.exp2(s).astype(BF16), vext)
        return
    m_prev = m_ref[idx, rows]
    m_new = jnp.maximum(m_prev, jnp.max(s, axis=1, keepdims=True))
    p = jnp.exp2(s - jnp.tile(m_new, (1, s.shape[1] // LANES)))
    alpha = jnp.exp2(m_prev - m_new)
    acc_ref[idx, rows] = jnp.tile(alpha, (1, VEXT // LANES)) * acc_ref[idx, rows] + _dot(p.astype(BF16), vext)
    m_ref[idx, rows] = m_new


def _init_softmax(m_ref, acc_ref):
    m_ref[...] = jnp.full(m_ref.shape, NEG, F32)
    acc_ref[...] = jnp.zeros(acc_ref.shape, F32)


def _with_score_bound(bound, attend):
    @pl.when(bound <= MAX_FIXED_SHIFT)
    def _():
        attend(bound)

    @pl.when(jnp.logical_not(bound <= MAX_FIXED_SHIFT))
    def _():
        attend(None)


def _normalised(acc_ref, idx):
    acc = acc_ref[idx]
    return acc[:, :LANES] / acc[:, LANES:]


def _causal_mask(row0, col0, tq, tkc, strict=False):
    rows = row0 + lax.broadcasted_iota(jnp.int32, (tq, tkc), 0)
    cols = col0 + lax.broadcasted_iota(jnp.int32, (tq, tkc), 1)
    return cols < rows if strict else cols <= rows


def _causal_chunks(chunk, qi, tq):
    def full_chunk(c, carry):
        chunk(pl.multiple_of(c * tq, tq), tq, None)
        return carry

    lax.fori_loop(0, qi, full_chunk, 0)
    half = tq // 2
    chunk(pl.multiple_of(qi * tq, tq), half, _causal_mask(0, 0, tq, half))
    chunk(pl.multiple_of(qi * tq + half, half), half, _causal_mask(0, 0, half, half), half, half)


def _attn_diff_kernel(bound_ref, lam_ref, sub_ref, q_ref, k_ref, v_ref, o_ref, m_ref, acc_ref, *, tq, layer, lam_init):
    qi = pl.program_id(1)
    _init_softmax(m_ref, acc_ref)

    def attend(shift):
        def chunk(off, keys, mask, row0=0, rows=tq):
            for mp in range(2 * GROUP_HEADS):
                h = mp // 2
                q = q_ref[0, row0:row0 + rows, mp * LANES:(mp + 1) * LANES]
                k = k_ref[0, pl.ds(off, keys), h * LANES:(h + 1) * LANES]
                v = v_ref[0, pl.ds(off, keys), h * VEXT:(h + 1) * VEXT]
                s = _dot_nt(q, k)
                if shift is not None:
                    s = s - shift
                if mask is not None:
                    s = jnp.where(mask, s, NEG)
                _softmax_chunk(s, v, m_ref, acc_ref, mp, shift is None, row0)

        _causal_chunks(chunk, qi, tq)

    _with_score_bound(bound_ref[layer, _BOUND_COLS.index("diff")], attend)

    lv = lam_ref[...]
    lam = (jnp.exp(jnp.sum(lv[0:1] * lv[1:2], axis=-1, keepdims=True))
           - jnp.exp(jnp.sum(lv[2:3] * lv[3:4], axis=-1, keepdims=True)) + lam_init)
    for h in range(GROUP_HEADS):
        o = _normalised(acc_ref, 2 * h) - lam * _normalised(acc_ref, 2 * h + 1)
        o = _rms_rows(o, sub_ref[...]) * (1.0 - lam_init)
        o_ref[0, :, h * LANES:(h + 1) * LANES] = o.astype(o_ref.dtype)


def _attn_mla_kernel(bound_ref, q_ref, k_ref, v_ref, o_ref, m_ref, acc_ref, *, tq, layer):
    qi = pl.program_id(1)
    _init_softmax(m_ref, acc_ref)

    def attend(shift):
        def chunk(off, keys, mask, row0=0, rows=tq):
            for h in range(GROUP_HEADS):
                q = q_ref[0, row0:row0 + rows, h * MLA_PAD_DIM:(h + 1) * MLA_PAD_DIM]
                k = k_ref[0, pl.ds(off, keys), h * MLA_PAD_DIM:(h + 1) * MLA_PAD_DIM]
                v = v_ref[0, pl.ds(off, keys), h * VEXT:(h + 1) * VEXT]
                s = _dot_nt(q, k)
                if shift is not None:
                    s = s - shift
                if mask is not None:
                    s = jnp.where(mask, s, NEG)
                _softmax_chunk(s, v, m_ref, acc_ref, h, shift is None, row0)

        _causal_chunks(chunk, qi, tq)

    _with_score_bound(bound_ref[layer, _BOUND_COLS.index("mla")], attend)
    for h in range(GROUP_HEADS):
        o_ref[0, :, h * LANES:(h + 1) * LANES] = _normalised(acc_ref, h).astype(o_ref.dtype)


def _attn_stick_kernel(tri_ref, q_ref, k_ref, v_ref, o_ref, carry_ref, acc_ref, *, tq, tkc):
    qi = pl.program_id(1)
    carry_ref[...] = jnp.zeros(carry_ref.shape, F32)
    acc_ref[...] = jnp.zeros(acc_ref.shape, F32)

    def chunk(off, mask):
        tri = tri_ref[...]
        for h in range(GROUP_HEADS):
            q = q_ref[0, :, h * LANES:(h + 1) * LANES]
            k = k_ref[0, pl.ds(off, tkc), h * LANES:(h + 1) * LANES]
            v = v_ref[0, pl.ds(off, tkc), h * LANES:(h + 1) * LANES]
            z = _dot_nt(q, k)
            log_b = jnp.minimum(z, 0.0) - jnp.log2(1.0 + jnp.exp2(-jnp.abs(z)))
            log_1mb = log_b - z
            if mask is not None:
                log_1mb = jnp.where(mask, log_1mb, 0.0)
            hi = log_1mb.astype(BF16)
            lo = (log_1mb - hi.astype(F32)).astype(BF16)
            inner = _dot(hi, tri) + _dot(lo, tri)
            carry = carry_ref[h]
            a = jnp.exp2(log_b + inner + jnp.tile(carry, (1, tkc // LANES)))
            if mask is not None:
                a = jnp.where(mask, a, 0.0)
            acc_ref[h] += _dot(a.astype(BF16), v)
            carry_ref[h] = carry + (inner[:, 0:1] + log_1mb[:, 0:1])

    n_diag = tq // tkc
    for j in range(n_diag):
        col0 = (n_diag - 1 - j) * tkc
        chunk(pl.multiple_of(qi * tq + col0, tkc), _causal_mask(0, col0, tq, tkc, strict=True))

    n_full = qi * n_diag

    def full_chunk(state):
        c, _ = state
        chunk(pl.multiple_of((n_full - 1 - c) * tkc, tkc), None)
        return c + 1, (jnp.max(carry_ref[...]) > STICK_DEAD_LOG2).astype(jnp.int32)

    lax.while_loop(lambda state: jnp.logical_and(state[0] < n_full, state[1] > 0), full_chunk,
                   (jnp.int32(0), jnp.int32(1)))
    for h in range(GROUP_HEADS):
        o_ref[0, :, h * LANES:(h + 1) * LANES] = acc_ref[h].astype(o_ref.dtype)


def _resident_attention(kernel, q, k, v, extra, layer_extra, layer, *, tq, scratch, name, score_bounds=None):
    b, s, _ = q.shape
    q_map = lambda bi, qi: (bi, qi, 0)
    all_map = lambda bi, qi: (bi, 0, 0)
    extra_specs = [pl.BlockSpec(e.shape, lambda bi, qi, nd=e.ndim: (0,) * nd) for e in extra]
    extra_specs += [pl.BlockSpec((None,) + e.shape[1:], lambda bi, qi, nd=e.ndim: (layer,) + (0,) * (nd - 1))
                    for e in layer_extra]
    extra = list(extra) + list(layer_extra)
    if score_bounds is not None:
        extra = [score_bounds] + extra
        extra_specs = [pl.BlockSpec(memory_space=pltpu.SMEM)] + extra_specs
    return pl.pallas_call(
        kernel,
        grid=(b, s // tq),
        in_specs=extra_specs + [pl.BlockSpec((1, tq, q.shape[2]), q_map),
                                pl.BlockSpec((1, s, k.shape[2]), all_map),
                                pl.BlockSpec((1, s, v.shape[2]), all_map)],
        out_specs=pl.BlockSpec((1, tq, GROUP_WIDTH), q_map),
        out_shape=jax.ShapeDtypeStruct((b, s, GROUP_WIDTH), BF16),
        scratch_shapes=scratch,
        compiler_params=_cparams(2),
        name=name,
    )(*extra, q, k, v)


def _attn_dsa_kernel(bound_ref, q_ref, k_ref, v_ref, iq_ref, ik_ref, iw_ref, earlier_ref, o_ref, key_ref, half_ref,
                     m_ref, acc_ref, *, tq, topk, layer):
    qi = pl.program_id(1)
    nkc = qi + 1
    iw_t = iw_ref[0].T

    one16, zero16, lowest16 = jnp.int16(1), jnp.int16(0), jnp.int16(-32768)
    half = tq // 2
    diag_tiles = ((0, half, 0, tq), (half, half, half, half))

    def score_tile(c, k0, nk, q0, nq, causal):
        off = pl.multiple_of(c * tq + k0, nk)
        ikc = ik_ref[0, pl.ds(off, nk), :]
        isc = jnp.zeros((nk, nq), F32)
        for hp in range(IDX_HEADS // 2):
            iqp = iq_ref[0, q0:q0 + nq, hp * LANES:(hp + 1) * LANES]
            for e in range(2):
                hh = 2 * hp + e
                s = _dot_nt(ikc[:, e * LANES:(e + 1) * LANES], iqp)
                isc = isc + jnp.maximum(s, 0.0) * iw_t[hh:hh + 1, q0:q0 + nq]
        bits = lax.bitcast_convert_type(isc, jnp.int32)
        key = jnp.where(bits < 0, bits ^ jnp.int32(0x7FFFFFFF), bits)
        if causal:
            key_pos = k0 + lax.broadcasted_iota(jnp.int32, (nk, nq), 0)
            query_pos = q0 + lax.broadcasted_iota(jnp.int32, (nk, nq), 1)
            key = jnp.where(key_pos <= query_pos, key, INT_MIN)
        key_ref[c, k0:k0 + nk, q0:q0 + nq] = key
        half_ref[c, k0:k0 + nk, q0:q0 + nq] = (key >> 16).astype(jnp.int16)

    def full_score_chunk(c, carry):
        score_tile(c, 0, tq, 0, tq, False)
        return carry

    lax.fori_loop(0, qi, full_score_chunk, 0)
    for tile in diag_tiles:
        score_tile(qi, *tile, True)
    key_ref[qi, half:, :half] = jnp.full((half, half), INT_MIN, jnp.int32)
    half_ref[qi, half:, :half] = jnp.full((half, half), lowest16, jnp.int16)

    @pl.when(nkc % COUNT_UNROLL == 1)
    def _():
        half_ref[nkc] = jnp.full((tq, tq), lowest16, jnp.int16)

    def row_count(bound, strict):
        def body(t, part):
            for u in range(COUNT_UNROLL):
                for r in range(tq // COUNT_ACC_ROWS):
                    half = half_ref[COUNT_UNROLL * t + u, r * COUNT_ACC_ROWS:(r + 1) * COUNT_ACC_ROWS, :]
                    part = part + jnp.where(half > bound if strict else half >= bound, one16, zero16)
            return part

        part = lax.fori_loop(0, (nkc + COUNT_UNROLL - 1) // COUNT_UNROLL, body,
                             jnp.zeros((COUNT_ACC_ROWS, tq), jnp.int16))
        return jnp.sum(part.astype(F32), axis=0, keepdims=True)

    def search16(rank, cnt_init):
        def bit(i, carry):
            thr_u, cnt_thr = carry
            cand_u = thr_u | lax.shift_left(jnp.int32(1), 15 - i)
            cnt = row_count((cand_u - 32768).astype(jnp.int16), strict=False)
            take = cnt >= rank
            return jnp.where(take, cand_u, thr_u), jnp.where(take, cnt, cnt_thr)

        return lax.fori_loop(0, 16, bit, (jnp.zeros((1, tq), jnp.int32), cnt_init))

    k_f = jnp.full((1, tq), float(topk), F32)
    hi_u, cnt_hi = search16(k_f, jnp.zeros((1, tq), F32))
    hi_thr = (hi_u - 32768).astype(jnp.int16)
    cnt_gt = row_count(hi_thr, strict=True)

    def low_halves(c, carry):
        lo = ((key_ref[c] & 0xFFFF) - 32768).astype(jnp.int16)
        half_ref[c] = jnp.where(half_ref[c] == hi_thr, lo, lowest16)
        return carry

    lax.fori_loop(0, nkc, low_halves, 0)
    lo_u, cnt_lo = search16(k_f - cnt_gt, cnt_hi - cnt_gt)
    thr_raw = ((hi_u << 16) | lo_u) ^ jnp.int32(INT_MIN)
    short_row = thr_raw == jnp.int32(INT_MIN)
    thr = jnp.maximum(thr_raw, jnp.int32(INT_MIN + 1))
    cnt_ge = cnt_gt + cnt_lo
    tied = jnp.logical_and(jnp.logical_not(short_row), cnt_ge > k_f)
    any_tied = jnp.max(jnp.where(tied, 1.0, 0.0)) > 0.0

    _init_softmax(m_ref, acc_ref)

    def attend(c, selected, shift, tiles=((0, tq, 0, tq),)):
        bias = jnp.where(selected, 0.0 if shift is None else -shift, NEG).T
        for k0, nk, q0, nq in tiles:
            off = pl.multiple_of(c * tq + k0, nk)
            for h in range(GROUP_HEADS):
                q = q_ref[0, q0:q0 + nq, h * LANES:(h + 1) * LANES]
                k = k_ref[0, pl.ds(off, nk), h * LANES:(h + 1) * LANES]
                v = v_ref[0, pl.ds(off, nk), h * VEXT:(h + 1) * VEXT]
                s = _dot_nt(q, k) + bias[q0:q0 + nq, k0:k0 + nk]
                _softmax_chunk(s, v, m_ref, acc_ref, h, shift is None, q0)

    def attend_untied(shift):
        def attend_chunk(c, carry):
            attend(c, key_ref[c] >= thr, shift)
            return carry

        lax.fori_loop(0, qi, attend_chunk, 0)
        attend(qi, key_ref[qi] >= thr, shift, diag_tiles)

    @pl.when(jnp.logical_not(any_tied))
    def _():
        _with_score_bound(bound_ref[layer, _BOUND_COLS.index("dsa")], attend_untied)

    @pl.when(any_tied)
    def _():
        def eq_chunk(c, n):
            return n + jnp.sum(jnp.where(key_ref[c] == thr, 1.0, 0.0), axis=0, keepdims=True)

        n_eq = lax.fori_loop(0, nkc, eq_chunk, jnp.zeros((1, tq), F32))
        keep = jnp.where(short_row, 0.0, k_f - (cnt_ge - n_eq))

        def attend_chunk(c, seen):
            key = key_ref[c]
            eq = key == thr
            eq_f = jnp.where(eq, 1.0, 0.0)
            rank = seen + _dot(earlier_ref[...], eq_f.astype(BF16))
            attend(c, jnp.logical_or(key > thr, jnp.logical_and(eq, rank < keep)), None)
            return seen + jnp.sum(eq_f, axis=0, keepdims=True)

        lax.fori_loop(0, nkc, attend_chunk, jnp.zeros((1, tq), F32))

    for h in range(GROUP_HEADS):
        o_ref[0, :, h * LANES:(h + 1) * LANES] = _normalised(acc_ref, h).astype(o_ref.dtype)


def _attn_dsa(q, k, v, iq, ik, iw, score_bounds, layer, *, tq):
    b, s, _ = q.shape
    topk = min(TOPK_MAX, s // 4)
    earlier = np.asarray(np.arange(tq)[:, None] > np.arange(tq)[None, :], dtype=BF16)
    q_map = lambda bi, qi: (bi, qi, 0)
    all_map = lambda bi, qi: (bi, 0, 0)
    return pl.pallas_call(
        functools.partial(_attn_dsa_kernel, tq=tq, topk=topk, layer=layer),
        grid=(b, s // tq),
        in_specs=[pl.BlockSpec(memory_space=pltpu.SMEM),
                  pl.BlockSpec((1, tq, GROUP_WIDTH), q_map),
                  pl.BlockSpec((1, s, GROUP_WIDTH), all_map),
                  pl.BlockSpec((1, s, GROUP_HEADS * VEXT), all_map),
                  pl.BlockSpec((1, tq, IDX_HEADS * IDX_DIM), q_map),
                  pl.BlockSpec((1, s, 2 * LANES), all_map),
                  pl.BlockSpec((1, tq, LANES), q_map),
                  pl.BlockSpec((tq, tq), lambda bi, qi: (0, 0))],
        out_specs=pl.BlockSpec((1, tq, GROUP_WIDTH), q_map),
        out_shape=jax.ShapeDtypeStruct((b, s, GROUP_WIDTH), BF16),
        scratch_shapes=[pltpu.VMEM((s // tq, tq, tq), jnp.int32),
                        pltpu.VMEM((-(-(s // tq) // COUNT_UNROLL) * COUNT_UNROLL, tq, tq), jnp.int16),
                        pltpu.VMEM((GROUP_HEADS, tq, LANES), F32),
                        pltpu.VMEM((GROUP_HEADS, tq, VEXT), F32)],
        compiler_params=_cparams(2),
        name="attn_dsa",
    )(score_bounds, q, k, v, iq, ik, iw, earlier)


def _rope_tables(seq, period, half):
    inv = np.float32(ROPE_THETA) ** (-np.arange(half, dtype=np.float32) / np.float32(half))
    lane = np.arange(LANES)
    idx = lane % period
    active = idx < half
    ang = np.arange(seq, dtype=np.float32)[:, None] * inv[np.minimum(idx, half - 1)][None, :]
    cos, sin = np.cos(ang), np.sin(ang)
    return np.stack([np.where(active, cos, 1.0), np.where(active, np.where(lane < LANES // 2, -sin, sin), 0.0)]
                    ).astype(np.float32)


def _half_split_layout(dims, half, start=0):
    first = list(range(half)) + list(range(2 * half, 2 * half + (dims - 2 * half) // 2))
    second = list(range(half, 2 * half)) + list(range(2 * half + (dims - 2 * half) // 2, dims))
    lanes = [-1] * LANES
    lanes[start:start + len(first)] = first
    lanes[start + LANES // 2:start + LANES // 2 + len(second)] = second
    return lanes


_HEAD_LAYOUT = _half_split_layout(HEAD_DIM, HEAD_DIM // 8)
_SUB_LAYOUT = _half_split_layout(DIFF_SUB_DIM, DIFF_SUB_DIM // 8)
_PAIR_LAYOUT = [a if a >= 0 else (DIFF_SUB_DIM + b if b >= 0 else -1)
                for a, b in zip(_SUB_LAYOUT, _half_split_layout(DIFF_SUB_DIM, DIFF_SUB_DIM // 8, start=32))]
_ROPE64_LAYOUT = _half_split_layout(MLA_ROPE_DIM, MLA_ROPE_DIM // 2)


def _take_cols(a, layout):
    src = max(layout) + 1
    place = np.zeros((src, LANES), np.float32)
    for lane, i in enumerate(layout):
        if i >= 0:
            place[i, lane] = 1.0
    blocks = a.reshape(a.shape[:-1] + (-1, src))
    precision = lax.Precision.HIGHEST if a.dtype == F32 else None
    out = jnp.einsum("...k,kj->...j", blocks, jnp.asarray(place, a.dtype), precision=precision,
                     preferred_element_type=a.dtype)
    return out.reshape(a.shape[:-1] + (-1,))


def _pad_cols(a, width):
    return jnp.pad(a, ((0, 0),) * (a.ndim - 1) + ((0, width - a.shape[-1]),))


def _relayout_w_in(w):
    sizes = (512,) * 3 + (MLA_Q_RANK, MLA_KV_RANK, MLA_ROPE_DIM) + (512,) * 6 + (IDX_HEADS * IDX_DIM, IDX_DIM, IDX_HEADS)
    names = ("a_q", "a_k", "a_v", "b_cq", "b_ckv", "b_kr", "c_q", "c_k", "c_v", "d_q", "d_k", "d_v", "d_iq", "d_ik", "d_iw")
    w = w.astype(BF16)
    parts, start = {}, 0
    for nme, sz in zip(names, sizes):
        parts[nme] = w[..., start:start + sz]
        start += sz
    for nme in ("a_q", "a_k", "d_iq"):
        parts[nme] = _take_cols(parts[nme], _PAIR_LAYOUT)
    for nme in ("d_q", "d_k"):
        parts[nme] = _take_cols(parts[nme], _HEAD_LAYOUT)
    parts["d_ik"] = _take_cols(parts["d_ik"], _SUB_LAYOUT)
    parts["b_kr"] = _take_cols(parts["b_kr"], _ROPE64_LAYOUT)
    order = sorted(_OFF, key=_OFF.get)
    ends = [_OFF[nme] for nme in order[1:]] + [Y_WIDTH]
    return jnp.concatenate([_pad_cols(parts[nme], end - _OFF[nme]) for nme, end in zip(order, ends)], axis=-1)


def _relayout_mla_q(w):
    lead = w.shape[:-1]
    w = w.reshape(lead + (-1, MLA_QK_DIM))
    rope = _take_cols(w[..., MLA_NOPE_DIM:], _ROPE64_LAYOUT)
    return jnp.concatenate([w[..., :MLA_NOPE_DIM], rope], axis=-1).reshape(lead + (-1,))


_BOUND_COLS = ("diff", "mla", "dsa", "unused")
_GAIN_ROWS = ("a_q", "a_k", "b_cq", "b_ckv", "b_q", "b_k", "d_q", "d_k", "d_ik")


def _score_bounds(diff_qk_norm, mla_qk_norm, dsa_qk_norm):
    def bound(dim, qk_norm):
        peak = jnp.max(jnp.abs(qk_norm), axis=-1)
        return (dim ** 0.5 * LOG2E * SCORE_BOUND_MARGIN) * peak[:, 0] * peak[:, 1]

    cols = [bound(DIFF_SUB_DIM, diff_qk_norm), bound(MLA_QK_DIM, mla_qk_norm), bound(HEAD_DIM, dsa_qk_norm)]
    return jnp.stack(cols + [jnp.zeros_like(cols[0])], axis=1).astype(F32)


def _packed_gains(diff_qk_norm, mla_q_a_norm, mla_kv_a_norm, mla_qk_norm, dsa_qk_norm, idx_k_norm):
    def tiled(g, reps):
        return jnp.tile(g, (1, reps))

    rows = dict(
        a_q=_take_cols(tiled(diff_qk_norm[:, 0], 8), _PAIR_LAYOUT),
        a_k=_take_cols(tiled(diff_qk_norm[:, 1], 8), _PAIR_LAYOUT),
        b_cq=mla_q_a_norm, b_ckv=mla_kv_a_norm,
        b_q=_relayout_mla_q(tiled(mla_qk_norm[:, 0], GROUP_HEADS)),
        b_k=_relayout_mla_q(mla_qk_norm[:, 1]),
        d_q=_take_cols(tiled(dsa_qk_norm[:, 0], 4), _HEAD_LAYOUT),
        d_k=_take_cols(tiled(dsa_qk_norm[:, 1], 4), _HEAD_LAYOUT),
        d_ik=_take_cols(idx_k_norm, _SUB_LAYOUT))
    packed = jnp.stack([_pad_cols(rows[nme], 1024) for nme in _GAIN_ROWS], axis=1)
    return jnp.pad(packed, ((0, 0), (0, 16 - len(_GAIN_ROWS)), (0, 0)))


def kernel(x, attn_norm, w_in, diff_qk_norm, diff_lambda, diff_subln, mla_q_a_norm, mla_wq_b, mla_kv_a_norm,
           mla_wkv_b, mla_qk_norm, dsa_qk_norm, idx_k_norm, w_o, ffn_norm, w_gate, w_up, w_down):
    b, s, d = x.shape
    n = b * s
    depth = w_in.shape[0]
    tm = min(1024, n)
    tq = min(512, s)
    tabs = (_rope_tables(s, LANES // 4, DIFF_SUB_DIM // 8),
            _rope_tables(s, LANES // 2, MLA_ROPE_DIM // 2),
            _rope_tables(s, LANES // 2, HEAD_DIM // 8))
    sub_head = np.arange(GROUP_WIDTH) // LANES * 2 + (np.arange(GROUP_WIDTH) % (LANES // 2)) // (LANES // 4)
    ones64 = np.asarray(sub_head[:, None] == sub_head[None, :], dtype=BF16)
    tk_stick = min(256, s)
    tri = np.asarray(np.arange(tk_stick)[:, None] > np.arange(tk_stick)[None, :], dtype=BF16)

    w_in_r = _relayout_w_in(w_in)
    wq = jnp.pad(_relayout_mla_q(mla_wq_b.astype(BF16)), ((0, 0), (0, 512 - MLA_Q_RANK), (0, 0)))
    wkv = mla_wkv_b.reshape(depth, MLA_KV_RANK, GROUP_HEADS, 2, HEAD_DIM).transpose(0, 1, 3, 2, 4)
    wkv = wkv.reshape(depth, MLA_KV_RANK, 2 * GROUP_WIDTH).astype(BF16)
    gains = _packed_gains(diff_qk_norm, mla_q_a_norm, mla_kv_a_norm, mla_qk_norm, dsa_qk_norm, idx_k_norm)
    bounds = _score_bounds(diff_qk_norm, mla_qk_norm, dsa_qk_norm)
    attn_gain, ffn_gain = attn_norm.reshape(depth, 1, d), ffn_norm.reshape(depth, 1, d)
    subln = diff_subln.reshape(depth, 1, HEAD_DIM)
    w_o_b, w_down_b = w_o.astype(BF16), w_down.astype(BF16)

    xf = x.reshape(n, d)
    for l in range(depth):
        lam_init = 0.8 - 0.6 * math.exp(-0.3 * l)
        y = _rms_matmul(xf, attn_gain, w_in_r, l, tm=tm, tn=Y_WIDTH // 4, out_dtype=F32)
        (qa, ka, va, qb, kb, vb, qc, kc, vc, qd, kd, vd, iq, ik, iw) = [
            a.reshape(b, s, a.shape[1]) for a in _prep(y, s, tabs, ones64, gains, wq, wkv, l, tm=min(256, s))]

        o_a = _resident_attention(
            functools.partial(_attn_diff_kernel, tq=tq, layer=l, lam_init=lam_init), qa, ka, va,
            [], [diff_lambda, subln], l, tq=tq,
            scratch=[pltpu.VMEM((8, tq, LANES), F32), pltpu.VMEM((8, tq, VEXT), F32)], name="attn_diff",
            score_bounds=bounds)
        o_b = _resident_attention(
            functools.partial(_attn_mla_kernel, tq=tq, layer=l), qb, kb, vb, [], [], l, tq=tq,
            scratch=[pltpu.VMEM((4, tq, LANES), F32), pltpu.VMEM((4, tq, VEXT), F32)], name="attn_mla",
            score_bounds=bounds)
        o_c = _resident_attention(
            functools.partial(_attn_stick_kernel, tq=tq, tkc=tk_stick), qc, kc, vc, [tri], [], l, tq=tq,
            scratch=[pltpu.VMEM((4, tq, LANES), F32), pltpu.VMEM((4, tq, LANES), F32)], name="attn_stick")
        o_d = _attn_dsa(qd, kd, vd, iq, ik, iw, bounds, l, tq=tq)

        mixed = [o.reshape(n, GROUP_WIDTH) for o in (o_a, o_b, o_c, o_d)]
        xf = _matmul_residual(mixed, w_o_b, xf, l, tm=min(512, n), tn=d)
        act = _ffn_up(xf, ffn_gain, w_gate, w_up, l, tm=tm, tn=512)
        xf = _matmul_residual([act], w_down_b, xf, l, tm=tm, tn=512)
    return xf.reshape(b, s, d)
```

```python
import functools
import math

import jax
import jax.numpy as jnp
import numpy as np
from jax import lax
from jax.experimental import pallas as pl
from jax.experimental.pallas import tpu as pltpu

F32 = jnp.float32
BF16 = jnp.bfloat16

HEAD_DIM = 128
GROUP_HEADS = 4
GROUP_WIDTH = GROUP_HEADS * HEAD_DIM
ROPE_THETA = 500000.0
EPS = 1e-6
DIFF_SUB_DIM = 64
MLA_Q_RANK = 448
MLA_KV_RANK = 128
MLA_NOPE_DIM = 128
MLA_ROPE_DIM = 64
MLA_QK_DIM = MLA_NOPE_DIM + MLA_ROPE_DIM
MLA_PAD_DIM = 256
IDX_HEADS = 16
IDX_DIM = 64
TOPK_MAX = 256

LANES = 128
VEXT = 2 * LANES
Y_WIDTH = 13 * GROUP_WIDTH
NEG = -1e30
INT_MIN = -2147483648
LOG2E = 1.4426950408889634
COUNT_UNROLL = 2
COUNT_ACC_ROWS = 64
MAX_FIXED_SHIFT = 56.0
SCORE_BOUND_MARGIN = 1.02
STICK_DEAD_LOG2 = -160.0
VMEM_LIMIT = 56 * 1024 * 1024

_OFF = dict(a_q=0, a_k=512, a_v=1024, b_cq=1536, b_ckv=2048, b_kr=2176, d_ik=2304, d_iw=2432,
            c_q=2560, c_k=3072, c_v=3584, d_q=4096, d_k=4608, d_v=5120, d_iq=5632)


def _cparams(n_axes):
    return pltpu.CompilerParams(dimension_semantics=("arbitrary",) * n_axes, vmem_limit_bytes=VMEM_LIMIT)


def _dot(a, b):
    return jnp.dot(a, b, preferred_element_type=F32)


def _dot_nt(a, b):
    return lax.dot_general(a, b, (((1,), (1,)), ((), ())), preferred_element_type=F32)


def _rms_rows(x, g):
    ms = jnp.mean(x * x, axis=-1, keepdims=True)
    return x * lax.rsqrt(ms + EPS) * g


def _rms_matmul_kernel(x_ref, g_ref, w_ref, o_ref, h_ref):
    @pl.when(pl.program_id(1) == 0)
    def _():
        h_ref[...] = _rms_rows(x_ref[...], g_ref[...]).astype(BF16)

    o_ref[...] = _dot(h_ref[...], w_ref[...]).astype(o_ref.dtype)


def _rms_matmul(x, g, w, layer, *, tm, tn, out_dtype):
    n, d = x.shape
    m = w.shape[2]
    return pl.pallas_call(
        _rms_matmul_kernel,
        grid=(n // tm, m // tn),
        in_specs=[pl.BlockSpec((tm, d), lambda i, j: (i, 0)),
                  pl.BlockSpec((None, 1, d), lambda i, j: (layer, 0, 0)),
                  pl.BlockSpec((None, d, tn), lambda i, j: (layer, 0, j))],
        out_specs=pl.BlockSpec((tm, tn), lambda i, j: (i, j)),
        out_shape=jax.ShapeDtypeStruct((n, m), out_dtype),
        scratch_shapes=[pltpu.VMEM((tm, d), BF16)],
        compiler_params=_cparams(2),
        name="rms_matmul",
    )(x, g, w)


def _ffn_up_kernel(x_ref, g_ref, wg_ref, wu_ref, o_ref, h_ref):
    @pl.when(pl.program_id(1) == 0)
    def _():
        h_ref[...] = _rms_rows(x_ref[...], g_ref[...]).astype(BF16)

    h = h_ref[...]
    a = _dot(h, wg_ref[...].astype(BF16))
    b = _dot(h, wu_ref[...].astype(BF16))
    o_ref[...] = (jax.nn.silu(a) * b).astype(o_ref.dtype)


def _ffn_up(x, g, wg, wu, layer, *, tm, tn):
    n, d = x.shape
    m = wg.shape[2]
    return pl.pallas_call(
        _ffn_up_kernel,
        grid=(n // tm, m // tn),
        in_specs=[pl.BlockSpec((tm, d), lambda i, j: (i, 0)),
                  pl.BlockSpec((None, 1, d), lambda i, j: (layer, 0, 0)),
                  pl.BlockSpec((None, d, tn), lambda i, j: (layer, 0, j)),
                  pl.BlockSpec((None, d, tn), lambda i, j: (layer, 0, j))],
        out_specs=pl.BlockSpec((tm, tn), lambda i, j: (i, j)),
        out_shape=jax.ShapeDtypeStruct((n, m), BF16),
        scratch_shapes=[pltpu.VMEM((tm, d), BF16)],
        compiler_params=_cparams(2),
        name="ffn_up",
    )(x, g, wg, wu)


def _matmul_residual_kernel(*refs):
    a_refs, (w_ref, r_ref, o_ref) = refs[:-3], refs[-3:]
    acc = r_ref[...]
    row = 0
    for a_ref in a_refs:
        k = a_ref.shape[1]
        acc = acc + _dot(a_ref[...], w_ref[row:row + k, :])
        row += k
    o_ref[...] = acc


def _matmul_residual(parts, w, r, layer, *, tm, tn):
    n = parts[0].shape[0]
    _, k, m = w.shape
    return pl.pallas_call(
        _matmul_residual_kernel,
        grid=(n // tm, m // tn),
        in_specs=[pl.BlockSpec((tm, a.shape[1]), lambda i, j: (i, 0)) for a in parts]
                 + [pl.BlockSpec((None, k, tn), lambda i, j: (layer, 0, j)),
                    pl.BlockSpec((tm, tn), lambda i, j: (i, j))],
        out_specs=pl.BlockSpec((tm, tn), lambda i, j: (i, j)),
        out_shape=jax.ShapeDtypeStruct((n, m), F32),
        compiler_params=_cparams(2),
        name="matmul_residual",
    )(*parts, w, r)


def _rope(x, t_ref, blocks=None):
    outs = []
    for b in range(x.shape[1] // LANES):
        xb = x[:, b * LANES:(b + 1) * LANES]
        if blocks is None or b in blocks:
            xb = xb * t_ref[0] + pltpu.roll(xb, LANES // 2, 1) * t_ref[1]
        outs.append(xb)
    return outs[0] if len(outs) == 1 else jnp.concatenate(outs, axis=1)


def _rms_lane_groups(x, width, count):
    outs = []
    for h in range(x.shape[1] // width):
        xh = x[:, h * width:(h + 1) * width]
        ms = jnp.sum(xh * xh, axis=-1, keepdims=True) * (1.0 / count)
        outs.append(xh * lax.rsqrt(ms + EPS))
    return outs[0] if len(outs) == 1 else jnp.concatenate(outs, axis=1)


def _rms_sub64(x, ones_ref):
    x2 = x * x
    hi = x2.astype(BF16)
    lo = (x2 - hi.astype(F32)).astype(BF16)
    ms = (_dot(hi, ones_ref[...]) + _dot(lo, ones_ref[...])) * (1.0 / DIFF_SUB_DIM)
    return x * lax.rsqrt(ms + EPS)


def _store_vext(ref, v):
    ones = jnp.ones((v.shape[0], LANES), BF16)
    for h in range(GROUP_HEADS):
        ref[:, h * VEXT:h * VEXT + LANES] = v[:, h * LANES:(h + 1) * LANES].astype(BF16)
        ref[:, h * VEXT + LANES:(h + 1) * VEXT] = ones


def _prep_kernel(y_ref, ta_ref, tb_ref, td_ref, ones_ref, g_ref, wq_ref, wkv_ref,
                 qa_ref, ka_ref, va_ref, qb_ref, kb_ref, vb_ref, qc_ref, kc_ref, vc_ref,
                 qd_ref, kd_ref, vd_ref, iq_ref, ik_ref, iw_ref):
    def sec(name, width):
        return y_ref[:, _OFF[name]:_OFF[name] + width]

    def gain(name, width):
        row = _GAIN_ROWS.index(name)
        return g_ref[row:row + 1, :width]

    tm = y_ref.shape[0]
    lane = lax.broadcasted_iota(jnp.int32, (tm, LANES), 1)
    first_half = (lane % (LANES // 2)) < DIFF_SUB_DIM // 2

    qa = _rope(_rms_sub64(sec("a_q", 512), ones_ref) * gain("a_q", 512), ta_ref) * (DIFF_SUB_DIM ** -0.5 * LOG2E)
    for h in range(GROUP_HEADS):
        qh = qa[:, h * LANES:(h + 1) * LANES]
        qa_ref[:, (2 * h) * LANES:(2 * h + 1) * LANES] = jnp.where(first_half, qh, 0.0).astype(BF16)
        qa_ref[:, (2 * h + 1) * LANES:(2 * h + 2) * LANES] = jnp.where(first_half, 0.0, qh).astype(BF16)
    ka_ref[...] = _rope(_rms_sub64(sec("a_k", 512), ones_ref) * gain("a_k", 512), ta_ref).astype(BF16)
    _store_vext(va_ref, sec("a_v", 512))

    cq = sec("b_cq", 512)
    cq = cq * lax.rsqrt(jnp.sum(cq * cq, axis=-1, keepdims=True) * (1.0 / MLA_Q_RANK) + EPS) * gain("b_cq", 512)
    qb = _dot(cq.astype(BF16), wq_ref[...])
    qb = _rms_lane_groups(qb, MLA_PAD_DIM, MLA_QK_DIM) * gain("b_q", 1024)
    ckv = _rms_lane_groups(sec("b_ckv", 128), 128, MLA_KV_RANK) * gain("b_ckv", 128)
    kv = _dot(ckv.astype(BF16), wkv_ref[...])
    kr = sec("b_kr", 128)
    kr_ss = jnp.sum(kr * kr, axis=-1, keepdims=True)
    for h in range(GROUP_HEADS):
        lo, hi = h * MLA_PAD_DIM, (h + 1) * MLA_PAD_DIM
        qb_ref[:, lo:hi] = (_rope(qb[:, lo:hi], tb_ref, blocks=(1,)) * (MLA_QK_DIM ** -0.5 * LOG2E)).astype(BF16)
        kn = kv[:, h * LANES:(h + 1) * LANES]
        ms = (jnp.sum(kn * kn, axis=-1, keepdims=True) + kr_ss) * (1.0 / MLA_QK_DIM)
        kh = jnp.concatenate([kn, kr], axis=1) * lax.rsqrt(ms + EPS) * gain("b_k", MLA_PAD_DIM)
        kb_ref[:, lo:hi] = _rope(kh, tb_ref, blocks=(1,)).astype(BF16)
    _store_vext(vb_ref, kv[:, GROUP_WIDTH:])

    qc_ref[...] = (sec("c_q", 512) * (HEAD_DIM ** -0.5 * LOG2E)).astype(BF16)
    kc_ref[...] = sec("c_k", 512).astype(BF16)
    vc_ref[...] = sec("c_v", 512).astype(BF16)

    qd = _rope(_rms_lane_groups(sec("d_q", 512), HEAD_DIM, HEAD_DIM) * gain("d_q", 512), td_ref)
    qd_ref[...] = (qd * (HEAD_DIM ** -0.5 * LOG2E)).astype(BF16)
    kd_ref[...] = _rope(_rms_lane_groups(sec("d_k", 512), HEAD_DIM, HEAD_DIM) * gain("d_k", 512), td_ref).astype(BF16)
    _store_vext(vd_ref, sec("d_v", 512))
    for half in range(2):
        iq = y_ref[:, _OFF["d_iq"] + half * 512:_OFF["d_iq"] + (half + 1) * 512]
        iq_ref[:, half * 512:(half + 1) * 512] = (_rope(iq, ta_ref) * (IDX_DIM ** -0.5)).astype(BF16)
    ik = _rope(_rms_lane_groups(sec("d_ik", 128), 128, IDX_DIM) * gain("d_ik", 128), ta_ref)
    ik_ref[:, :LANES] = ik.astype(BF16)
    ik_ref[:, LANES:] = pltpu.roll(ik, IDX_DIM // 2, 1).astype(BF16)
    iw_ref[...] = sec("d_iw", 128) * (IDX_HEADS ** -0.5)


def _prep(y, seq, tabs, ones64, gains, wq, wkv, layer, *, tm):
    n = y.shape[0]
    nblk_seq = seq // tm
    ta, tb, td = tabs

    def rows(width):
        return pl.BlockSpec((tm, width), lambda i: (i, 0))

    def table():
        return pl.BlockSpec((2, tm, LANES), lambda i: (0, i % nblk_seq, 0))

    def whole(a):
        return pl.BlockSpec(a.shape, lambda i: (0,) * a.ndim)

    def of_layer(a):
        return pl.BlockSpec((None,) + a.shape[1:], lambda i: (layer,) + (0,) * (a.ndim - 1))

    widths = [1024, 512, 1024, 1024, 1024, 1024, 512, 512, 512, 512, 512, 1024, 1024, 256]
    out_shape = [jax.ShapeDtypeStruct((n, w), BF16) for w in widths] + [jax.ShapeDtypeStruct((n, LANES), F32)]
    out_specs = [rows(w) for w in widths] + [rows(LANES)]
    return pl.pallas_call(
        _prep_kernel,
        grid=(n // tm,),
        in_specs=[rows(Y_WIDTH), table(), table(), table(), whole(ones64), of_layer(gains), of_layer(wq), of_layer(wkv)],
        out_specs=out_specs,
        out_shape=out_shape,
        compiler_params=_cparams(1),
        name="prep",
    )(y, ta, tb, td, ones64, gains, wq, wkv)


def _softmax_chunk(s, vext, m_ref, acc_ref, idx, running_max, row0=0):
    rows = slice(row0, row0 + s.shape[0])
    if not running_max:
        acc_ref[idx, rows] += _dot(jnp.exp2(s).astype(BF16), vext)
        return
    m_prev = m_ref[idx, rows]
    m_new = jnp.maximum(m_prev, jnp.max(s, axis=1, keepdims=True))
    p = jnp.exp2(s - jnp.tile(m_new, (1, s.shape[1] // LANES)))
    alpha = jnp.exp2(m_prev - m_new)
    acc_ref[idx, rows] = jnp.tile(alpha, (1, VEXT // LANES)) * acc_ref[idx, rows] + _dot(p.astype(BF16), vext)
    m_ref[idx, rows] = m_new


def _init_softmax(m_ref, acc_ref):
    m_ref[...] = jnp.full(m_ref.shape, NEG, F32)
    acc_ref[...] = jnp.zeros(acc_ref.shape, F32)


def _with_score_bound(bound, attend):
    @pl.when(bound <= MAX_FIXED_SHIFT)
    def _():
        attend(bound)

    @pl.when(jnp.logical_not(bound <= MAX_FIXED_SHIFT))
    def _():
        attend(None)


def _normalised(acc_ref, idx):
    acc = acc_ref[idx]
    return acc[:, :LANES] / acc[:, LANES:]


def _causal_mask(row0, col0, tq, tkc, strict=False):
    rows = row0 + lax.broadcasted_iota(jnp.int32, (tq, tkc), 0)
    cols = col0 + lax.broadcasted_iota(jnp.int32, (tq, tkc), 1)
    return cols < rows if strict else cols <= rows


def _causal_chunks(chunk, qi, tq):
    def full_chunk(c, carry):
        chunk(pl.multiple_of(c * tq, tq), tq, None)
        return carry

    lax.fori_loop(0, qi, full_chunk, 0)
    half = tq // 2
    chunk(pl.multiple_of(qi * tq, tq), half, _causal_mask(0, 0, tq, half))
    chunk(pl.multiple_of(qi * tq + half, half), half, _causal_mask(0, 0, half, half), half, half)


def _attn_diff_kernel(bound_ref, lam_ref, sub_ref, q_ref, k_ref, v_ref, o_ref, m_ref, acc_ref, *, tq, layer, lam_init):
    qi = pl.program_id(1)
    _init_softmax(m_ref, acc_ref)

    def attend(shift):
        def chunk(off, keys, mask, row0=0, rows=tq):
            for mp in range(2 * GROUP_HEADS):
                h = mp // 2
                q = q_ref[0, row0:row0 + rows, mp * LANES:(mp + 1) * LANES]
                k = k_ref[0, pl.ds(off, keys), h * LANES:(h + 1) * LANES]
                v = v_ref[0, pl.ds(off, keys), h * VEXT:(h + 1) * VEXT]
                s = _dot_nt(q, k)
                if shift is not None:
                    s = s - shift
                if mask is not None:
                    s = jnp.where(mask, s, NEG)
                _softmax_chunk(s, v, m_ref, acc_ref, mp, shift is None, row0)

        _causal_chunks(chunk, qi, tq)

    _with_score_bound(bound_ref[layer, _BOUND_COLS.index("diff")], attend)

    lv = lam_ref[...]
    lam = (jnp.exp(jnp.sum(lv[0:1] * lv[1:2], axis=-1, keepdims=True))
           - jnp.exp(jnp.sum(lv[2:3] * lv[3:4], axis=-1, keepdims=True)) + lam_init)
    for h in range(GROUP_HEADS):
        o = _normalised(acc_ref, 2 * h) - lam * _normalised(acc_ref, 2 * h + 1)
        o = _rms_rows(o, sub_ref[...]) * (1.0 - lam_init)
        o_ref[0, :, h * LANES:(h + 1) * LANES] = o.astype(o_ref.dtype)


def _attn_mla_kernel(bound_ref, q_ref, k_ref, v_ref, o_ref, m_ref, acc_ref, *, tq, layer):
    qi = pl.program_id(1)
    _init_softmax(m_ref, acc_ref)

    def attend(shift):
        def chunk(off, keys, mask, row0=0, rows=tq):
            for h in range(GROUP_HEADS):
                q = q_ref[0, row0:row0 + rows, h * MLA_PAD_DIM:(h + 1) * MLA_PAD_DIM]
                k = k_ref[0, pl.ds(off, keys), h * MLA_PAD_DIM:(h + 1) * MLA_PAD_DIM]
                v = v_ref[0, pl.ds(off, keys), h * VEXT:(h + 1) * VEXT]
                s = _dot_nt(q, k)
                if shift is not None:
                    s = s - shift
                if mask is not None:
                    s = jnp.where(mask, s, NEG)
                _softmax_chunk(s, v, m_ref, acc_ref, h, shift is None, row0)

        _causal_chunks(chunk, qi, tq)

    _with_score_bound(bound_ref[layer, _BOUND_COLS.index("mla")], attend)
    for h in range(GROUP_HEADS):
        o_ref[0, :, h * LANES:(h + 1) * LANES] = _normalised(acc_ref, h).astype(o_ref.dtype)


def _attn_stick_kernel(tri_ref, q_ref, k_ref, v_ref, o_ref, carry_ref, acc_ref, *, tq, tkc):
    qi = pl.program_id(1)
    carry_ref[...] = jnp.zeros(carry_ref.shape, F32)
    acc_ref[...] = jnp.zeros(acc_ref.shape, F32)

    def chunk(off, mask):
        tri = tri_ref[...]
        for h in range(GROUP_HEADS):
            q = q_ref[0, :, h * LANES:(h + 1) * LANES]
            k = k_ref[0, pl.ds(off, tkc), h * LANES:(h + 1) * LANES]
            v = v_ref[0, pl.ds(off, tkc), h * LANES:(h + 1) * LANES]
            z = _dot_nt(q, k)
            log_b = jnp.minimum(z, 0.0) - jnp.log2(1.0 + jnp.exp2(-jnp.abs(z)))
            log_1mb = log_b - z
            if mask is not None:
                log_1mb = jnp.where(mask, log_1mb, 0.0)
            hi = log_1mb.astype(BF16)
            lo = (log_1mb - hi.astype(F32)).astype(BF16)
            inner = _dot(hi, tri) + _dot(lo, tri)
            carry = carry_ref[h]
            a = jnp.exp2(log_b + inner + jnp.tile(carry, (1, tkc // LANES)))
            if mask is not None:
                a = jnp.where(mask, a, 0.0)
            acc_ref[h] += _dot(a.astype(BF16), v)
            carry_ref[h] = carry + (inner[:, 0:1] + log_1mb[:, 0:1])

    n_diag = tq // tkc
    for j in range(n_diag):
        col0 = (n_diag - 1 - j) * tkc
        chunk(pl.multiple_of(qi * tq + col0, tkc), _causal_mask(0, col0, tq, tkc, strict=True))

    n_full = qi * n_diag

    def full_chunk(state):
        c, _ = state
        chunk(pl.multiple_of((n_full - 1 - c) * tkc, tkc), None)
        return c + 1, (jnp.max(carry_ref[...]) > STICK_DEAD_LOG2).astype(jnp.int32)

    lax.while_loop(lambda state: jnp.logical_and(state[0] < n_full, state[1] > 0), full_chunk,
                   (jnp.int32(0), jnp.int32(1)))
    for h in range(GROUP_HEADS):
        o_ref[0, :, h * LANES:(h + 1) * LANES] = acc_ref[h].astype(o_ref.dtype)


def _resident_attention(kernel, q, k, v, extra, layer_extra, layer, *, tq, scratch, name, score_bounds=None):
    b, s, _ = q.shape
    q_map = lambda bi, qi: (bi, qi, 0)
    all_map = lambda bi, qi: (bi, 0, 0)
    extra_specs = [pl.BlockSpec(e.shape, lambda bi, qi, nd=e.ndim: (0,) * nd) for e in extra]
    extra_specs += [pl.BlockSpec((None,) + e.shape[1:], lambda bi, qi, nd=e.ndim: (layer,) + (0,) * (nd - 1))
                    for e in layer_extra]
    extra = list(extra) + list(layer_extra)
    if score_bounds is not None:
        extra = [score_bounds] + extra
        extra_specs = [pl.BlockSpec(memory_space=pltpu.SMEM)] + extra_specs
    return pl.pallas_call(
        kernel,
        grid=(b, s // tq),
        in_specs=extra_specs + [pl.BlockSpec((1, tq, q.shape[2]), q_map),
                                pl.BlockSpec((1, s, k.shape[2]), all_map),
                                pl.BlockSpec((1, s, v.shape[2]), all_map)],
        out_specs=pl.BlockSpec((1, tq, GROUP_WIDTH), q_map),
        out_shape=jax.ShapeDtypeStruct((b, s, GROUP_WIDTH), BF16),
        scratch_shapes=scratch,
        compiler_params=_cparams(2),
        name=name,
    )(*extra, q, k, v)


def _attn_dsa_kernel(bound_ref, q_ref, k_ref, v_ref, iq_ref, ik_ref, iw_ref, earlier_ref, o_ref, key_ref, half_ref,
                     m_ref, acc_ref, *, tq, topk, layer):
    qi = pl.program_id(1)
    nkc = qi + 1
    iw_t = iw_ref[0].T

    one16, zero16, lowest16 = jnp.int16(1), jnp.int16(0), jnp.int16(-32768)
    half = tq // 2
    diag_tiles = ((0, half, 0, tq), (half, half, half, half))

    def score_tile(c, k0, nk, q0, nq, causal):
        off = pl.multiple_of(c * tq + k0, nk)
        ikc = ik_ref[0, pl.ds(off, nk), :]
        isc = jnp.zeros((nk, nq), F32)
        for hp in range(IDX_HEADS // 2):
            iqp = iq_ref[0, q0:q0 + nq, hp * LANES:(hp + 1) * LANES]
            for e in range(2):
                hh = 2 * hp + e
                s = _dot_nt(ikc[:, e * LANES:(e + 1) * LANES], iqp)
                isc = isc + jnp.maximum(s, 0.0) * iw_t[hh:hh + 1, q0:q0 + nq]
        bits = lax.bitcast_convert_type(isc, jnp.int32)
        key = jnp.where(bits < 0, bits ^ jnp.int32(0x7FFFFFFF), bits)
        if causal:
            key_pos = k0 + lax.broadcasted_iota(jnp.int32, (nk, nq), 0)
            query_pos = q0 + lax.broadcasted_iota(jnp.int32, (nk, nq), 1)
            key = jnp.where(key_pos <= query_pos, key, INT_MIN)
        key_ref[c, k0:k0 + nk, q0:q0 + nq] = key
        half_ref[c, k0:k0 + nk, q0:q0 + nq] = (key >> 16).astype(jnp.int16)

    def full_score_chunk(c, carry):
        score_tile(c, 0, tq, 0, tq, False)
        return carry

    lax.fori_loop(0, qi, full_score_chunk, 0)
    for tile in diag_tiles:
        score_tile(qi, *tile, True)
    key_ref[qi, half:, :half] = jnp.full((half, half), INT_MIN, jnp.int32)
    half_ref[qi, half:, :half] = jnp.full((half, half), lowest16, jnp.int16)

    @pl.when(nkc % COUNT_UNROLL == 1)
    def _():
        half_ref[nkc] = jnp.full((tq, tq), lowest16, jnp.int16)

    def row_count(bound, strict):
        def body(t, part):
            for u in range(COUNT_UNROLL):
                for r in range(tq // COUNT_ACC_ROWS):
                    half = half_ref[COUNT_UNROLL * t + u, r * COUNT_ACC_ROWS:(r + 1) * COUNT_ACC_ROWS, :]
                    part = part + jnp.where(half > bound if strict else half >= bound, one16, zero16)
            return part

        part = lax.fori_loop(0, (nkc + COUNT_UNROLL - 1) // COUNT_UNROLL, body,
                             jnp.zeros((COUNT_ACC_ROWS, tq), jnp.int16))
        return jnp.sum(part.astype(F32), axis=0, keepdims=True)

    def search16(rank, cnt_init):
        def bit(i, carry):
            thr_u, cnt_thr = carry
            cand_u = thr_u | lax.shift_left(jnp.int32(1), 15 - i)
            cnt = row_count((cand_u - 32768).astype(jnp.int16), strict=False)
            take = cnt >= rank
            return jnp.where(take, cand_u, thr_u), jnp.where(take, cnt, cnt_thr)

        return lax.fori_loop(0, 16, bit, (jnp.zeros((1, tq), jnp.int32), cnt_init))

    k_f = jnp.full((1, tq), float(topk), F32)
    hi_u, cnt_hi = search16(k_f, jnp.zeros((1, tq), F32))
    hi_thr = (hi_u - 32768).astype(jnp.int16)
    cnt_gt = row_count(hi_thr, strict=True)

    def low_halves(c, carry):
        lo = ((key_ref[c] & 0xFFFF) - 32768).astype(jnp.int16)
        half_ref[c] = jnp.where(half_ref[c] == hi_thr, lo, lowest16)
        return carry

    lax.fori_loop(0, nkc, low_halves, 0)
    lo_u, cnt_lo = search16(k_f - cnt_gt, cnt_hi - cnt_gt)
    thr_raw = ((hi_u << 16) | lo_u) ^ jnp.int32(INT_MIN)
    short_row = thr_raw == jnp.int32(INT_MIN)
    thr = jnp.maximum(thr_raw, jnp.int32(INT_MIN + 1))
    cnt_ge = cnt_gt + cnt_lo
    tied = jnp.logical_and(jnp.logical_not(short_row), cnt_ge > k_f)
    any_tied = jnp.max(jnp.where(tied, 1.0, 0.0)) > 0.0

    _init_softmax(m_ref, acc_ref)

    def attend(c, selected, shift, tiles=((0, tq, 0, tq),)):
        bias = jnp.where(selected, 0.0 if shift is None else -shift, NEG).T
        for k0, nk, q0, nq in tiles:
            off = pl.multiple_of(c * tq + k0, nk)
            for h in range(GROUP_HEADS):
                q = q_ref[0, q0:q0 + nq, h * LANES:(h + 1) * LANES]
                k = k_ref[0, pl.ds(off, nk), h * LANES:(h + 1) * LANES]
                v = v_ref[0, pl.ds(off, nk), h * VEXT:(h + 1) * VEXT]
                s = _dot_nt(q, k) + bias[q0:q0 + nq, k0:k0 + nk]
                _softmax_chunk(s, v, m_ref, acc_ref, h, shift is None, q0)

    def attend_untied(shift):
        def attend_chunk(c, carry):
            attend(c, key_ref[c] >= thr, shift)
            return carry

        lax.fori_loop(0, qi, attend_chunk, 0)
        attend(qi, key_ref[qi] >= thr, shift, diag_tiles)

    @pl.when(jnp.logical_not(any_tied))
    def _():
        _with_score_bound(bound_ref[layer, _BOUND_COLS.index("dsa")], attend_untied)

    @pl.when(any_tied)
    def _():
        def eq_chunk(c, n):
            return n + jnp.sum(jnp.where(key_ref[c] == thr, 1.0, 0.0), axis=0, keepdims=True)

        n_eq = lax.fori_loop(0, nkc, eq_chunk, jnp.zeros((1, tq), F32))
        keep = jnp.where(short_row, 0.0, k_f - (cnt_ge - n_eq))

        def attend_chunk(c, seen):
            key = key_ref[c]
            eq = key == thr
            eq_f = jnp.where(eq, 1.0, 0.0)
            rank = seen + _dot(earlier_ref[...], eq_f.astype(BF16))
            attend(c, jnp.logical_or(key > thr, jnp.logical_and(eq, rank < keep)), None)
            return seen + jnp.sum(eq_f, axis=0, keepdims=True)

        lax.fori_loop(0, nkc, attend_chunk, jnp.zeros((1, tq), F32))

    for h in range(GROUP_HEADS):
        o_ref[0, :, h * LANES:(h + 1) * LANES] = _normalised(acc_ref, h).astype(o_ref.dtype)


def _attn_dsa(q, k, v, iq, ik, iw, score_bounds, layer, *, tq):
    b, s, _ = q.shape
    topk = min(TOPK_MAX, s // 4)
    earlier = np.asarray(np.arange(tq)[:, None] > np.arange(tq)[None, :], dtype=BF16)
    q_map = lambda bi, qi: (bi, qi, 0)
    all_map = lambda bi, qi: (bi, 0, 0)
    return pl.pallas_call(
        functools.partial(_attn_dsa_kernel, tq=tq, topk=topk, layer=layer),
        grid=(b, s // tq),
        in_specs=[pl.BlockSpec(memory_space=pltpu.SMEM),
                  pl.BlockSpec((1, tq, GROUP_WIDTH), q_map),
                  pl.BlockSpec((1, s, GROUP_WIDTH), all_map),
                  pl.BlockSpec((1, s, GROUP_HEADS * VEXT), all_map),
                  pl.BlockSpec((1, tq, IDX_HEADS * IDX_DIM), q_map),
                  pl.BlockSpec((1, s, 2 * LANES), all_map),
                  pl.BlockSpec((1, tq, LANES), q_map),
                  pl.BlockSpec((tq, tq), lambda bi, qi: (0, 0))],
        out_specs=pl.BlockSpec((1, tq, GROUP_WIDTH), q_map),
        out_shape=jax.ShapeDtypeStruct((b, s, GROUP_WIDTH), BF16),
        scratch_shapes=[pltpu.VMEM((s // tq, tq, tq), jnp.int32),
                        pltpu.VMEM((-(-(s // tq) // COUNT_UNROLL) * COUNT_UNROLL, tq, tq), jnp.int16),
                        pltpu.VMEM((GROUP_HEADS, tq, LANES), F32),
                        pltpu.VMEM((GROUP_HEADS, tq, VEXT), F32)],
        compiler_params=_cparams(2),
        name="attn_dsa",
    )(score_bounds, q, k, v, iq, ik, iw, earlier)


def _rope_tables(seq, period, half):
    inv = np.float32(ROPE_THETA) ** (-np.arange(half, dtype=np.float32) / np.float32(half))
    lane = np.arange(LANES)
    idx = lane % period
    active = idx < half
    ang = np.arange(seq, dtype=np.float32)[:, None] * inv[np.minimum(idx, half - 1)][None, :]
    cos, sin = np.cos(ang), np.sin(ang)
    return np.stack([np.where(active, cos, 1.0), np.where(active, np.where(lane < LANES // 2, -sin, sin), 0.0)]
                    ).astype(np.float32)


def _half_split_layout(dims, half, start=0):
    first = list(range(half)) + list(range(2 * half, 2 * half + (dims - 2 * half) // 2))
    second = list(range(half, 2 * half)) + list(range(2 * half + (dims - 2 * half) // 2, dims))
    lanes = [-1] * LANES
    lanes[start:start + len(first)] = first
    lanes[start + LANES // 2:start + LANES // 2 + len(second)] = second
    return lanes


_HEAD_LAYOUT = _half_split_layout(HEAD_DIM, HEAD_DIM // 8)
_SUB_LAYOUT = _half_split_layout(DIFF_SUB_DIM, DIFF_SUB_DIM // 8)
_PAIR_LAYOUT = [a if a >= 0 else (DIFF_SUB_DIM + b if b >= 0 else -1)
                for a, b in zip(_SUB_LAYOUT, _half_split_layout(DIFF_SUB_DIM, DIFF_SUB_DIM // 8, start=32))]
_ROPE64_LAYOUT = _half_split_layout(MLA_ROPE_DIM, MLA_ROPE_DIM // 2)


def _take_cols(a, layout):
    src = max(layout) + 1
    place = np.zeros((src, LANES), np.float32)
    for lane, i in enumerate(layout):
        if i >= 0:
            place[i, lane] = 1.0
    blocks = a.reshape(a.shape[:-1] + (-1, src))
    precision = lax.Precision.HIGHEST if a.dtype == F32 else None
    out = jnp.einsum("...k,kj->...j", blocks, jnp.asarray(place, a.dtype), precision=precision,
                     preferred_element_type=a.dtype)
    return out.reshape(a.shape[:-1] + (-1,))


def _pad_cols(a, width):
    return jnp.pad(a, ((0, 0),) * (a.ndim - 1) + ((0, width - a.shape[-1]),))


def _relayout_w_in(w):
    sizes = (512,) * 3 + (MLA_Q_RANK, MLA_KV_RANK, MLA_ROPE_DIM) + (512,) * 6 + (IDX_HEADS * IDX_DIM, IDX_DIM, IDX_HEADS)
    names = ("a_q", "a_k", "a_v", "b_cq", "b_ckv", "b_kr", "c_q", "c_k", "c_v", "d_q", "d_k", "d_v", "d_iq", "d_ik", "d_iw")
    w = w.astype(BF16)
    parts, start = {}, 0
    for nme, sz in zip(names, sizes):
        parts[nme] = w[..., start:start + sz]
        start += sz
    for nme in ("a_q", "a_k", "d_iq"):
        parts[nme] = _take_cols(parts[nme], _PAIR_LAYOUT)
    for nme in ("d_q", "d_k"):
        parts[nme] = _take_cols(parts[nme], _HEAD_LAYOUT)
    parts["d_ik"] = _take_cols(parts["d_ik"], _SUB_LAYOUT)
    parts["b_kr"] = _take_cols(parts["b_kr"], _ROPE64_LAYOUT)
    order = sorted(_OFF, key=_OFF.get)
    ends = [_OFF[nme] for nme in order[1:]] + [Y_WIDTH]
    return jnp.concatenate([_pad_cols(parts[nme], end - _OFF[nme]) for nme, end in zip(order, ends)], axis=-1)


def _relayout_mla_q(w):
    lead = w.shape[:-1]
    w = w.reshape(lead + (-1, MLA_QK_DIM))
    rope = _take_cols(w[..., MLA_NOPE_DIM:], _ROPE64_LAYOUT)
    return jnp.concatenate([w[..., :MLA_NOPE_DIM], rope], axis=-1).reshape(lead + (-1,))


_BOUND_COLS = ("diff", "mla", "dsa", "unused")
_GAIN_ROWS = ("a_q", "a_k", "b_cq", "b_ckv", "b_q", "b_k", "d_q", "d_k", "d_ik")


def _score_bounds(diff_qk_norm, mla_qk_norm, dsa_qk_norm):
    def bound(dim, qk_norm):
        peak = jnp.max(jnp.abs(qk_norm), axis=-1)
        return (dim ** 0.5 * LOG2E * SCORE_BOUND_MARGIN) * peak[:, 0] * peak[:, 1]

    cols = [bound(DIFF_SUB_DIM, diff_qk_norm), bound(MLA_QK_DIM, mla_qk_norm), bound(HEAD_DIM, dsa_qk_norm)]
    return jnp.stack(cols + [jnp.zeros_like(cols[0])], axis=1).astype(F32)


def _packed_gains(diff_qk_norm, mla_q_a_norm, mla_kv_a_norm, mla_qk_norm, dsa_qk_norm, idx_k_norm):
    def tiled(g, reps):
        return jnp.tile(g, (1, reps))

    rows = dict(
        a_q=_take_cols(tiled(diff_qk_norm[:, 0], 8), _PAIR_LAYOUT),
        a_k=_take_cols(tiled(diff_qk_norm[:, 1], 8), _PAIR_LAYOUT),
        b_cq=mla_q_a_norm, b_ckv=mla_kv_a_norm,
        b_q=_relayout_mla_q(tiled(mla_qk_norm[:, 0], GROUP_HEADS)),
        b_k=_relayout_mla_q(mla_qk_norm[:, 1]),
        d_q=_take_cols(tiled(dsa_qk_norm[:, 0], 4), _HEAD_LAYOUT),
        d_k=_take_cols(tiled(dsa_qk_norm[:, 1], 4), _HEAD_LAYOUT),
        d_ik=_take_cols(idx_k_norm, _SUB_LAYOUT))
    packed = jnp.stack([_pad_cols(rows[nme], 1024) for nme in _GAIN_ROWS], axis=1)
    return jnp.pad(packed, ((0, 0), (0, 16 - len(_GAIN_ROWS)), (0, 0)))


def kernel(x, attn_norm, w_in, diff_qk_norm, diff_lambda, diff_subln, mla_q_a_norm, mla_wq_b, mla_kv_a_norm,
           mla_wkv_b, mla_qk_norm, dsa_qk_norm, idx_k_norm, w_o, ffn_norm, w_gate, w_up, w_down):
    b, s, d = x.shape
    n = b * s
    depth = w_in.shape[0]
    tm = min(1024, n)
    tq = min(512, s)
    tabs = (_rope_tables(s, LANES // 4, DIFF_SUB_DIM // 8),
            _rope_tables(s, LANES // 2, MLA_ROPE_DIM // 2),
            _rope_tables(s, LANES // 2, HEAD_DIM // 8))
    sub_head = np.arange(GROUP_WIDTH) // LANES * 2 + (np.arange(GROUP_WIDTH) % (LANES // 2)) // (LANES // 4)
    ones64 = np.asarray(sub_head[:, None] == sub_head[None, :], dtype=BF16)
    tk_stick = min(256, s)
    tri = np.asarray(np.arange(tk_stick)[:, None] > np.arange(tk_stick)[None, :], dtype=BF16)

    w_in_r = _relayout_w_in(w_in)
    wq = jnp.pad(_relayout_mla_q(mla_wq_b.astype(BF16)), ((0, 0), (0, 512 - MLA_Q_RANK), (0, 0)))
    wkv = mla_wkv_b.reshape(depth, MLA_KV_RANK, GROUP_HEADS, 2, HEAD_DIM).transpose(0, 1, 3, 2, 4)
    wkv = wkv.reshape(depth, MLA_KV_RANK, 2 * GROUP_WIDTH).astype(BF16)
    gains = _packed_gains(diff_qk_norm, mla_q_a_norm, mla_kv_a_norm, mla_qk_norm, dsa_qk_norm, idx_k_norm)
    bounds = _score_bounds(diff_qk_norm, mla_qk_norm, dsa_qk_norm)
    attn_gain, ffn_gain = attn_norm.reshape(depth, 1, d), ffn_norm.reshape(depth, 1, d)
    subln = diff_subln.reshape(depth, 1, HEAD_DIM)
    w_o_b, w_down_b = w_o.astype(BF16), w_down.astype(BF16)

    xf = x.reshape(n, d)
    for l in range(depth):
        lam_init = 0.8 - 0.6 * math.exp(-0.3 * l)
        y = _rms_matmul(xf, attn_gain, w_in_r, l, tm=tm, tn=Y_WIDTH // 4, out_dtype=F32)
        (qa, ka, va, qb, kb, vb, qc, kc, vc, qd, kd, vd, iq, ik, iw) = [
            a.reshape(b, s, a.shape[1]) for a in _prep(y, s, tabs, ones64, gains, wq, wkv, l, tm=min(256, s))]

        o_a = _resident_attention(
            functools.partial(_attn_diff_kernel, tq=tq, layer=l, lam_init=lam_init), qa, ka, va,
            [], [diff_lambda, subln], l, tq=tq,
            scratch=[pltpu.VMEM((8, tq, LANES), F32), pltpu.VMEM((8, tq, VEXT), F32)], name="attn_diff",
            score_bounds=bounds)
        o_b = _resident_attention(
            functools.partial(_attn_mla_kernel, tq=tq, layer=l), qb, kb, vb, [], [], l, tq=tq,
            scratch=[pltpu.VMEM((4, tq, LANES), F32), pltpu.VMEM((4, tq, VEXT), F32)], name="attn_mla",
            score_bounds=bounds)
        o_c = _resident_attention(
            functools.partial(_attn_stick_kernel, tq=tk_stick, tkc=tk_stick), qc, kc, vc, [tri], [], l, tq=tk_stick,
            scratch=[pltpu.VMEM((4, tk_stick, LANES), F32), pltpu.VMEM((4, tk_stick, LANES), F32)],
            name="attn_stick")
        o_d = _attn_dsa(qd, kd, vd, iq, ik, iw, bounds, l, tq=tq)

        mixed = [o.reshape(n, GROUP_WIDTH) for o in (o_a, o_b, o_c, o_d)]
        xf = _matmul_residual(mixed, w_o_b, xf, l, tm=min(512, n), tn=d)
        act = _ffn_up(xf, ffn_gain, w_gate, w_up, l, tm=tm, tn=512)
        xf = _matmul_residual([act], w_down_b, xf, l, tm=tm, tn=512)
    return xf.reshape(b, s, d)
```

```python
import functools
import math

import jax
import jax.numpy as jnp
import numpy as np
from jax import lax
from jax.experimental import pallas as pl
from jax.experimental.pallas import tpu as pltpu

F32 = jnp.float32
BF16 = jnp.bfloat16

HEAD_DIM = 128
GROUP_HEADS = 4
GROUP_WIDTH = GROUP_HEADS * HEAD_DIM
ROPE_THETA = 500000.0
EPS = 1e-6
DIFF_SUB_DIM = 64
MLA_Q_RANK = 448
MLA_KV_RANK = 128
MLA_NOPE_DIM = 128
MLA_ROPE_DIM = 64
MLA_QK_DIM = MLA_NOPE_DIM + MLA_ROPE_DIM
MLA_PAD_DIM = 256
IDX_HEADS = 16
IDX_DIM = 64
TOPK_MAX = 256

LANES = 128
VEXT = 2 * LANES
Y_WIDTH = 13 * GROUP_WIDTH
NEG = -1e30
INT_MIN = -2147483648
LOG2E = 1.4426950408889634
COUNT_UNROLL = 2
COUNT_ACC_ROWS = 64
MAX_FIXED_SHIFT = 56.0
SCORE_BOUND_MARGIN = 1.02
STICK_DEAD_LOG2 = -160.0
VMEM_LIMIT = 56 * 1024 * 1024

_OFF = dict(a_q=0, a_k=512, a_v=1024, b_cq=1536, b_ckv=2048, b_kr=2176, d_ik=2304, d_iw=2432,
            c_q=2560, c_k=3072, c_v=3584, d_q=4096, d_k=4608, d_v=5120, d_iq=5632)


def _cparams(n_axes):
    return pltpu.CompilerParams(dimension_semantics=("arbitrary",) * n_axes, vmem_limit_bytes=VMEM_LIMIT)


def _dot(a, b):
    return jnp.dot(a, b, preferred_element_type=F32)


def _dot_nt(a, b):
    return lax.dot_general(a, b, (((1,), (1,)), ((), ())), preferred_element_type=F32)


def _rms_rows(x, g):
    ms = jnp.mean(x * x, axis=-1, keepdims=True)
    return x * lax.rsqrt(ms + EPS) * g


def _rms_matmul_kernel(x_ref, g_ref, w_ref, o_ref, h_ref):
    @pl.when(pl.program_id(1) == 0)
    def _():
        h_ref[...] = _rms_rows(x_ref[...], g_ref[...]).astype(BF16)

    o_ref[...] = _dot(h_ref[...], w_ref[...]).astype(o_ref.dtype)


def _rms_matmul(x, g, w, layer, *, tm, tn, out_dtype):
    n, d = x.shape
    m = w.shape[2]
    return pl.pallas_call(
        _rms_matmul_kernel,
        grid=(n // tm, m // tn),
        in_specs=[pl.BlockSpec((tm, d), lambda i, j: (i, 0)),
                  pl.BlockSpec((None, 1, d), lambda i, j: (layer, 0, 0)),
                  pl.BlockSpec((None, d, tn), lambda i, j: (layer, 0, j))],
        out_specs=pl.BlockSpec((tm, tn), lambda i, j: (i, j)),
        out_shape=jax.ShapeDtypeStruct((n, m), out_dtype),
        scratch_shapes=[pltpu.VMEM((tm, d), BF16)],
        compiler_params=_cparams(2),
        name="rms_matmul",
    )(x, g, w)


def _ffn_up_kernel(x_ref, g_ref, wg_ref, wu_ref, o_ref, h_ref):
    @pl.when(pl.program_id(1) == 0)
    def _():
        h_ref[...] = _rms_rows(x_ref[...], g_ref[...]).astype(BF16)

    h = h_ref[...]
    a = _dot(h, wg_ref[...].astype(BF16))
    b = _dot(h, wu_ref[...].astype(BF16))
    o_ref[...] = (jax.nn.silu(a) * b).astype(o_ref.dtype)


def _ffn_up(x, g, wg, wu, layer, *, tm, tn):
    n, d = x.shape
    m = wg.shape[2]
    return pl.pallas_call(
        _ffn_up_kernel,
        grid=(n // tm, m // tn),
        in_specs=[pl.BlockSpec((tm, d), lambda i, j: (i, 0)),
                  pl.BlockSpec((None, 1, d), lambda i, j: (layer, 0, 0)),
                  pl.BlockSpec((None, d, tn), lambda i, j: (layer, 0, j)),
                  pl.BlockSpec((None, d, tn), lambda i, j: (layer, 0, j))],
        out_specs=pl.BlockSpec((tm, tn), lambda i, j: (i, j)),
        out_shape=jax.ShapeDtypeStruct((n, m), BF16),
        scratch_shapes=[pltpu.VMEM((tm, d), BF16)],
        compiler_params=_cparams(2),
        name="ffn_up",
    )(x, g, wg, wu)


def _matmul_residual_kernel(*refs):
    a_refs, (w_ref, r_ref, o_ref) = refs[:-3], refs[-3:]
    acc = r_ref[...]
    row = 0
    for a_ref in a_refs:
        k = a_ref.shape[1]
        acc = acc + _dot(a_ref[...], w_ref[row:row + k, :])
        row += k
    o_ref[...] = acc


def _matmul_residual(parts, w, r, layer, *, tm, tn):
    n = parts[0].shape[0]
    _, k, m = w.shape
    return pl.pallas_call(
        _matmul_residual_kernel,
        grid=(n // tm, m // tn),
        in_specs=[pl.BlockSpec((tm, a.shape[1]), lambda i, j: (i, 0)) for a in parts]
                 + [pl.BlockSpec((None, k, tn), lambda i, j: (layer, 0, j)),
                    pl.BlockSpec((tm, tn), lambda i, j: (i, j))],
        out_specs=pl.BlockSpec((tm, tn), lambda i, j: (i, j)),
        out_shape=jax.ShapeDtypeStruct((n, m), F32),
        compiler_params=_cparams(2),
        name="matmul_residual",
    )(*parts, w, r)


def _rope(x, t_ref, blocks=None):
    outs = []
    for b in range(x.shape[1] // LANES):
        xb = x[:, b * LANES:(b + 1) * LANES]
        if blocks is None or b in blocks:
            xb = xb * t_ref[0] + pltpu.roll(xb, LANES // 2, 1) * t_ref[1]
        outs.append(xb)
    return outs[0] if len(outs) == 1 else jnp.concatenate(outs, axis=1)


def _rms_lane_groups(x, width, count):
    outs = []
    for h in range(x.shape[1] // width):
        xh = x[:, h * width:(h + 1) * width]
        ms = jnp.sum(xh * xh, axis=-1, keepdims=True) * (1.0 / count)
        outs.append(xh * lax.rsqrt(ms + EPS))
    return outs[0] if len(outs) == 1 else jnp.concatenate(outs, axis=1)


def _rms_sub64(x, ones_ref):
    x2 = x * x
    hi = x2.astype(BF16)
    lo = (x2 - hi.astype(F32)).astype(BF16)
    ms = (_dot(hi, ones_ref[...]) + _dot(lo, ones_ref[...])) * (1.0 / DIFF_SUB_DIM)
    return x * lax.rsqrt(ms + EPS)


def _store_vext(ref, v):
    ones = jnp.ones((v.shape[0], LANES), BF16)
    for h in range(GROUP_HEADS):
        ref[:, h * VEXT:h * VEXT + LANES] = v[:, h * LANES:(h + 1) * LANES].astype(BF16)
        ref[:, h * VEXT + LANES:(h + 1) * VEXT] = ones


def _prep_kernel(y_ref, ta_ref, tb_ref, td_ref, ones_ref, g_ref, wq_ref, wkv_ref,
                 qa_ref, ka_ref, va_ref, qb_ref, kb_ref, vb_ref, qc_ref, kc_ref, vc_ref,
                 qd_ref, kd_ref, vd_ref, iq_ref, ik_ref, iw_ref):
    def sec(name, width):
        return y_ref[:, _OFF[name]:_OFF[name] + width]

    def gain(name, width):
        row = _GAIN_ROWS.index(name)
        return g_ref[row:row + 1, :width]

    tm = y_ref.shape[0]
    lane = lax.broadcasted_iota(jnp.int32, (tm, LANES), 1)
    first_half = (lane % (LANES // 2)) < DIFF_SUB_DIM // 2

    qa = _rope(_rms_sub64(sec("a_q", 512), ones_ref) * gain("a_q", 512), ta_ref) * (DIFF_SUB_DIM ** -0.5 * LOG2E)
    for h in range(GROUP_HEADS):
        qh = qa[:, h * LANES:(h + 1) * LANES]
        qa_ref[:, (2 * h) * LANES:(2 * h + 1) * LANES] = jnp.where(first_half, qh, 0.0).astype(BF16)
        qa_ref[:, (2 * h + 1) * LANES:(2 * h + 2) * LANES] = jnp.where(first_half, 0.0, qh).astype(BF16)
    ka_ref[...] = _rope(_rms_sub64(sec("a_k", 512), ones_ref) * gain("a_k", 512), ta_ref).astype(BF16)
    _store_vext(va_ref, sec("a_v", 512))

    cq = sec("b_cq", 512)
    cq = cq * lax.rsqrt(jnp.sum(cq * cq, axis=-1, keepdims=True) * (1.0 / MLA_Q_RANK) + EPS) * gain("b_cq", 512)
    qb = _dot(cq.astype(BF16), wq_ref[...])
    qb = _rms_lane_groups(qb, MLA_PAD_DIM, MLA_QK_DIM) * gain("b_q", 1024)
    ckv = _rms_lane_groups(sec("b_ckv", 128), 128, MLA_KV_RANK) * gain("b_ckv", 128)
    kv = _dot(ckv.astype(BF16), wkv_ref[...])
    kr = sec("b_kr", 128)
    kr_ss = jnp.sum(kr * kr, axis=-1, keepdims=True)
    for h in range(GROUP_HEADS):
        lo, hi = h * MLA_PAD_DIM, (h + 1) * MLA_PAD_DIM
        qb_ref[:, lo:hi] = (_rope(qb[:, lo:hi], tb_ref, blocks=(1,)) * (MLA_QK_DIM ** -0.5 * LOG2E)).astype(BF16)
        kn = kv[:, h * LANES:(h + 1) * LANES]
        ms = (jnp.sum(kn * kn, axis=-1, keepdims=True) + kr_ss) * (1.0 / MLA_QK_DIM)
        kh = jnp.concatenate([kn, kr], axis=1) * lax.rsqrt(ms + EPS) * gain("b_k", MLA_PAD_DIM)
        kb_ref[:, lo:hi] = _rope(kh, tb_ref, blocks=(1,)).astype(BF16)
    _store_vext(vb_ref, kv[:, GROUP_WIDTH:])

    qc_ref[...] = (sec("c_q", 512) * (HEAD_DIM ** -0.5 * LOG2E)).astype(BF16)
    kc_ref[...] = sec("c_k", 512).astype(BF16)
    vc_ref[...] = sec("c_v", 512).astype(BF16)

    qd = _rope(_rms_lane_groups(sec("d_q", 512), HEAD_DIM, HEAD_DIM) * gain("d_q", 512), td_ref)
    qd_ref[...] = (qd * (HEAD_DIM ** -0.5 * LOG2E)).astype(BF16)
    kd_ref[...] = _rope(_rms_lane_groups(sec("d_k", 512), HEAD_DIM, HEAD_DIM) * gain("d_k", 512), td_ref).astype(BF16)
    _store_vext(vd_ref, sec("d_v", 512))
    for half in range(2):
        iq = y_ref[:, _OFF["d_iq"] + half * 512:_OFF["d_iq"] + (half + 1) * 512]
        iq_ref[:, half * 512:(half + 1) * 512] = (_rope(iq, ta_ref) * (IDX_DIM ** -0.5)).astype(BF16)
    ik = _rope(_rms_lane_groups(sec("d_ik", 128), 128, IDX_DIM) * gain("d_ik", 128), ta_ref)
    ik_ref[:, :LANES] = ik.astype(BF16)
    ik_ref[:, LANES:] = pltpu.roll(ik, IDX_DIM // 2, 1).astype(BF16)
    iw_ref[...] = sec("d_iw", 128) * (IDX_HEADS ** -0.5)


def _prep(y, seq, tabs, ones64, gains, wq, wkv, layer, *, tm):
    n = y.shape[0]
    nblk_seq = seq // tm
    ta, tb, td = tabs

    def rows(width):
        return pl.BlockSpec((tm, width), lambda i: (i, 0))

    def table():
        return pl.BlockSpec((2, tm, LANES), lambda i: (0, i % nblk_seq, 0))

    def whole(a):
        return pl.BlockSpec(a.shape, lambda i: (0,) * a.ndim)

    def of_layer(a):
        return pl.BlockSpec((None,) + a.shape[1:], lambda i: (layer,) + (0,) * (a.ndim - 1))

    widths = [1024, 512, 1024, 1024, 1024, 1024, 512, 512, 512, 512, 512, 1024, 1024, 256]
    out_shape = [jax.ShapeDtypeStruct((n, w), BF16) for w in widths] + [jax.ShapeDtypeStruct((n, LANES), F32)]
    out_specs = [rows(w) for w in widths] + [rows(LANES)]
    return pl.pallas_call(
        _prep_kernel,
        grid=(n // tm,),
        in_specs=[rows(Y_WIDTH), table(), table(), table(), whole(ones64), of_layer(gains), of_layer(wq), of_layer(wkv)],
        out_specs=out_specs,
        out_shape=out_shape,
        compiler_params=_cparams(1),
        name="prep",
    )(y, ta, tb, td, ones64, gains, wq, wkv)


def _softmax_chunk(s, vext, m_ref, acc_ref, idx, running_max, row0=0):
    rows = slice(row0, row0 + s.shape[0])
    if not running_max:
        acc_ref[idx, rows] += _dot(jnp.exp2(s).astype(BF16), vext)
        return
    m_prev = m_ref[idx, rows]
    m_new = jnp.maximum(m_prev, jnp.max(s, axis=1, keepdims=True))
    p = jnp.exp2(s - jnp.tile(m_new, (1, s.shape[1] // LANES)))
    alpha = jnp.exp2(m_prev - m_new)
    acc_ref[idx, rows] = jnp.tile(alpha, (1, VEXT // LANES)) * acc_ref[idx, rows] + _dot(p.astype(BF16), vext)
    m_ref[idx, rows] = m_new


def _init_softmax(m_ref, acc_ref):
    m_ref[...] = jnp.full(m_ref.shape, NEG, F32)
    acc_ref[...] = jnp.zeros(acc_ref.shape, F32)


def _with_score_bound(bound, attend):
    @pl.when(bound <= MAX_FIXED_SHIFT)
    def _():
        attend(bound)

    @pl.when(jnp.logical_not(bound <= MAX_FIXED_SHIFT))
    def _():
        attend(None)


def _normalised(acc_ref, idx):
    acc = acc_ref[idx]
    return acc[:, :LANES] / acc[:, LANES:]


def _causal_mask(row0, col0, tq, tkc, strict=False):
    rows = row0 + lax.broadcasted_iota(jnp.int32, (tq, tkc), 0)
    cols = col0 + lax.broadcasted_iota(jnp.int32, (tq, tkc), 1)
    return cols < rows if strict else cols <= rows


def _for_each_chunk(n, body):
    def pair(t, carry):
        body(2 * t)
        body(2 * t + 1)
        return carry

    lax.fori_loop(0, n // 2, pair, 0)

    @pl.when(n % 2 == 1)
    def _():
        body(n - 1)


def _causal_chunks(chunk, qi, tq):
    _for_each_chunk(qi, lambda c: chunk(pl.multiple_of(c * tq, tq), tq, None))
    half = tq // 2
    chunk(pl.multiple_of(qi * tq, tq), half, _causal_mask(0, 0, tq, half))
    chunk(pl.multiple_of(qi * tq + half, half), half, _causal_mask(0, 0, half, half), half, half)


def _attn_diff_kernel(bound_ref, lam_ref, sub_ref, q_ref, k_ref, v_ref, o_ref, m_ref, acc_ref, *, tq, layer, lam_init):
    qi = pl.program_id(1)
    _init_softmax(m_ref, acc_ref)

    def attend(shift):
        def chunk(off, keys, mask, row0=0, rows=tq):
            for mp in range(2 * GROUP_HEADS):
                h = mp // 2
                q = q_ref[0, row0:row0 + rows, mp * LANES:(mp + 1) * LANES]
                k = k_ref[0, pl.ds(off, keys), h * LANES:(h + 1) * LANES]
                v = v_ref[0, pl.ds(off, keys), h * VEXT:(h + 1) * VEXT]
                s = _dot_nt(q, k)
                if shift is not None:
                    s = s - shift
                if mask is not None:
                    s = jnp.where(mask, s, NEG)
                _softmax_chunk(s, v, m_ref, acc_ref, mp, shift is None, row0)

        _causal_chunks(chunk, qi, tq)

    _with_score_bound(bound_ref[layer, _BOUND_COLS.index("diff")], attend)

    lv = lam_ref[...]
    lam = (jnp.exp(jnp.sum(lv[0:1] * lv[1:2], axis=-1, keepdims=True))
           - jnp.exp(jnp.sum(lv[2:3] * lv[3:4], axis=-1, keepdims=True)) + lam_init)
    for h in range(GROUP_HEADS):
        o = _normalised(acc_ref, 2 * h) - lam * _normalised(acc_ref, 2 * h + 1)
        o = _rms_rows(o, sub_ref[...]) * (1.0 - lam_init)
        o_ref[0, :, h * LANES:(h + 1) * LANES] = o.astype(o_ref.dtype)


def _attn_mla_kernel(bound_ref, q_ref, k_ref, v_ref, o_ref, m_ref, acc_ref, *, tq, layer):
    qi = pl.program_id(1)
    _init_softmax(m_ref, acc_ref)

    def attend(shift):
        def chunk(off, keys, mask, row0=0, rows=tq):
            for h in range(GROUP_HEADS):
                q = q_ref[0, row0:row0 + rows, h * MLA_PAD_DIM:(h + 1) * MLA_PAD_DIM]
                k = k_ref[0, pl.ds(off, keys), h * MLA_PAD_DIM:(h + 1) * MLA_PAD_DIM]
                v = v_ref[0, pl.ds(off, keys), h * VEXT:(h + 1) * VEXT]
                s = _dot_nt(q, k)
                if shift is not None:
                    s = s - shift
                if mask is not None:
                    s = jnp.where(mask, s, NEG)
                _softmax_chunk(s, v, m_ref, acc_ref, h, shift is None, row0)

        _causal_chunks(chunk, qi, tq)

    _with_score_bound(bound_ref[layer, _BOUND_COLS.index("mla")], attend)
    for h in range(GROUP_HEADS):
        o_ref[0, :, h * LANES:(h + 1) * LANES] = _normalised(acc_ref, h).astype(o_ref.dtype)


def _attn_stick_kernel(tri_ref, q_ref, k_ref, v_ref, o_ref, carry_ref, acc_ref, *, tq, tkc):
    qi = pl.program_id(1)
    carry_ref[...] = jnp.zeros(carry_ref.shape, F32)
    acc_ref[...] = jnp.zeros(acc_ref.shape, F32)

    def chunk(off, mask, row0=0):
        rows = slice(row0, tq)
        tri = tri_ref[...]
        for h in range(GROUP_HEADS):
            q = q_ref[0, rows, h * LANES:(h + 1) * LANES]
            k = k_ref[0, pl.ds(off, tkc), h * LANES:(h + 1) * LANES]
            v = v_ref[0, pl.ds(off, tkc), h * LANES:(h + 1) * LANES]
            z = _dot_nt(q, k)
            log_b = jnp.minimum(z, 0.0) - jnp.log2(1.0 + jnp.exp2(-jnp.abs(z)))
            log_1mb = log_b - z
            if mask is not None:
                log_1mb = jnp.where(mask, log_1mb, 0.0)
            hi = log_1mb.astype(BF16)
            lo = (log_1mb - hi.astype(F32)).astype(BF16)
            inner = _dot(hi, tri) + _dot(lo, tri)
            carry = carry_ref[h, rows]
            a = jnp.exp2(log_b + inner + jnp.tile(carry, (1, tkc // LANES)))
            if mask is not None:
                a = jnp.where(mask, a, 0.0)
            acc_ref[h, rows] += _dot(a.astype(BF16), v)
            carry_ref[h, rows] = carry + (inner[:, 0:1] + log_1mb[:, 0:1])

    n_diag = tq // tkc
    for j in range(n_diag):
        col0 = (n_diag - 1 - j) * tkc
        chunk(pl.multiple_of(qi * tq + col0, tkc), _causal_mask(col0, col0, tq - col0, tkc, strict=True), col0)

    n_full = qi * n_diag

    def full_chunk(state):
        c, _ = state
        chunk(pl.multiple_of((n_full - 1 - c) * tkc, tkc), None)
        return c + 1, (jnp.max(carry_ref[...]) > STICK_DEAD_LOG2).astype(jnp.int32)

    lax.while_loop(lambda state: jnp.logical_and(state[0] < n_full, state[1] > 0), full_chunk,
                   (jnp.int32(0), jnp.int32(1)))
    for h in range(GROUP_HEADS):
        o_ref[0, :, h * LANES:(h + 1) * LANES] = acc_ref[h].astype(o_ref.dtype)


def _resident_attention(kernel, q, k, v, extra, layer_extra, layer, *, tq, scratch, name, score_bounds=None):
    b, s, _ = q.shape
    q_map = lambda bi, qi: (bi, qi, 0)
    all_map = lambda bi, qi: (bi, 0, 0)
    extra_specs = [pl.BlockSpec(e.shape, lambda bi, qi, nd=e.ndim: (0,) * nd) for e in extra]
    extra_specs += [pl.BlockSpec((None,) + e.shape[1:], lambda bi, qi, nd=e.ndim: (layer,) + (0,) * (nd - 1))
                    for e in layer_extra]
    extra = list(extra) + list(layer_extra)
    if score_bounds is not None:
        extra = [score_bounds] + extra
        extra_specs = [pl.BlockSpec(memory_space=pltpu.SMEM)] + extra_specs
    return pl.pallas_call(
        kernel,
        grid=(b, s // tq),
        in_specs=extra_specs + [pl.BlockSpec((1, tq, q.shape[2]), q_map),
                                pl.BlockSpec((1, s, k.shape[2]), all_map),
                                pl.BlockSpec((1, s, v.shape[2]), all_map)],
        out_specs=pl.BlockSpec((1, tq, GROUP_WIDTH), q_map),
        out_shape=jax.ShapeDtypeStruct((b, s, GROUP_WIDTH), BF16),
        scratch_shapes=scratch,
        compiler_params=_cparams(2),
        name=name,
    )(*extra, q, k, v)


def _attn_dsa_kernel(bound_ref, q_ref, k_ref, v_ref, iq_ref, ik_ref, iw_ref, earlier_ref, o_ref, key_ref, half_ref,
                     m_ref, acc_ref, *, tq, topk, layer):
    qi = pl.program_id(1)
    nkc = qi + 1
    iw_t = iw_ref[0].T

    one16, zero16, lowest16 = jnp.int16(1), jnp.int16(0), jnp.int16(-32768)
    half = tq // 2
    diag_tiles = ((0, half, 0, tq), (half, half, half, half))

    def score_tile(c, k0, nk, q0, nq, causal):
        off = pl.multiple_of(c * tq + k0, nk)
        ikc = ik_ref[0, pl.ds(off, nk), :]
        isc = jnp.zeros((nk, nq), F32)
        for hp in range(IDX_HEADS // 2):
            iqp = iq_ref[0, q0:q0 + nq, hp * LANES:(hp + 1) * LANES]
            for e in range(2):
                hh = 2 * hp + e
                s = _dot_nt(ikc[:, e * LANES:(e + 1) * LANES], iqp)
                isc = isc + jnp.maximum(s, 0.0) * iw_t[hh:hh + 1, q0:q0 + nq]
        bits = lax.bitcast_convert_type(isc, jnp.int32)
        key = jnp.where(bits < 0, bits ^ jnp.int32(0x7FFFFFFF), bits)
        if causal:
            key_pos = k0 + lax.broadcasted_iota(jnp.int32, (nk, nq), 0)
            query_pos = q0 + lax.broadcasted_iota(jnp.int32, (nk, nq), 1)
            key = jnp.where(key_pos <= query_pos, key, INT_MIN)
        key_ref[c, k0:k0 + nk, q0:q0 + nq] = key
        half_ref[c, k0:k0 + nk, q0:q0 + nq] = (key >> 16).astype(jnp.int16)

    _for_each_chunk(qi, lambda c: score_tile(c, 0, tq, 0, tq, False))
    for tile in diag_tiles:
        score_tile(qi, *tile, True)
    key_ref[qi, half:, :half] = jnp.full((half, half), INT_MIN, jnp.int32)
    half_ref[qi, half:, :half] = jnp.full((half, half), lowest16, jnp.int16)

    @pl.when(nkc % COUNT_UNROLL == 1)
    def _():
        half_ref[nkc] = jnp.full((tq, tq), lowest16, jnp.int16)

    def row_count(bound, strict):
        def body(t, part):
            for u in range(COUNT_UNROLL):
                for r in range(tq // COUNT_ACC_ROWS):
                    half = half_ref[COUNT_UNROLL * t + u, r * COUNT_ACC_ROWS:(r + 1) * COUNT_ACC_ROWS, :]
                    part = part + jnp.where(half > bound if strict else half >= bound, one16, zero16)
            return part

        part = lax.fori_loop(0, (nkc + COUNT_UNROLL - 1) // COUNT_UNROLL, body,
                             jnp.zeros((COUNT_ACC_ROWS, tq), jnp.int16))
        return jnp.sum(part.astype(F32), axis=0, keepdims=True)

    def search16(rank, cnt_init):
        def bit(i, carry):
            thr_u, cnt_thr = carry
            cand_u = thr_u | lax.shift_left(jnp.int32(1), 15 - i)
            cnt = row_count((cand_u - 32768).astype(jnp.int16), strict=False)
            take = cnt >= rank
            return jnp.where(take, cand_u, thr_u), jnp.where(take, cnt, cnt_thr)

        return lax.fori_loop(0, 16, bit, (jnp.zeros((1, tq), jnp.int32), cnt_init))

    k_f = jnp.full((1, tq), float(topk), F32)
    hi_u, cnt_hi = search16(k_f, jnp.zeros((1, tq), F32))
    hi_thr = (hi_u - 32768).astype(jnp.int16)
    cnt_gt = row_count(hi_thr, strict=True)

    def low_halves(c, carry):
        lo = ((key_ref[c] & 0xFFFF) - 32768).astype(jnp.int16)
        half_ref[c] = jnp.where(half_ref[c] == hi_thr, lo, lowest16)
        return carry

    lax.fori_loop(0, nkc, low_halves, 0)
    lo_u, cnt_lo = search16(k_f - cnt_gt, cnt_hi - cnt_gt)
    thr_raw = ((hi_u << 16) | lo_u) ^ jnp.int32(INT_MIN)
    short_row = thr_raw == jnp.int32(INT_MIN)
    thr = jnp.maximum(thr_raw, jnp.int32(INT_MIN + 1))
    cnt_ge = cnt_gt + cnt_lo
    tied = jnp.logical_and(jnp.logical_not(short_row), cnt_ge > k_f)
    any_tied = jnp.max(jnp.where(tied, 1.0, 0.0)) > 0.0

    _init_softmax(m_ref, acc_ref)

    def attend(c, selected, shift, tiles=((0, tq, 0, tq),)):
        bias = jnp.where(selected, 0.0 if shift is None else -shift, NEG).T
        for k0, nk, q0, nq in tiles:
            off = pl.multiple_of(c * tq + k0, nk)
            for h in range(GROUP_HEADS):
                q = q_ref[0, q0:q0 + nq, h * LANES:(h + 1) * LANES]
                k = k_ref[0, pl.ds(off, nk), h * LANES:(h + 1) * LANES]
                v = v_ref[0, pl.ds(off, nk), h * VEXT:(h + 1) * VEXT]
                s = _dot_nt(q, k) + bias[q0:q0 + nq, k0:k0 + nk]
                _softmax_chunk(s, v, m_ref, acc_ref, h, shift is None, q0)

    def attend_untied(shift):
        _for_each_chunk(qi, lambda c: attend(c, key_ref[c] >= thr, shift))
        attend(qi, key_ref[qi] >= thr, shift, diag_tiles)

    @pl.when(jnp.logical_not(any_tied))
    def _():
        _with_score_bound(bound_ref[layer, _BOUND_COLS.index("dsa")], attend_untied)

    @pl.when(any_tied)
    def _():
        def eq_chunk(c, n):
            return n + jnp.sum(jnp.where(key_ref[c] == thr, 1.0, 0.0), axis=0, keepdims=True)

        n_eq = lax.fori_loop(0, nkc, eq_chunk, jnp.zeros((1, tq), F32))
        keep = jnp.where(short_row, 0.0, k_f - (cnt_ge - n_eq))

        def attend_chunk(c, seen):
            key = key_ref[c]
            eq = key == thr
            eq_f = jnp.where(eq, 1.0, 0.0)
            rank = seen + _dot(earlier_ref[...], eq_f.astype(BF16))
            attend(c, jnp.logical_or(key > thr, jnp.logical_and(eq, rank < keep)), None)
            return seen + jnp.sum(eq_f, axis=0, keepdims=True)

        lax.fori_loop(0, nkc, attend_chunk, jnp.zeros((1, tq), F32))

    for h in range(GROUP_HEADS):
        o_ref[0, :, h * LANES:(h + 1) * LANES] = _normalised(acc_ref, h).astype(o_ref.dtype)


def _attn_dsa(q, k, v, iq, ik, iw, score_bounds, layer, *, tq):
    b, s, _ = q.shape
    topk = min(TOPK_MAX, s // 4)
    earlier = np.asarray(np.arange(tq)[:, None] > np.arange(tq)[None, :], dtype=BF16)
    q_map = lambda bi, qi: (bi, qi, 0)
    all_map = lambda bi, qi: (bi, 0, 0)
    return pl.pallas_call(
        functools.partial(_attn_dsa_kernel, tq=tq, topk=topk, layer=layer),
        grid=(b, s // tq),
        in_specs=[pl.BlockSpec(memory_space=pltpu.SMEM),
                  pl.BlockSpec((1, tq, GROUP_WIDTH), q_map),
                  pl.BlockSpec((1, s, GROUP_WIDTH), all_map),
                  pl.BlockSpec((1, s, GROUP_HEADS * VEXT), all_map),
                  pl.BlockSpec((1, tq, IDX_HEADS * IDX_DIM), q_map),
                  pl.BlockSpec((1, s, 2 * LANES), all_map),
                  pl.BlockSpec((1, tq, LANES), q_map),
                  pl.BlockSpec((tq, tq), lambda bi, qi: (0, 0))],
        out_specs=pl.BlockSpec((1, tq, GROUP_WIDTH), q_map),
        out_shape=jax.ShapeDtypeStruct((b, s, GROUP_WIDTH), BF16),
        scratch_shapes=[pltpu.VMEM((s // tq, tq, tq), jnp.int32),
                        pltpu.VMEM((-(-(s // tq) // COUNT_UNROLL) * COUNT_UNROLL, tq, tq), jnp.int16),
                        pltpu.VMEM((GROUP_HEADS, tq, LANES), F32),
                        pltpu.VMEM((GROUP_HEADS, tq, VEXT), F32)],
        compiler_params=_cparams(2),
        name="attn_dsa",
    )(score_bounds, q, k, v, iq, ik, iw, earlier)


def _rope_tables(seq, period, half):
    inv = np.float32(ROPE_THETA) ** (-np.arange(half, dtype=np.float32) / np.float32(half))
    lane = np.arange(LANES)
    idx = lane % period
    active = idx < half
    ang = np.arange(seq, dtype=np.float32)[:, None] * inv[np.minimum(idx, half - 1)][None, :]
    cos, sin = np.cos(ang), np.sin(ang)
    return np.stack([np.where(active, cos, 1.0), np.where(active, np.where(lane < LANES // 2, -sin, sin), 0.0)]
                    ).astype(np.float32)


def _half_split_layout(dims, half, start=0):
    first = list(range(half)) + list(range(2 * half, 2 * half + (dims - 2 * half) // 2))
    second = list(range(half, 2 * half)) + list(range(2 * half + (dims - 2 * half) // 2, dims))
    lanes = [-1] * LANES
    lanes[start:start + len(first)] = first
    lanes[start + LANES // 2:start + LANES // 2 + len(second)] = second
    return lanes


_HEAD_LAYOUT = _half_split_layout(HEAD_DIM, HEAD_DIM // 8)
_SUB_LAYOUT = _half_split_layout(DIFF_SUB_DIM, DIFF_SUB_DIM // 8)
_PAIR_LAYOUT = [a if a >= 0 else (DIFF_SUB_DIM + b if b >= 0 else -1)
                for a, b in zip(_SUB_LAYOUT, _half_split_layout(DIFF_SUB_DIM, DIFF_SUB_DIM // 8, start=32))]
_ROPE64_LAYOUT = _half_split_layout(MLA_ROPE_DIM, MLA_ROPE_DIM // 2)


def _take_cols(a, layout):
    src = max(layout) + 1
    place = np.zeros((src, LANES), np.float32)
    for lane, i in enumerate(layout):
        if i >= 0:
            place[i, lane] = 1.0
    blocks = a.reshape(a.shape[:-1] + (-1, src))
    precision = lax.Precision.HIGHEST if a.dtype == F32 else None
    out = jnp.einsum("...k,kj->...j", blocks, jnp.asarray(place, a.dtype), precision=precision,
                     preferred_element_type=a.dtype)
    return out.reshape(a.shape[:-1] + (-1,))


def _pad_cols(a, width):
    return jnp.pad(a, ((0, 0),) * (a.ndim - 1) + ((0, width - a.shape[-1]),))


def _relayout_w_in(w):
    sizes = (512,) * 3 + (MLA_Q_RANK, MLA_KV_RANK, MLA_ROPE_DIM) + (512,) * 6 + (IDX_HEADS * IDX_DIM, IDX_DIM, IDX_HEADS)
    names = ("a_q", "a_k", "a_v", "b_cq", "b_ckv", "b_kr", "c_q", "c_k", "c_v", "d_q", "d_k", "d_v", "d_iq", "d_ik", "d_iw")
    w = w.astype(BF16)
    parts, start = {}, 0
    for nme, sz in zip(names, sizes):
        parts[nme] = w[..., start:start + sz]
        start += sz
    for nme in ("a_q", "a_k", "d_iq"):
        parts[nme] = _take_cols(parts[nme], _PAIR_LAYOUT)
    for nme in ("d_q", "d_k"):
        parts[nme] = _take_cols(parts[nme], _HEAD_LAYOUT)
    parts["d_ik"] = _take_cols(parts["d_ik"], _SUB_LAYOUT)
    parts["b_kr"] = _take_cols(parts["b_kr"], _ROPE64_LAYOUT)
    order = sorted(_OFF, key=_OFF.get)
    ends = [_OFF[nme] for nme in order[1:]] + [Y_WIDTH]
    return jnp.concatenate([_pad_cols(parts[nme], end - _OFF[nme]) for nme, end in zip(order, ends)], axis=-1)


def _relayout_mla_q(w):
    lead = w.shape[:-1]
    w = w.reshape(lead + (-1, MLA_QK_DIM))
    rope = _take_cols(w[..., MLA_NOPE_DIM:], _ROPE64_LAYOUT)
    return jnp.concatenate([w[..., :MLA_NOPE_DIM], rope], axis=-1).reshape(lead + (-1,))


_BOUND_COLS = ("diff", "mla", "dsa", "unused")
_GAIN_ROWS = ("a_q", "a_k", "b_cq", "b_ckv", "b_q", "b_k", "d_q", "d_k", "d_ik")


def _score_bounds(diff_qk_norm, mla_qk_norm, dsa_qk_norm):
    def bound(dim, qk_norm):
        peak = jnp.max(jnp.abs(qk_norm), axis=-1)
        return (dim ** 0.5 * LOG2E * SCORE_BOUND_MARGIN) * peak[:, 0] * peak[:, 1]

    cols = [bound(DIFF_SUB_DIM, diff_qk_norm), bound(MLA_QK_DIM, mla_qk_norm), bound(HEAD_DIM, dsa_qk_norm)]
    return jnp.stack(cols + [jnp.zeros_like(cols[0])], axis=1).astype(F32)


def _packed_gains(diff_qk_norm, mla_q_a_norm, mla_kv_a_norm, mla_qk_norm, dsa_qk_norm, idx_k_norm):
    def tiled(g, reps):
        return jnp.tile(g, (1, reps))

    rows = dict(
        a_q=_take_cols(tiled(diff_qk_norm[:, 0], 8), _PAIR_LAYOUT),
        a_k=_take_cols(tiled(diff_qk_norm[:, 1], 8), _PAIR_LAYOUT),
        b_cq=mla_q_a_norm, b_ckv=mla_kv_a_norm,
        b_q=_relayout_mla_q(tiled(mla_qk_norm[:, 0], GROUP_HEADS)),
        b_k=_relayout_mla_q(mla_qk_norm[:, 1]),
        d_q=_take_cols(tiled(dsa_qk_norm[:, 0], 4), _HEAD_LAYOUT),
        d_k=_take_cols(tiled(dsa_qk_norm[:, 1], 4), _HEAD_LAYOUT),
        d_ik=_take_cols(idx_k_norm, _SUB_LAYOUT))
    packed = jnp.stack([_pad_cols(rows[nme], 1024) for nme in _GAIN_ROWS], axis=1)
    return jnp.pad(packed, ((0, 0), (0, 16 - len(_GAIN_ROWS)), (0, 0)))


def kernel(x, attn_norm, w_in, diff_qk_norm, diff_lambda, diff_subln, mla_q_a_norm, mla_wq_b, mla_kv_a_norm,
           mla_wkv_b, mla_qk_norm, dsa_qk_norm, idx_k_norm, w_o, ffn_norm, w_gate, w_up, w_down):
    b, s, d = x.shape
    n = b * s
    depth = w_in.shape[0]
    tm = min(1024, n)
    tq = min(512, s)
    tabs = (_rope_tables(s, LANES // 4, DIFF_SUB_DIM // 8),
            _rope_tables(s, LANES // 2, MLA_ROPE_DIM // 2),
            _rope_tables(s, LANES // 2, HEAD_DIM // 8))
    sub_head = np.arange(GROUP_WIDTH) // LANES * 2 + (np.arange(GROUP_WIDTH) % (LANES // 2)) // (LANES // 4)
    ones64 = np.asarray(sub_head[:, None] == sub_head[None, :], dtype=BF16)
    tk_stick = min(256, s)
    tri = np.asarray(np.arange(tk_stick)[:, None] > np.arange(tk_stick)[None, :], dtype=BF16)

    w_in_r = _relayout_w_in(w_in)
    wq = jnp.pad(_relayout_mla_q(mla_wq_b.astype(BF16)), ((0, 0), (0, 512 - MLA_Q_RANK), (0, 0)))
    wkv = mla_wkv_b.reshape(depth, MLA_KV_RANK, GROUP_HEADS, 2, HEAD_DIM).transpose(0, 1, 3, 2, 4)
    wkv = wkv.reshape(depth, MLA_KV_RANK, 2 * GROUP_WIDTH).astype(BF16)
    gains = _packed_gains(diff_qk_norm, mla_q_a_norm, mla_kv_a_norm, mla_qk_norm, dsa_qk_norm, idx_k_norm)
    bounds = _score_bounds(diff_qk_norm, mla_qk_norm, dsa_qk_norm)
    attn_gain, ffn_gain = attn_norm.reshape(depth, 1, d), ffn_norm.reshape(depth, 1, d)
    subln = diff_subln.reshape(depth, 1, HEAD_DIM)
    w_o_b, w_down_b = w_o.astype(BF16), w_down.astype(BF16)

    xf = x.reshape(n, d)
    for l in range(depth):
        lam_init = 0.8 - 0.6 * math.exp(-0.3 * l)
        y = _rms_matmul(xf, attn_gain, w_in_r, l, tm=tm, tn=Y_WIDTH // 4, out_dtype=F32)
        (qa, ka, va, qb, kb, vb, qc, kc, vc, qd, kd, vd, iq, ik, iw) = [
            a.reshape(b, s, a.shape[1]) for a in _prep(y, s, tabs, ones64, gains, wq, wkv, l, tm=min(256, s))]

        o_a = _resident_attention(
            functools.partial(_attn_diff_kernel, tq=tq, layer=l, lam_init=lam_init), qa, ka, va,
            [], [diff_lambda, subln], l, tq=tq,
            scratch=[pltpu.VMEM((8, tq, LANES), F32), pltpu.VMEM((8, tq, VEXT), F32)], name="attn_diff",
            score_bounds=bounds)
        o_b = _resident_attention(
            functools.partial(_attn_mla_kernel, tq=tq, layer=l), qb, kb, vb, [], [], l, tq=tq,
            scratch=[pltpu.VMEM((4, tq, LANES), F32), pltpu.VMEM((4, tq, VEXT), F32)], name="attn_mla",
            score_bounds=bounds)
        o_c = _resident_attention(
            functools.partial(_attn_stick_kernel, tq=tq, tkc=tk_stick), qc, kc, vc, [tri], [], l, tq=tq,
            scratch=[pltpu.VMEM((4, tq, LANES), F32), pltpu.VMEM((4, tq, LANES), F32)], name="attn_stick")
        o_d = _attn_dsa(qd, kd, vd, iq, ik, iw, bounds, l, tq=tq)

        mixed = [o.reshape(n, GROUP_WIDTH) for o in (o_a, o_b, o_c, o_d)]
        xf = _matmul_residual(mixed, w_o_b, xf, l, tm=min(512, n), tn=d)
        act = _ffn_up(xf, ffn_gain, w_gate, w_up, l, tm=tm, tn=512)
        xf = _matmul_residual([act], w_down_b, xf, l, tm=tm, tn=512)
    return xf.reshape(b, s, d)
```

```python
import functools
import math

import jax
import jax.numpy as jnp
import numpy as np
from jax import lax
from jax.experimental import pallas as pl
from jax.experimental.pallas import tpu as pltpu

F32 = jnp.float32
BF16 = jnp.bfloat16

HEAD_DIM = 128
GROUP_HEADS = 4
GROUP_WIDTH = GROUP_HEADS * HEAD_DIM
ROPE_THETA = 500000.0
EPS = 1e-6
DIFF_SUB_DIM = 64
MLA_Q_RANK = 448
MLA_KV_RANK = 128
MLA_NOPE_DIM = 128
MLA_ROPE_DIM = 64
MLA_QK_DIM = MLA_NOPE_DIM + MLA_ROPE_DIM
MLA_PAD_DIM = 256
IDX_HEADS = 16
IDX_DIM = 64
TOPK_MAX = 256

LANES = 128
VEXT = 2 * LANES
Y_WIDTH = 13 * GROUP_WIDTH
NEG = -1e30
INT_MIN = -2147483648
LOG2E = 1.4426950408889634
COUNT_UNROLL = 2
COUNT_ACC_ROWS = 64
MAX_FIXED_SHIFT = 56.0
SCORE_BOUND_MARGIN = 1.02
STICK_DEAD_LOG2 = -160.0
VMEM_LIMIT = 56 * 1024 * 1024

_OFF = dict(a_q=0, a_k=512, a_v=1024, b_cq=1536, b_ckv=2048, b_kr=2176, d_ik=2304, d_iw=2432,
            c_q=2560, c_k=3072, c_v=3584, d_q=4096, d_k=4608, d_v=5120, d_iq=5632)


def _cparams(n_axes):
    return pltpu.CompilerParams(dimension_semantics=("arbitrary",) * n_axes, vmem_limit_bytes=VMEM_LIMIT)


def _dot(a, b):
    return jnp.dot(a, b, preferred_element_type=F32)


def _dot_nt(a, b):
    return lax.dot_general(a, b, (((1,), (1,)), ((), ())), preferred_element_type=F32)


def _rms_rows(x, g):
    ms = jnp.mean(x * x, axis=-1, keepdims=True)
    return x * lax.rsqrt(ms + EPS) * g


def _rms_matmul_kernel(x_ref, g_ref, w_ref, o_ref, h_ref):
    @pl.when(pl.program_id(1) == 0)
    def _():
        h_ref[...] = _rms_rows(x_ref[...], g_ref[...]).astype(BF16)

    o_ref[...] = _dot(h_ref[...], w_ref[...]).astype(o_ref.dtype)


def _rms_matmul(x, g, w, layer, *, tm, tn, out_dtype):
    n, d = x.shape
    m = w.shape[2]
    return pl.pallas_call(
        _rms_matmul_kernel,
        grid=(n // tm, m // tn),
        in_specs=[pl.BlockSpec((tm, d), lambda i, j: (i, 0)),
                  pl.BlockSpec((None, 1, d), lambda i, j: (layer, 0, 0)),
                  pl.BlockSpec((None, d, tn), lambda i, j: (layer, 0, j))],
        out_specs=pl.BlockSpec((tm, tn), lambda i, j: (i, j)),
        out_shape=jax.ShapeDtypeStruct((n, m), out_dtype),
        scratch_shapes=[pltpu.VMEM((tm, d), BF16)],
        compiler_params=_cparams(2),
        name="rms_matmul",
    )(x, g, w)


def _ffn_up_kernel(x_ref, g_ref, wg_ref, wu_ref, o_ref, h_ref):
    @pl.when(pl.program_id(1) == 0)
    def _():
        h_ref[...] = _rms_rows(x_ref[...], g_ref[...]).astype(BF16)

    h = h_ref[...]
    a = _dot(h, wg_ref[...].astype(BF16))
    b = _dot(h, wu_ref[...].astype(BF16))
    o_ref[...] = (jax.nn.silu(a) * b).astype(o_ref.dtype)


def _ffn_up(x, g, wg, wu, layer, *, tm, tn):
    n, d = x.shape
    m = wg.shape[2]
    return pl.pallas_call(
        _ffn_up_kernel,
        grid=(n // tm, m // tn),
        in_specs=[pl.BlockSpec((tm, d), lambda i, j: (i, 0)),
                  pl.BlockSpec((None, 1, d), lambda i, j: (layer, 0, 0)),
                  pl.BlockSpec((None, d, tn), lambda i, j: (layer, 0, j)),
                  pl.BlockSpec((None, d, tn), lambda i, j: (layer, 0, j))],
        out_specs=pl.BlockSpec((tm, tn), lambda i, j: (i, j)),
        out_shape=jax.ShapeDtypeStruct((n, m), BF16),
        scratch_shapes=[pltpu.VMEM((tm, d), BF16)],
        compiler_params=_cparams(2),
        name="ffn_up",
    )(x, g, wg, wu)


def _matmul_residual_kernel(*refs):
    a_refs, (w_ref, r_ref, o_ref) = refs[:-3], refs[-3:]
    acc = r_ref[...]
    row = 0
    for a_ref in a_refs:
        k = a_ref.shape[1]
        acc = acc + _dot(a_ref[...], w_ref[row:row + k, :])
        row += k
    o_ref[...] = acc


def _matmul_residual(parts, w, r, layer, *, tm, tn):
    n = parts[0].shape[0]
    _, k, m = w.shape
    return pl.pallas_call(
        _matmul_residual_kernel,
        grid=(n // tm, m // tn),
        in_specs=[pl.BlockSpec((tm, a.shape[1]), lambda i, j: (i, 0)) for a in parts]
                 + [pl.BlockSpec((None, k, tn), lambda i, j: (layer, 0, j)),
                    pl.BlockSpec((tm, tn), lambda i, j: (i, j))],
        out_specs=pl.BlockSpec((tm, tn), lambda i, j: (i, j)),
        out_shape=jax.ShapeDtypeStruct((n, m), F32),
        compiler_params=_cparams(2),
        name="matmul_residual",
    )(*parts, w, r)


def _rope(x, t_ref, blocks=None):
    outs = []
    for b in range(x.shape[1] // LANES):
        xb = x[:, b * LANES:(b + 1) * LANES]
        if blocks is None or b in blocks:
            xb = xb * t_ref[0] + pltpu.roll(xb, LANES // 2, 1) * t_ref[1]
        outs.append(xb)
    return outs[0] if len(outs) == 1 else jnp.concatenate(outs, axis=1)


def _rms_lane_groups(x, width, count):
    outs = []
    for h in range(x.shape[1] // width):
        xh = x[:, h * width:(h + 1) * width]
        ms = jnp.sum(xh * xh, axis=-1, keepdims=True) * (1.0 / count)
        outs.append(xh * lax.rsqrt(ms + EPS))
    return outs[0] if len(outs) == 1 else jnp.concatenate(outs, axis=1)


def _rms_sub64(x, ones_ref):
    x2 = x * x
    hi = x2.astype(BF16)
    lo = (x2 - hi.astype(F32)).astype(BF16)
    ms = (_dot(hi, ones_ref[...]) + _dot(lo, ones_ref[...])) * (1.0 / DIFF_SUB_DIM)
    return x * lax.rsqrt(ms + EPS)


def _store_vext(ref, v):
    ones = jnp.ones((v.shape[0], LANES), BF16)
    for h in range(GROUP_HEADS):
        ref[:, h * VEXT:h * VEXT + LANES] = v[:, h * LANES:(h + 1) * LANES].astype(BF16)
        ref[:, h * VEXT + LANES:(h + 1) * VEXT] = ones


def _prep_kernel(y_ref, ta_ref, tb_ref, td_ref, ones_ref, g_ref, wq_ref, wkv_ref,
                 qa_ref, ka_ref, va_ref, qb_ref, kb_ref, vb_ref, qc_ref, kc_ref, vc_ref,
                 qd_ref, kd_ref, vd_ref, iq_ref, ik_ref, iw_ref):
    def sec(name, width):
        return y_ref[:, _OFF[name]:_OFF[name] + width]

    def gain(name, width):
        row = _GAIN_ROWS.index(name)
        return g_ref[row:row + 1, :width]

    tm = y_ref.shape[0]
    lane = lax.broadcasted_iota(jnp.int32, (tm, LANES), 1)
    first_half = (lane % (LANES // 2)) < DIFF_SUB_DIM // 2

    qa = _rope(_rms_sub64(sec("a_q", 512), ones_ref) * gain("a_q", 512), ta_ref) * (DIFF_SUB_DIM ** -0.5 * LOG2E)
    for h in range(GROUP_HEADS):
        qh = qa[:, h * LANES:(h + 1) * LANES]
        qa_ref[:, (2 * h) * LANES:(2 * h + 1) * LANES] = jnp.where(first_half, qh, 0.0).astype(BF16)
        qa_ref[:, (2 * h + 1) * LANES:(2 * h + 2) * LANES] = jnp.where(first_half, 0.0, qh).astype(BF16)
    ka_ref[...] = _rope(_rms_sub64(sec("a_k", 512), ones_ref) * gain("a_k", 512), ta_ref).astype(BF16)
    _store_vext(va_ref, sec("a_v", 512))

    cq = sec("b_cq", 512)
    cq = cq * lax.rsqrt(jnp.sum(cq * cq, axis=-1, keepdims=True) * (1.0 / MLA_Q_RANK) + EPS) * gain("b_cq", 512)
    qb = _dot(cq.astype(BF16), wq_ref[...])
    qb = _rms_lane_groups(qb, MLA_PAD_DIM, MLA_QK_DIM) * gain("b_q", 1024)
    ckv = _rms_lane_groups(sec("b_ckv", 128), 128, MLA_KV_RANK) * gain("b_ckv", 128)
    kv = _dot(ckv.astype(BF16), wkv_ref[...])
    kr = sec("b_kr", 128)
    kr_ss = jnp.sum(kr * kr, axis=-1, keepdims=True)
    for h in range(GROUP_HEADS):
        lo, hi = h * MLA_PAD_DIM, (h + 1) * MLA_PAD_DIM
        qb_ref[:, lo:hi] = (_rope(qb[:, lo:hi], tb_ref, blocks=(1,)) * (MLA_QK_DIM ** -0.5 * LOG2E)).astype(BF16)
        kn = kv[:, h * LANES:(h + 1) * LANES]
        ms = (jnp.sum(kn * kn, axis=-1, keepdims=True) + kr_ss) * (1.0 / MLA_QK_DIM)
        kh = jnp.concatenate([kn, kr], axis=1) * lax.rsqrt(ms + EPS) * gain("b_k", MLA_PAD_DIM)
        kb_ref[:, lo:hi] = _rope(kh, tb_ref, blocks=(1,)).astype(BF16)
    _store_vext(vb_ref, kv[:, GROUP_WIDTH:])

    qc_ref[...] = (sec("c_q", 512) * (HEAD_DIM ** -0.5 * LOG2E)).astype(BF16)
    kc_ref[...] = sec("c_k", 512).astype(BF16)
    vc_ref[...] = sec("c_v", 512).astype(BF16)

    qd = _rope(_rms_lane_groups(sec("d_q", 512), HEAD_DIM, HEAD_DIM) * gain("d_q", 512), td_ref)
    qd_ref[...] = (qd * (HEAD_DIM ** -0.5 * LOG2E)).astype(BF16)
    kd_ref[...] = _rope(_rms_lane_groups(sec("d_k", 512), HEAD_DIM, HEAD_DIM) * gain("d_k", 512), td_ref).astype(BF16)
    _store_vext(vd_ref, sec("d_v", 512))
    for half in range(2):
        iq = y_ref[:, _OFF["d_iq"] + half * 512:_OFF["d_iq"] + (half + 1) * 512]
        iq_ref[:, half * 512:(half + 1) * 512] = (_rope(iq, ta_ref) * (IDX_DIM ** -0.5)).astype(BF16)
    ik = _rope(_rms_lane_groups(sec("d_ik", 128), 128, IDX_DIM) * gain("d_ik", 128), ta_ref)
    ik_ref[:, :LANES] = ik.astype(BF16)
    ik_ref[:, LANES:] = pltpu.roll(ik, IDX_DIM // 2, 1).astype(BF16)
    iw_ref[...] = sec("d_iw", 128) * (IDX_HEADS ** -0.5)


def _prep(y, seq, tabs, ones64, gains, wq, wkv, layer, *, tm):
    n = y.shape[0]
    nblk_seq = seq // tm
    ta, tb, td = tabs

    def rows(width):
        return pl.BlockSpec((tm, width), lambda i: (i, 0))

    def table():
        return pl.BlockSpec((2, tm, LANES), lambda i: (0, i % nblk_seq, 0))

    def whole(a):
        return pl.BlockSpec(a.shape, lambda i: (0,) * a.ndim)

    def of_layer(a):
        return pl.BlockSpec((None,) + a.shape[1:], lambda i: (layer,) + (0,) * (a.ndim - 1))

    widths = [1024, 512, 1024, 1024, 1024, 1024, 512, 512, 512, 512, 512, 1024, 1024, 256]
    out_shape = [jax.ShapeDtypeStruct((n, w), BF16) for w in widths] + [jax.ShapeDtypeStruct((n, LANES), F32)]
    out_specs = [rows(w) for w in widths] + [rows(LANES)]
    return pl.pallas_call(
        _prep_kernel,
        grid=(n // tm,),
        in_specs=[rows(Y_WIDTH), table(), table(), table(), whole(ones64), of_layer(gains), of_layer(wq), of_layer(wkv)],
        out_specs=out_specs,
        out_shape=out_shape,
        compiler_params=_cparams(1),
        name="prep",
    )(y, ta, tb, td, ones64, gains, wq, wkv)


def _softmax_chunk(s, vext, m_ref, acc_ref, idx, running_max, row0=0):
    rows = slice(row0, row0 + s.shape[0])
    if not running_max:
        acc_ref[idx, rows] += _dot(jnp.exp2(s).astype(BF16), vext)
        return
    m_prev = m_ref[idx, rows]
    m_new = jnp.maximum(m_prev, jnp.max(s, axis=1, keepdims=True))
    p = jnp.exp2(s - jnp.tile(m_new, (1, s.shape[1] // LANES)))
    alpha = jnp.exp2(m_prev - m_new)
    acc_ref[idx, rows] = jnp.tile(alpha, (1, VEXT // LANES)) * acc_ref[idx, rows] + _dot(p.astype(BF16), vext)
    m_ref[idx, rows] = m_new


def _init_softmax(m_ref, acc_ref):
    m_ref[...] = jnp.full(m_ref.shape, NEG, F32)
    acc_ref[...] = jnp.zeros(acc_ref.shape, F32)


def _with_score_bound(bound, attend):
    @pl.when(bound <= MAX_FIXED_SHIFT)
    def _():
        attend(bound)

    @pl.when(jnp.logical_not(bound <= MAX_FIXED_SHIFT))
    def _():
        attend(None)


def _normalised(acc_ref, idx):
    acc = acc_ref[idx]
    return acc[:, :LANES] / acc[:, LANES:]


def _causal_mask(row0, col0, tq, tkc, strict=False):
    rows = row0 + lax.broadcasted_iota(jnp.int32, (tq, tkc), 0)
    cols = col0 + lax.broadcasted_iota(jnp.int32, (tq, tkc), 1)
    return cols < rows if strict else cols <= rows


def _for_each_chunk(n, body):
    def pair(t, carry):
        body(2 * t)
        body(2 * t + 1)
        return carry

    lax.fori_loop(0, n // 2, pair, 0)

    @pl.when(n % 2 == 1)
    def _():
        body(n - 1)


def _causal_chunks(chunk, qi, tq):
    _for_each_chunk(qi, lambda c: chunk(pl.multiple_of(c * tq, tq), tq, None))
    half = tq // 2
    chunk(pl.multiple_of(qi * tq, tq), half, _causal_mask(0, 0, tq, half))
    chunk(pl.multiple_of(qi * tq + half, half), half, _causal_mask(0, 0, half, half), half, half)


def _attn_diff_kernel(bound_ref, lam_ref, sub_ref, q_ref, k_ref, v_ref, o_ref, m_ref, acc_ref, *, tq, layer, lam_init):
    qi = pl.program_id(1)
    _init_softmax(m_ref, acc_ref)

    def attend(shift):
        def chunk(off, keys, mask, row0=0, rows=tq):
            for mp in range(2 * GROUP_HEADS):
                h = mp // 2
                q = q_ref[0, row0:row0 + rows, mp * LANES:(mp + 1) * LANES]
                k = k_ref[0, pl.ds(off, keys), h * LANES:(h + 1) * LANES]
                v = v_ref[0, pl.ds(off, keys), h * VEXT:(h + 1) * VEXT]
                s = _dot_nt(q, k)
                if shift is not None:
                    s = s - shift
                if mask is not None:
                    s = jnp.where(mask, s, NEG)
                _softmax_chunk(s, v, m_ref, acc_ref, mp, shift is None, row0)

        _causal_chunks(chunk, qi, tq)

    _with_score_bound(bound_ref[layer, _BOUND_COLS.index("diff")], attend)

    lv = lam_ref[...]
    lam = (jnp.exp(jnp.sum(lv[0:1] * lv[1:2], axis=-1, keepdims=True))
           - jnp.exp(jnp.sum(lv[2:3] * lv[3:4], axis=-1, keepdims=True)) + lam_init)
    for h in range(GROUP_HEADS):
        o = _normalised(acc_ref, 2 * h) - lam * _normalised(acc_ref, 2 * h + 1)
        o = _rms_rows(o, sub_ref[...]) * (1.0 - lam_init)
        o_ref[0, :, h * LANES:(h + 1) * LANES] = o.astype(o_ref.dtype)


def _attn_mla_kernel(bound_ref, q_ref, k_ref, v_ref, o_ref, m_ref, acc_ref, *, tq, layer):
    qi = pl.program_id(1)
    _init_softmax(m_ref, acc_ref)

    def attend(shift):
        def chunk(off, keys, mask, row0=0, rows=tq):
            for h in range(GROUP_HEADS):
                q = q_ref[0, row0:row0 + rows, h * MLA_PAD_DIM:(h + 1) * MLA_PAD_DIM]
                k = k_ref[0, pl.ds(off, keys), h * MLA_PAD_DIM:(h + 1) * MLA_PAD_DIM]
                v = v_ref[0, pl.ds(off, keys), h * VEXT:(h + 1) * VEXT]
                s = _dot_nt(q, k)
                if shift is not None:
                    s = s - shift
                if mask is not None:
                    s = jnp.where(mask, s, NEG)
                _softmax_chunk(s, v, m_ref, acc_ref, h, shift is None, row0)

        _causal_chunks(chunk, qi, tq)

    _with_score_bound(bound_ref[layer, _BOUND_COLS.index("mla")], attend)
    for h in range(GROUP_HEADS):
        o_ref[0, :, h * LANES:(h + 1) * LANES] = _normalised(acc_ref, h).astype(o_ref.dtype)


def _attn_stick_kernel(tri_ref, q_ref, k_ref, v_ref, o_ref, carry_ref, acc_ref, *, tq, tkc):
    qi = pl.program_id(1)
    carry_ref[...] = jnp.zeros(carry_ref.shape, F32)
    acc_ref[...] = jnp.zeros(acc_ref.shape, F32)

    def chunk(off, mask, row0=0, row1=tq):
        rows = slice(row0, row1)
        tri = tri_ref[...]
        for h in range(GROUP_HEADS):
            q = q_ref[0, rows, h * LANES:(h + 1) * LANES]
            k = k_ref[0, pl.ds(off, tkc), h * LANES:(h + 1) * LANES]
            v = v_ref[0, pl.ds(off, tkc), h * LANES:(h + 1) * LANES]
            z = _dot_nt(q, k)
            log_b = jnp.minimum(z, 0.0) - jnp.log2(1.0 + jnp.exp2(-jnp.abs(z)))
            log_1mb = log_b - z
            if mask is not None:
                log_1mb = jnp.where(mask, log_1mb, 0.0)
            hi = log_1mb.astype(BF16)
            lo = (log_1mb - hi.astype(F32)).astype(BF16)
            inner = _dot(hi, tri) + _dot(lo, tri)
            carry = carry_ref[h, rows]
            a = jnp.exp2(log_b + inner + jnp.tile(carry, (1, tkc // LANES)))
            if mask is not None:
                a = jnp.where(mask, a, 0.0)
            acc_ref[h, rows] += _dot(a.astype(BF16), v)
            carry_ref[h, rows] = carry + (inner[:, 0:1] + log_1mb[:, 0:1])

    n_diag = tq // tkc
    for j in range(n_diag):
        col0 = (n_diag - 1 - j) * tkc
        chunk(pl.multiple_of(qi * tq + col0, tkc), _causal_mask(col0, col0, tq - col0, tkc, strict=True), col0)

    n_full = qi * n_diag
    half = tq // 2

    def alive(row0, row1):
        return (jnp.max(carry_ref[:, row0:row1]) > STICK_DEAD_LOG2).astype(jnp.int32)

    def full_chunk(state):
        c, _, late_alive = state
        off = pl.multiple_of((n_full - 1 - c) * tkc, tkc)

        @pl.when(late_alive > 0)
        def _():
            chunk(off, None)

        @pl.when(late_alive == 0)
        def _():
            chunk(off, None, 0, half)

        return c + 1, alive(0, half), jnp.minimum(late_alive, alive(half, tq))

    lax.while_loop(lambda state: jnp.logical_and(state[0] < n_full, state[1] + state[2] > 0), full_chunk,
                   (jnp.int32(0), alive(0, half), alive(half, tq)))
    for h in range(GROUP_HEADS):
        o_ref[0, :, h * LANES:(h + 1) * LANES] = acc_ref[h].astype(o_ref.dtype)


def _resident_attention(kernel, q, k, v, extra, layer_extra, layer, *, tq, scratch, name, score_bounds=None):
    b, s, _ = q.shape
    q_map = lambda bi, qi: (bi, qi, 0)
    all_map = lambda bi, qi: (bi, 0, 0)
    extra_specs = [pl.BlockSpec(e.shape, lambda bi, qi, nd=e.ndim: (0,) * nd) for e in extra]
    extra_specs += [pl.BlockSpec((None,) + e.shape[1:], lambda bi, qi, nd=e.ndim: (layer,) + (0,) * (nd - 1))
                    for e in layer_extra]
    extra = list(extra) + list(layer_extra)
    if score_bounds is not None:
        extra = [score_bounds] + extra
        extra_specs = [pl.BlockSpec(memory_space=pltpu.SMEM)] + extra_specs
    return pl.pallas_call(
        kernel,
        grid=(b, s // tq),
        in_specs=extra_specs + [pl.BlockSpec((1, tq, q.shape[2]), q_map),
                                pl.BlockSpec((1, s, k.shape[2]), all_map),
                                pl.BlockSpec((1, s, v.shape[2]), all_map)],
        out_specs=pl.BlockSpec((1, tq, GROUP_WIDTH), q_map),
        out_shape=jax.ShapeDtypeStruct((b, s, GROUP_WIDTH), BF16),
        scratch_shapes=scratch,
        compiler_params=_cparams(2),
        name=name,
    )(*extra, q, k, v)


def _attn_dsa_kernel(bound_ref, q_ref, k_ref, v_ref, iq_ref, ik_ref, iw_ref, earlier_ref, o_ref, key_ref, half_ref,
                     m_ref, acc_ref, *, tq, topk, layer):
    qi = pl.program_id(1)
    nkc = qi + 1
    iw_t = iw_ref[0].T

    one16, zero16, lowest16 = jnp.int16(1), jnp.int16(0), jnp.int16(-32768)
    half = tq // 2
    diag_tiles = ((0, half, 0, tq), (half, half, half, half))

    def score_tile(c, k0, nk, q0, nq, causal):
        off = pl.multiple_of(c * tq + k0, nk)
        ikc = ik_ref[0, pl.ds(off, nk), :]
        isc = jnp.zeros((nk, nq), F32)
        for hp in range(IDX_HEADS // 2):
            iqp = iq_ref[0, q0:q0 + nq, hp * LANES:(hp + 1) * LANES]
            for e in range(2):
                hh = 2 * hp + e
                s = _dot_nt(ikc[:, e * LANES:(e + 1) * LANES], iqp)
                isc = isc + jnp.maximum(s, 0.0) * iw_t[hh:hh + 1, q0:q0 + nq]
        bits = lax.bitcast_convert_type(isc, jnp.int32)
        key = jnp.where(bits < 0, bits ^ jnp.int32(0x7FFFFFFF), bits)
        if causal:
            key_pos = k0 + lax.broadcasted_iota(jnp.int32, (nk, nq), 0)
            query_pos = q0 + lax.broadcasted_iota(jnp.int32, (nk, nq), 1)
            key = jnp.where(key_pos <= query_pos, key, INT_MIN)
        key_ref[c, k0:k0 + nk, q0:q0 + nq] = key
        half_ref[c, k0:k0 + nk, q0:q0 + nq] = (key >> 16).astype(jnp.int16)

    _for_each_chunk(qi, lambda c: score_tile(c, 0, tq, 0, tq, False))
    for tile in diag_tiles:
        score_tile(qi, *tile, True)
    key_ref[qi, half:, :half] = jnp.full((half, half), INT_MIN, jnp.int32)
    half_ref[qi, half:, :half] = jnp.full((half, half), lowest16, jnp.int16)

    @pl.when(nkc % COUNT_UNROLL == 1)
    def _():
        half_ref[nkc] = jnp.full((tq, tq), lowest16, jnp.int16)

    def row_count(bound, strict):
        def body(t, part):
            for u in range(COUNT_UNROLL):
                for r in range(tq // COUNT_ACC_ROWS):
                    half = half_ref[COUNT_UNROLL * t + u, r * COUNT_ACC_ROWS:(r + 1) * COUNT_ACC_ROWS, :]
                    part = part + jnp.where(half > bound if strict else half >= bound, one16, zero16)
            return part

        part = lax.fori_loop(0, (nkc + COUNT_UNROLL - 1) // COUNT_UNROLL, body,
                             jnp.zeros((COUNT_ACC_ROWS, tq), jnp.int16))
        return jnp.sum(part.astype(F32), axis=0, keepdims=True)

    def search16(rank, cnt_init):
        def bit(i, carry):
            thr_u, cnt_thr = carry
            cand_u = thr_u | lax.shift_left(jnp.int32(1), 15 - i)
            cnt = row_count((cand_u - 32768).astype(jnp.int16), strict=False)
            take = cnt >= rank
            return jnp.where(take, cand_u, thr_u), jnp.where(take, cnt, cnt_thr)

        return lax.fori_loop(0, 16, bit, (jnp.zeros((1, tq), jnp.int32), cnt_init))

    k_f = jnp.full((1, tq), float(topk), F32)
    hi_u, cnt_hi = search16(k_f, jnp.zeros((1, tq), F32))
    hi_thr = (hi_u - 32768).astype(jnp.int16)
    cnt_gt = row_count(hi_thr, strict=True)

    def low_halves(c, carry):
        lo = ((key_ref[c] & 0xFFFF) - 32768).astype(jnp.int16)
        half_ref[c] = jnp.where(half_ref[c] == hi_thr, lo, lowest16)
        return carry

    lax.fori_loop(0, nkc, low_halves, 0)
    lo_u, cnt_lo = search16(k_f - cnt_gt, cnt_hi - cnt_gt)
    thr_raw = ((hi_u << 16) | lo_u) ^ jnp.int32(INT_MIN)
    short_row = thr_raw == jnp.int32(INT_MIN)
    thr = jnp.maximum(thr_raw, jnp.int32(INT_MIN + 1))
    cnt_ge = cnt_gt + cnt_lo
    tied = jnp.logical_and(jnp.logical_not(short_row), cnt_ge > k_f)
    any_tied = jnp.max(jnp.where(tied, 1.0, 0.0)) > 0.0

    _init_softmax(m_ref, acc_ref)

    def attend(c, selected, shift, tiles=((0, tq, 0, tq),)):
        bias = jnp.where(selected, 0.0 if shift is None else -shift, NEG).T
        for k0, nk, q0, nq in tiles:
            off = pl.multiple_of(c * tq + k0, nk)
            for h in range(GROUP_HEADS):
                q = q_ref[0, q0:q0 + nq, h * LANES:(h + 1) * LANES]
                k = k_ref[0, pl.ds(off, nk), h * LANES:(h + 1) * LANES]
                v = v_ref[0, pl.ds(off, nk), h * VEXT:(h + 1) * VEXT]
                s = _dot_nt(q, k) + bias[q0:q0 + nq, k0:k0 + nk]
                _softmax_chunk(s, v, m_ref, acc_ref, h, shift is None, q0)

    def attend_untied(shift):
        _for_each_chunk(qi, lambda c: attend(c, key_ref[c] >= thr, shift))
        attend(qi, key_ref[qi] >= thr, shift, diag_tiles)

    @pl.when(jnp.logical_not(any_tied))
    def _():
        _with_score_bound(bound_ref[layer, _BOUND_COLS.index("dsa")], attend_untied)

    @pl.when(any_tied)
    def _():
        def eq_chunk(c, n):
            return n + jnp.sum(jnp.where(key_ref[c] == thr, 1.0, 0.0), axis=0, keepdims=True)

        n_eq = lax.fori_loop(0, nkc, eq_chunk, jnp.zeros((1, tq), F32))
        keep = jnp.where(short_row, 0.0, k_f - (cnt_ge - n_eq))

        def attend_chunk(c, seen):
            key = key_ref[c]
            eq = key == thr
            eq_f = jnp.where(eq, 1.0, 0.0)
            rank = seen + _dot(earlier_ref[...], eq_f.astype(BF16))
            attend(c, jnp.logical_or(key > thr, jnp.logical_and(eq, rank < keep)), None)
            return seen + jnp.sum(eq_f, axis=0, keepdims=True)

        lax.fori_loop(0, nkc, attend_chunk, jnp.zeros((1, tq), F32))

    for h in range(GROUP_HEADS):
        o_ref[0, :, h * LANES:(h + 1) * LANES] = _normalised(acc_ref, h).astype(o_ref.dtype)


def _attn_dsa(q, k, v, iq, ik, iw, score_bounds, layer, *, tq):
    b, s, _ = q.shape
    topk = min(TOPK_MAX, s // 4)
    earlier = np.asarray(np.arange(tq)[:, None] > np.arange(tq)[None, :], dtype=BF16)
    q_map = lambda bi, qi: (bi, qi, 0)
    all_map = lambda bi, qi: (bi, 0, 0)
    return pl.pallas_call(
        functools.partial(_attn_dsa_kernel, tq=tq, topk=topk, layer=layer),
        grid=(b, s // tq),
        in_specs=[pl.BlockSpec(memory_space=pltpu.SMEM),
                  pl.BlockSpec((1, tq, GROUP_WIDTH), q_map),
                  pl.BlockSpec((1, s, GROUP_WIDTH), all_map),
                  pl.BlockSpec((1, s, GROUP_HEADS * VEXT), all_map),
                  pl.BlockSpec((1, tq, IDX_HEADS * IDX_DIM), q_map),
                  pl.BlockSpec((1, s, 2 * LANES), all_map),
                  pl.BlockSpec((1, tq, LANES), q_map),
                  pl.BlockSpec((tq, tq), lambda bi, qi: (0, 0))],
        out_specs=pl.BlockSpec((1, tq, GROUP_WIDTH), q_map),
        out_shape=jax.ShapeDtypeStruct((b, s, GROUP_WIDTH), BF16),
        scratch_shapes=[pltpu.VMEM((s // tq, tq, tq), jnp.int32),
                        pltpu.VMEM((-(-(s // tq) // COUNT_UNROLL) * COUNT_UNROLL, tq, tq), jnp.int16),
                        pltpu.VMEM((GROUP_HEADS, tq, LANES), F32),
                        pltpu.VMEM((GROUP_HEADS, tq, VEXT), F32)],
        compiler_params=_cparams(2),
        name="attn_dsa",
    )(score_bounds, q, k, v, iq, ik, iw, earlier)


def _rope_tables(seq, period, half):
    inv = np.float32(ROPE_THETA) ** (-np.arange(half, dtype=np.float32) / np.float32(half))
    lane = np.arange(LANES)
    idx = lane % period
    active = idx < half
    ang = np.arange(seq, dtype=np.float32)[:, None] * inv[np.minimum(idx, half - 1)][None, :]
    cos, sin = np.cos(ang), np.sin(ang)
    return np.stack([np.where(active, cos, 1.0), np.where(active, np.where(lane < LANES // 2, -sin, sin), 0.0)]
                    ).astype(np.float32)


def _half_split_layout(dims, half, start=0):
    first = list(range(half)) + list(range(2 * half, 2 * half + (dims - 2 * half) // 2))
    second = list(range(half, 2 * half)) + list(range(2 * half + (dims - 2 * half) // 2, dims))
    lanes = [-1] * LANES
    lanes[start:start + len(first)] = first
    lanes[start + LANES // 2:start + LANES // 2 + len(second)] = second
    return lanes


_HEAD_LAYOUT = _half_split_layout(HEAD_DIM, HEAD_DIM // 8)
_SUB_LAYOUT = _half_split_layout(DIFF_SUB_DIM, DIFF_SUB_DIM // 8)
_PAIR_LAYOUT = [a if a >= 0 else (DIFF_SUB_DIM + b if b >= 0 else -1)
                for a, b in zip(_SUB_LAYOUT, _half_split_layout(DIFF_SUB_DIM, DIFF_SUB_DIM // 8, start=32))]
_ROPE64_LAYOUT = _half_split_layout(MLA_ROPE_DIM, MLA_ROPE_DIM // 2)


def _take_cols(a, layout):
    src = max(layout) + 1
    place = np.zeros((src, LANES), np.float32)
    for lane, i in enumerate(layout):
        if i >= 0:
            place[i, lane] = 1.0
    blocks = a.reshape(a.shape[:-1] + (-1, src))
    precision = lax.Precision.HIGHEST if a.dtype == F32 else None
    out = jnp.einsum("...k,kj->...j", blocks, jnp.asarray(place, a.dtype), precision=precision,
                     preferred_element_type=a.dtype)
    return out.reshape(a.shape[:-1] + (-1,))


def _pad_cols(a, width):
    return jnp.pad(a, ((0, 0),) * (a.ndim - 1) + ((0, width - a.shape[-1]),))


def _relayout_w_in(w):
    sizes = (512,) * 3 + (MLA_Q_RANK, MLA_KV_RANK, MLA_ROPE_DIM) + (512,) * 6 + (IDX_HEADS * IDX_DIM, IDX_DIM, IDX_HEADS)
    names = ("a_q", "a_k", "a_v", "b_cq", "b_ckv", "b_kr", "c_q", "c_k", "c_v", "d_q", "d_k", "d_v", "d_iq", "d_ik", "d_iw")
    w = w.astype(BF16)
    parts, start = {}, 0
    for nme, sz in zip(names, sizes):
        parts[nme] = w[..., start:start + sz]
        start += sz
    for nme in ("a_q", "a_k", "d_iq"):
        parts[nme] = _take_cols(parts[nme], _PAIR_LAYOUT)
    for nme in ("d_q", "d_k"):
        parts[nme] = _take_cols(parts[nme], _HEAD_LAYOUT)
    parts["d_ik"] = _take_cols(parts["d_ik"], _SUB_LAYOUT)
    parts["b_kr"] = _take_cols(parts["b_kr"], _ROPE64_LAYOUT)
    order = sorted(_OFF, key=_OFF.get)
    ends = [_OFF[nme] for nme in order[1:]] + [Y_WIDTH]
    return jnp.concatenate([_pad_cols(parts[nme], end - _OFF[nme]) for nme, end in zip(order, ends)], axis=-1)


def _relayout_mla_q(w):
    lead = w.shape[:-1]
    w = w.reshape(lead + (-1, MLA_QK_DIM))
    rope = _take_cols(w[..., MLA_NOPE_DIM:], _ROPE64_LAYOUT)
    return jnp.concatenate([w[..., :MLA_NOPE_DIM], rope], axis=-1).reshape(lead + (-1,))


_BOUND_COLS = ("diff", "mla", "dsa", "unused")
_GAIN_ROWS = ("a_q", "a_k", "b_cq", "b_ckv", "b_q", "b_k", "d_q", "d_k", "d_ik")


def _score_bounds(diff_qk_norm, mla_qk_norm, dsa_qk_norm):
    def bound(dim, qk_norm):
        peak = jnp.max(jnp.abs(qk_norm), axis=-1)
        return (dim ** 0.5 * LOG2E * SCORE_BOUND_MARGIN) * peak[:, 0] * peak[:, 1]

    cols = [bound(DIFF_SUB_DIM, diff_qk_norm), bound(MLA_QK_DIM, mla_qk_norm), bound(HEAD_DIM, dsa_qk_norm)]
    return jnp.stack(cols + [jnp.zeros_like(cols[0])], axis=1).astype(F32)


def _packed_gains(diff_qk_norm, mla_q_a_norm, mla_kv_a_norm, mla_qk_norm, dsa_qk_norm, idx_k_norm):
    def tiled(g, reps):
        return jnp.tile(g, (1, reps))

    rows = dict(
        a_q=_take_cols(tiled(diff_qk_norm[:, 0], 8), _PAIR_LAYOUT),
        a_k=_take_cols(tiled(diff_qk_norm[:, 1], 8), _PAIR_LAYOUT),
        b_cq=mla_q_a_norm, b_ckv=mla_kv_a_norm,
        b_q=_relayout_mla_q(tiled(mla_qk_norm[:, 0], GROUP_HEADS)),
        b_k=_relayout_mla_q(mla_qk_norm[:, 1]),
        d_q=_take_cols(tiled(dsa_qk_norm[:, 0], 4), _HEAD_LAYOUT),
        d_k=_take_cols(tiled(dsa_qk_norm[:, 1], 4), _HEAD_LAYOUT),
        d_ik=_take_cols(idx_k_norm, _SUB_LAYOUT))
    packed = jnp.stack([_pad_cols(rows[nme], 1024) for nme in _GAIN_ROWS], axis=1)
    return jnp.pad(packed, ((0, 0), (0, 16 - len(_GAIN_ROWS)), (0, 0)))


def kernel(x, attn_norm, w_in, diff_qk_norm, diff_lambda, diff_subln, mla_q_a_norm, mla_wq_b, mla_kv_a_norm,
           mla_wkv_b, mla_qk_norm, dsa_qk_norm, idx_k_norm, w_o, ffn_norm, w_gate, w_up, w_down):
    b, s, d = x.shape
    n = b * s
    depth = w_in.shape[0]
    tm = min(1024, n)
    tq = min(512, s)
    tabs = (_rope_tables(s, LANES // 4, DIFF_SUB_DIM // 8),
            _rope_tables(s, LANES // 2, MLA_ROPE_DIM // 2),
            _rope_tables(s, LANES // 2, HEAD_DIM // 8))
    sub_head = np.arange(GROUP_WIDTH) // LANES * 2 + (np.arange(GROUP_WIDTH) % (LANES // 2)) // (LANES // 4)
    ones64 = np.asarray(sub_head[:, None] == sub_head[None, :], dtype=BF16)
    tk_stick = min(256, s)
    tri = np.asarray(np.arange(tk_stick)[:, None] > np.arange(tk_stick)[None, :], dtype=BF16)

    w_in_r = _relayout_w_in(w_in)
    wq = jnp.pad(_relayout_mla_q(mla_wq_b.astype(BF16)), ((0, 0), (0, 512 - MLA_Q_RANK), (0, 0)))
    wkv = mla_wkv_b.reshape(depth, MLA_KV_RANK, GROUP_HEADS, 2, HEAD_DIM).transpose(0, 1, 3, 2, 4)
    wkv = wkv.reshape(depth, MLA_KV_RANK, 2 * GROUP_WIDTH).astype(BF16)
    gains = _packed_gains(diff_qk_norm, mla_q_a_norm, mla_kv_a_norm, mla_qk_norm, dsa_qk_norm, idx_k_norm)
    bounds = _score_bounds(diff_qk_norm, mla_qk_norm, dsa_qk_norm)
    attn_gain, ffn_gain = attn_norm.reshape(depth, 1, d), ffn_norm.reshape(depth, 1, d)
    subln = diff_subln.reshape(depth, 1, HEAD_DIM)
    w_o_b, w_down_b = w_o.astype(BF16), w_down.astype(BF16)

    xf = x.reshape(n, d)
    for l in range(depth):
        lam_init = 0.8 - 0.6 * math.exp(-0.3 * l)
        y = _rms_matmul(xf, attn_gain, w_in_r, l, tm=tm, tn=Y_WIDTH // 4, out_dtype=F32)
        (qa, ka, va, qb, kb, vb, qc, kc, vc, qd, kd, vd, iq, ik, iw) = [
            a.reshape(b, s, a.shape[1]) for a in _prep(y, s, tabs, ones64, gains, wq, wkv, l, tm=min(256, s))]

        o_a = _resident_attention(
            functools.partial(_attn_diff_kernel, tq=tq, layer=l, lam_init=lam_init), qa, ka, va,
            [], [diff_lambda, subln], l, tq=tq,
            scratch=[pltpu.VMEM((8, tq, LANES), F32), pltpu.VMEM((8, tq, VEXT), F32)], name="attn_diff",
            score_bounds=bounds)
        o_b = _resident_attention(
            functools.partial(_attn_mla_kernel, tq=tq, layer=l), qb, kb, vb, [], [], l, tq=tq,
            scratch=[pltpu.VMEM((4, tq, LANES), F32), pltpu.VMEM((4, tq, VEXT), F32)], name="attn_mla",
            score_bounds=bounds)
        o_c = _resident_attention(
            functools.partial(_attn_stick_kernel, tq=tq, tkc=tk_stick), qc, kc, vc, [tri], [], l, tq=tq,
            scratch=[pltpu.VMEM((4, tq, LANES), F32), pltpu.VMEM((4, tq, LANES), F32)], name="attn_stick")
        o_d = _attn_dsa(qd, kd, vd, iq, ik, iw, bounds, l, tq=tq)

        mixed = [o.reshape(n, GROUP_WIDTH) for o in (o_a, o_b, o_c, o_d)]
        xf = _matmul_residual(mixed, w_o_b, xf, l, tm=min(512, n), tn=d)
        act = _ffn_up(xf, ffn_gain, w_gate, w_up, l, tm=tm, tn=512)
        xf = _matmul_residual([act], w_down_b, xf, l, tm=tm, tn=512)
    return xf.reshape(b, s, d)
```
